```python
import jax, jax.numpy as jnp
from jax import lax
import numpy as np

D_MODEL = 2048
BATCH = 1
SEQ = 8192
DEPTH = 4
DEC_BATCH = 4
DEC_SEQ = 4096
PAST_LEN = 128

N_MIXERS = 2
N_ATT_LAYERS = (DEPTH + 1) // 2
N_RWKV_LAYERS = DEPTH // 2
N_VRES_LAYERS = max(N_RWKV_LAYERS - 1, 0)

ATT_HEAD_DIM = 128
ATT_HEADS = D_MODEL // ATT_HEAD_DIM
ATT_GROUPS = ((128, 1), (512, 4), (2048, 16))
N_ATT_GROUPS = len(ATT_GROUPS)
ATT_WIDTH = ATT_HEADS * ATT_HEAD_DIM
ATT_QKV_COLS = N_ATT_GROUPS * 3 * ATT_WIDTH
ATT_IN_COLS = ATT_QKV_COLS + ATT_WIDTH
ROPE_THETA = 10000.0

RWKV_HEAD_DIM = 64
RWKV_HEADS = D_MODEL // RWKV_HEAD_DIM
DECAY_LORA = 96
ICLR_LORA = 96
VRES_LORA = 64
N_SHIFT_TARGETS = 6
N_DIRS = 2

RMS_EPS = 1e-6
GN_EPS = 64e-5
NEG_INF = -1e30

kernel_name = "hybrid_dilated_attn_rwkv7_bidir_encoder"


def rmsnorm(x, g):
    xf = x.astype(jnp.float32)
    y = xf * lax.rsqrt(jnp.mean(xf * xf, axis=-1, keepdims=True) + RMS_EPS)
    return (y * g.astype(jnp.float32)).astype(x.dtype)


def apply_rope(t):
    S = t.shape[1]
    half = ATT_HEAD_DIM // 2
    inv_freq = 1.0 / (ROPE_THETA ** (jnp.arange(half, dtype=jnp.float32) * 2.0 / ATT_HEAD_DIM))
    ang = jnp.arange(S, dtype=jnp.float32)[:, None] * inv_freq[None, :]
    cos = jnp.cos(ang)[:, None, None, :]
    sin = jnp.sin(ang)[:, None, None, :]
    tf = t.astype(jnp.float32)
    t1, t2 = tf[..., :half], tf[..., half:]
    return jnp.concatenate([t1 * cos - t2 * sin, t2 * cos + t1 * sin], axis=-1).astype(t.dtype)


def banded_attention(q, k, v, half):
    N, L, H, dh = q.shape
    blk = half
    nb = -(-L // blk)
    Lp = nb * blk
    qb = jnp.pad(q, ((0, 0), (0, Lp - L), (0, 0), (0, 0))).reshape(N, nb, blk, H, dh)
    pad_kv = ((0, 0), (blk, Lp - L + blk), (0, 0), (0, 0))

    def bands(t):
        tb = jnp.pad(t, pad_kv).reshape(N, nb + 2, blk, H, dh)
        return jnp.concatenate([tb[:, :-2], tb[:, 1:-1], tb[:, 2:]], axis=2)

    kb, vb = bands(k), bands(v)
    scores = jnp.einsum('nbqhd,nbkhd->nbhqk', qb, kb,
                        preferred_element_type=jnp.float32) * (dh ** -0.5)
    i = jnp.arange(blk)[:, None]
    j = jnp.arange(3 * blk)[None, :]
    rel_ok = (j - i >= 0) & (j - i <= 2 * blk)
    key_pos = jnp.arange(nb)[:, None] * blk - blk + jnp.arange(3 * blk)[None, :]
    key_ok = (key_pos >= 0) & (key_pos < L)
    mask = rel_ok[None, :, :] & key_ok[:, None, :]
    scores = jnp.where(mask[None, :, None], scores, NEG_INF)
    lse = jax.nn.logsumexp(scores, axis=-1)
    p = jnp.exp(scores - lse[..., None])
    o = jnp.einsum('nbhqk,nbkhd->nbqhd', p, vb.astype(jnp.float32))
    o = o.reshape(N, Lp, H, dh)[:, :L]
    lse = jnp.transpose(lse, (0, 1, 3, 2)).reshape(N, Lp, H)[:, :L]
    return o, lse


def dilated_group_attention(q, k, v, window, dilation):
    B, S, H, dh = q.shape
    L = S // dilation
    half = (window // 2) // dilation

    def to_sub(t):
        return jnp.transpose(t.reshape(B, L, dilation, H, dh), (0, 2, 1, 3, 4)).reshape(B * dilation, L, H, dh)

    o, lse = banded_attention(to_sub(q), to_sub(k), to_sub(v), half)
    o = jnp.transpose(o.reshape(B, dilation, L, H, dh), (0, 2, 1, 3, 4)).reshape(B, S, H, dh)
    lse = jnp.transpose(lse.reshape(B, dilation, L, H), (0, 2, 1, 3)).reshape(B, S, H)
    return o, lse


def attention_mixer(h, w_in, w_out):
    B, S, _ = h.shape
    proj = h @ w_in
    qkv = proj[..., :ATT_QKV_COLS].reshape(B, S, N_ATT_GROUPS, 3, ATT_HEADS, ATT_HEAD_DIM)
    z = proj[..., ATT_QKV_COLS:]
    q = apply_rope(qkv[:, :, :, 0])
    k = apply_rope(qkv[:, :, :, 1])
    v = qkv[:, :, :, 2]
    outs, lses = [], []
    for g, (window, dilation) in enumerate(ATT_GROUPS):
        o, l = dilated_group_attention(q[:, :, g], k[:, :, g], v[:, :, g], window, dilation)
        outs.append(o)
        lses.append(l)
    wts = jax.nn.softmax(jnp.stack(lses, axis=0), axis=0)
    o = jnp.sum(wts[..., None] * jnp.stack(outs, axis=0), axis=0)
    y = o.reshape(B, S, ATT_WIDTH).astype(h.dtype) * jax.nn.silu(z)
    return y @ w_out


def heads(t):
    return t.reshape(t.shape[:-1] + (RWKV_HEADS, RWKV_HEAD_DIM))


def rwkv_step(state, inp):
    r, w, k, v, kk, kka = inp
    sa = jnp.einsum('bhvk,bhk->bhv', state, kk)
    state = state * w[:, :, None, :] - sa[..., None] * kka[:, :, None, :] + v[..., None] * k[:, :, None, :]
    y = jnp.einsum('bhvk,bhk->bhv', state, r)
    return state, y


def rwkv_scan(r, w, k, v, kk, kka, reverse):
    B, S, H, N = r.shape
    tm = lambda t: jnp.swapaxes(t, 0, 1)
    state0 = jnp.zeros((B, H, N, N), jnp.float32)
    _, y = lax.scan(rwkv_step, state0, (tm(r), tm(w), tm(k), tm(v), tm(kk), tm(kka)), reverse=reverse)
    return jnp.swapaxes(y, 0, 1)


def rwkv_mixer(h, v_first, vres, mu_prev, mu_next, w_in, w0, w1, w2, a0, a1, a2,
               k_k, k_a, r_k, gn_g, gn_b, w_out):
    B, S, D = h.shape
    f32 = jnp.float32
    x_prev = jnp.pad(h[:, :-1], ((0, 0), (1, 0), (0, 0)))
    x_next = jnp.pad(h[:, 1:], ((0, 0), (0, 1), (0, 0)))
    xs = (h[None] + (x_prev - h)[None] * mu_prev[:, None, None, :]
          + (x_next - h)[None] * mu_next[:, None, None, :])
    rkvz = jnp.einsum('gbsd,dge->gbse', jnp.stack([xs[0], xs[2], xs[3], xs[5]]),
                      w_in.reshape(D, 4, D))
    r, k, v, z = rkvz[0], rkvz[1], rkvz[2], rkvz[3]
    xw, xv, xa = xs[1], xs[3], xs[4]
    wlog = w0[:, None, None, :] + jnp.einsum(
        'cbsr,crd->cbsd', jnp.tanh(jnp.einsum('bsd,cdr->cbsr', xw, w1)), w2)
    decay = jnp.exp(-jnp.exp(-jax.nn.softplus(-wlog.astype(f32)) - 0.5))
    a = jax.nn.sigmoid((a0[:, None, None, :] + jnp.einsum(
        'cbsr,crd->cbsd', jnp.einsum('bsd,cdr->cbsr', xa, a1), a2)).astype(f32))
    if vres is None:
        v_first = v
    else:
        v0, v1, v2 = vres
        v = v + (v_first - v) * jax.nn.sigmoid(v0 + (xv @ v1) @ v2)
    rf, kf, vf = r.astype(f32), k.astype(f32), v.astype(f32)
    kk = heads(kf * k_k)
    kk = kk / jnp.maximum(jnp.sqrt(jnp.sum(kk * kk, axis=-1, keepdims=True)), 1e-12)
    k_dir = kf[None] * (1.0 + (a - 1.0) * k_a)
    rh, vh = heads(rf), heads(vf)
    kdh, ah = heads(k_dir), heads(a)
    y_f = rwkv_scan(rh, heads(decay[0]), kdh[0], vh, kk, kk * ah[0], reverse=False)
    y_b = rwkv_scan(rh, heads(decay[1]), kdh[1], vh, kk, kk * ah[1], reverse=True)
    y = y_f + y_b
    mu = jnp.mean(y, axis=-1, keepdims=True)
    var = jnp.mean(jnp.square(y - mu), axis=-1, keepdims=True)
    y = ((y - mu) * lax.rsqrt(var + GN_EPS)).reshape(B, S, D) * gn_g + gn_b
    bonus = jnp.sum(rh[None] * kdh * r_k[:, None, None], axis=(0, 4))[..., None] * vh
    out = (y + bonus.reshape(B, S, D)).astype(h.dtype) * jax.nn.silu(z)
    return out @ w_out, v_first


def trunk(x, norm_pre, norm_post, att_w_in, att_w_out, rwkv_mu_prev, rwkv_mu_next, rwkv_w_in,
          rwkv_w0, rwkv_w1, rwkv_w2, rwkv_a0, rwkv_a1, rwkv_a2, rwkv_v0, rwkv_v1, rwkv_v2,
          rwkv_k_k, rwkv_k_a, rwkv_r_k, rwkv_gn_g, rwkv_gn_b, rwkv_w_out):
    v_first = None
    for layer in range(DEPTH):
        h = rmsnorm(x, norm_pre[layer])
        j = layer // N_MIXERS
        if layer % N_MIXERS == 0:
            out = attention_mixer(h, att_w_in[j], att_w_out[j])
        else:
            vres = None if j == 0 else (rwkv_v0[j - 1], rwkv_v1[j - 1], rwkv_v2[j - 1])
            out, v_first = rwkv_mixer(h, v_first, vres, rwkv_mu_prev[j], rwkv_mu_next[j], rwkv_w_in[j],
                                      rwkv_w0[j], rwkv_w1[j], rwkv_w2[j], rwkv_a0[j], rwkv_a1[j], rwkv_a2[j],
                                      rwkv_k_k[j], rwkv_k_a[j], rwkv_r_k[j], rwkv_gn_g[j], rwkv_gn_b[j],
                                      rwkv_w_out[j])
        x = x + rmsnorm(out, norm_post[layer])
    return x


def setup_inputs(seed: int = 0) -> dict:
    key = jax.random.key(seed)
    ks = jax.random.split(key, 24)
    f32 = jnp.float32
    D, H, N = D_MODEL, RWKV_HEADS, RWKV_HEAD_DIM
    NA, NR, NV = N_ATT_LAYERS, N_RWKV_LAYERS, N_VRES_LAYERS
    nrm = lambda k, shape, scale: scale * jax.random.normal(k, shape, f32)
    uni = lambda k, shape, lo, hi: jax.random.uniform(k, shape, f32, lo, hi)
    return {
        "x_prompt": nrm(ks[0], (BATCH, SEQ, D), 1.0),
        "x_sample": nrm(ks[1], (DEC_BATCH, DEC_SEQ, D), 1.0),
        "norm_pre": 1.0 + nrm(ks[2], (DEPTH, D), 0.02),
        "norm_post": 1.0 + nrm(ks[3], (DEPTH, D), 0.02),
        "att_w_in": nrm(ks[4], (NA, D, ATT_IN_COLS), D ** -0.5),
        "att_w_out": nrm(ks[5], (NA, ATT_WIDTH, D), ATT_WIDTH ** -0.5),
        "rwkv_mu_prev": uni(ks[6], (NR, N_SHIFT_TARGETS, D), 0.0, 0.5),
        "rwkv_mu_next": uni(ks[7], (NR, N_SHIFT_TARGETS, D), 0.0, 0.5),
        "rwkv_w_in": nrm(ks[8], (NR, D, 4 * D), D ** -0.5),
        "rwkv_w0": uni(ks[9], (NR, N_DIRS, D), -4.0, 0.0),
        "rwkv_w1": nrm(ks[10], (NR, N_DIRS, D, DECAY_LORA), D ** -0.5),
        "rwkv_w2": nrm(ks[11], (NR, N_DIRS, DECAY_LORA, D), 0.5 * DECAY_LORA ** -0.5),
        "rwkv_a0": nrm(ks[12], (NR, N_DIRS, D), 0.5),
        "rwkv_a1": nrm(ks[13], (NR, N_DIRS, D, ICLR_LORA), D ** -0.5),
        "rwkv_a2": nrm(ks[14], (NR, N_DIRS, ICLR_LORA, D), 0.5 * ICLR_LORA ** -0.5),
        "rwkv_v0": nrm(ks[15], (NV, D), 0.5),
        "rwkv_v1": nrm(ks[16], (NV, D, VRES_LORA), D ** -0.5),
        "rwkv_v2": nrm(ks[17], (NV, VRES_LORA, D), 0.5 * VRES_LORA ** -0.5),
        "rwkv_k_k": 0.85 + nrm(ks[18], (NR, D), 0.1),
        "rwkv_k_a": 1.0 + nrm(ks[19], (NR, D), 0.1),
        "rwkv_r_k": nrm(ks[20], (NR, N_DIRS, H, N), 0.1),
        "rwkv_gn_g": 1.0 + nrm(ks[21], (NR, D), 0.02),
        "rwkv_gn_b": nrm(ks[22], (NR, D), 0.02),
        "rwkv_w_out": nrm(ks[23], (NR, D, D), D ** -0.5),
    }


def reference(x_prompt, x_sample, norm_pre, norm_post, att_w_in, att_w_out, rwkv_mu_prev, rwkv_mu_next,
              rwkv_w_in, rwkv_w0, rwkv_w1, rwkv_w2, rwkv_a0, rwkv_a1, rwkv_a2, rwkv_v0, rwkv_v1, rwkv_v2,
              rwkv_k_k, rwkv_k_a, rwkv_r_k, rwkv_gn_g, rwkv_gn_b, rwkv_w_out):
    y_prompt = trunk(x_prompt, norm_pre, norm_post, att_w_in, att_w_out, rwkv_mu_prev, rwkv_mu_next,
                     rwkv_w_in, rwkv_w0, rwkv_w1, rwkv_w2, rwkv_a0, rwkv_a1, rwkv_a2, rwkv_v0, rwkv_v1,
                     rwkv_v2, rwkv_k_k, rwkv_k_a, rwkv_r_k, rwkv_gn_g, rwkv_gn_b, rwkv_w_out)
    y_sample = trunk(x_sample, norm_pre, norm_post, att_w_in, att_w_out, rwkv_mu_prev, rwkv_mu_next,
                     rwkv_w_in, rwkv_w0, rwkv_w1, rwkv_w2, rwkv_a0, rwkv_a1, rwkv_a2, rwkv_v0, rwkv_v1,
                     rwkv_v2, rwkv_k_k, rwkv_k_a, rwkv_r_k, rwkv_gn_g, rwkv_gn_b, rwkv_w_out)
    return (y_prompt, y_sample)
```

```python
import functools
import math

import jax
import jax.numpy as jnp
from jax import lax
from jax.experimental import pallas as pl
from jax.experimental.pallas import tpu as pltpu

F32 = jnp.float32
BF16 = jnp.bfloat16

D_MODEL = 2048
LANES = 128
ATT_HEAD_DIM = 128
ATT_HEADS = D_MODEL // ATT_HEAD_DIM
ATT_GROUPS = ((128, 1), (512, 4), (2048, 16))
ATT_HALF = 64
ATT_IN_COLS = 3 * 3 * D_MODEL + D_MODEL
ROPE_THETA = 10000.0
RWKV_HEAD_DIM = 64
LORA_PAD = 128
RMS_EPS = 1e-6
GN_EPS = 64e-5
NEG_INF = -1e30
CHUNK = 64
VMEM_LIMIT_BYTES = 56 * 1024 * 1024

NT_DIMS = (((1,), (1,)), ((), ()))
TN_DIMS = (((0,), (0,)), ((), ()))


def _cparams(*sem):
    return pltpu.CompilerParams(dimension_semantics=sem, vmem_limit_bytes=VMEM_LIMIT_BYTES)


def _rms_scale(x):
    return lax.rsqrt(jnp.mean(x * x, axis=-1, keepdims=True) + RMS_EPS)


def _sigmoid(x):
    return 1.0 / (1.0 + jnp.exp(-x))


def _group_sum(x, gmat):
    hi = x.astype(BF16)
    lo = (x - hi.astype(F32)).astype(BF16)
    return (jnp.dot(hi, gmat, preferred_element_type=F32)
            + jnp.dot(lo, gmat, preferred_element_type=F32))


def _group_matrix(n):
    r = lax.broadcasted_iota(jnp.int32, (n, n), 0) // RWKV_HEAD_DIM
    c = lax.broadcasted_iota(jnp.int32, (n, n), 1) // RWKV_HEAD_DIM
    return jnp.where(r == c, 1.0, 0.0).astype(BF16)


def _rope_table_kernel(invf_ref, cos_ref, sin_ref):
    rows = cos_ref.shape[0]
    base = pl.program_id(0) * rows
    pos = (base + lax.broadcasted_iota(jnp.int32, (rows, LANES), 0)).astype(F32)
    ang = pos * invf_ref[...]
    lane = lax.broadcasted_iota(jnp.int32, (rows, LANES), 1)
    s = jnp.sin(ang)
    cos_ref[...] = jnp.cos(ang)
    sin_ref[...] = jnp.where(lane < ATT_HEAD_DIM // 2, -s, s)


def _rope_tables(seq):
    half = ATT_HEAD_DIM // 2
    inv_freq = 1.0 / (ROPE_THETA ** (jnp.arange(half, dtype=F32) * 2.0 / ATT_HEAD_DIM))
    invf = jnp.concatenate([inv_freq, inv_freq])[None, :]
    rows = 512
    return pl.pallas_call(
        _rope_table_kernel,
        grid=(seq // rows,),
        in_specs=[pl.BlockSpec((1, LANES), lambda i: (0, 0))],
        out_specs=[pl.BlockSpec((rows, LANES), lambda i: (i, 0))] * 2,
        out_shape=[jax.ShapeDtypeStruct((seq, LANES), F32)] * 2,
        compiler_params=_cparams("arbitrary"),
        name="rope_table",
    )(invf)


def _att_in_kernel(x_ref, g_ref, w_ref, cos_ref, sin_ref, o_ref, h_ref, *, tn, scale):
    j = pl.program_id(1)

    @pl.when(j == 0)
    def _():
        x = x_ref[...]
        h_ref[...] = (x * _rms_scale(x) * g_ref[...]).astype(BF16)

    acc = jnp.dot(h_ref[...], w_ref[...], preferred_element_type=F32)
    blk = j // (D_MODEL // tn)
    is_qk = jnp.logical_and(blk < 9, blk % 3 < 2)
    is_q = jnp.logical_and(blk < 9, blk % 3 == 0)

    @pl.when(is_qk)
    def _():
        cos = cos_ref[...]
        sin = sin_ref[...]
        sc = jnp.where(is_q, scale, 1.0).astype(F32)
        for hh in range(tn // LANES):
            sl = slice(hh * LANES, (hh + 1) * LANES)
            t = acc[:, sl]
            rt = pltpu.roll(t, ATT_HEAD_DIM // 2, axis=1)
            o_ref[:, sl] = ((t * cos + rt * sin) * sc).astype(BF16)

    @pl.when(jnp.logical_not(is_qk))
    def _():
        o_ref[...] = acc.astype(BF16)


def _att_in(x2, g, w_bf, cos_t, sin_t, seq):
    rows = x2.shape[0]
    tm, tn = 512, 1024
    tiles_per_seq = seq // tm
    kern = functools.partial(_att_in_kernel, tn=tn, scale=ATT_HEAD_DIM ** -0.5)
    return pl.pallas_call(
        kern,
        grid=(rows // tm, ATT_IN_COLS // tn),
        in_specs=[
            pl.BlockSpec((tm, D_MODEL), lambda i, j: (i, 0)),
            pl.BlockSpec((1, D_MODEL), lambda i, j: (0, 0)),
            pl.BlockSpec((D_MODEL, tn), lambda i, j: (0, j)),
            pl.BlockSpec((tm, LANES), lambda i, j: (i % tiles_per_seq, 0)),
            pl.BlockSpec((tm, LANES), lambda i, j: (i % tiles_per_seq, 0)),
        ],
        out_specs=pl.BlockSpec((tm, tn), lambda i, j: (i, j)),
        out_shape=jax.ShapeDtypeStruct((rows, ATT_IN_COLS), BF16),
        scratch_shapes=[pltpu.VMEM((tm, D_MODEL), BF16)],
        compiler_params=_cparams("parallel", "arbitrary"),
        name="att_in",
    )(x2, g, w_bf, cos_t, sin_t)


def _attn_kernel(q_ref, kp_ref, kc_ref, kn_ref, vp_ref, vc_ref, vn_ref, o_ref, lse_ref, *, bq, sub_len, nh):
    i = pl.program_id(3)
    nk = bq + 2 * ATT_HALF
    ii = lax.broadcasted_iota(jnp.int32, (bq, nk), 0)
    jj = lax.broadcasted_iota(jnp.int32, (bq, nk), 1)
    rel = jj - ii
    kpos = i * bq - ATT_HALF + jj
    valid = (rel >= 0) & (rel <= 2 * ATT_HALF) & (kpos >= 0) & (kpos < sub_len)
    for hh in range(nh):
        sl = slice(hh * LANES, (hh + 1) * LANES)
        q = q_ref[:, sl]
        kcat = jnp.concatenate([kp_ref[:, sl], kc_ref[:, sl], kn_ref[:, sl]], axis=0)
        vcat = jnp.concatenate([vp_ref[:, sl], vc_ref[:, sl], vn_ref[:, sl]], axis=0)
        s = lax.dot_general(q, kcat, NT_DIMS, preferred_element_type=F32)
        s = jnp.where(valid, s, NEG_INF)
        m = jnp.max(s, axis=-1, keepdims=True)
        p = jnp.exp(s - m)
        l = jnp.sum(p, axis=-1, keepdims=True)
        o = jnp.dot(p.astype(BF16), vcat, preferred_element_type=F32)
        o_ref[:, sl] = o / l
        lse_ref[:, sl] = jnp.broadcast_to(m + jnp.log(l), (bq, LANES))


def _attn_group(proj, gidx, dilation, batch, seq):
    sub_len = seq // dilation
    pv = proj.reshape(batch, sub_len, dilation * ATT_IN_COLS)
    bq = min(256, sub_len)
    hw = 1024
    nh = hw // LANES
    hblocks = D_MODEL // hw
    cols_in = ATT_IN_COLS // hw
    halo_per_q = bq // ATT_HALF
    n_halo = sub_len // ATT_HALF
    qb = (gidx * 3 + 0) * hblocks
    kb = (gidx * 3 + 1) * hblocks
    vb = (gidx * 3 + 2) * hblocks

    def cur(base):
        return pl.BlockSpec((None, bq, hw), lambda b, r, h, i: (b, i, r * cols_in + base + h))

    def prev(base):
        return pl.BlockSpec((None, ATT_HALF, hw),
                            lambda b, r, h, i: (b, jnp.maximum(i * halo_per_q - 1, 0), r * cols_in + base + h))

    def nxt(base):
        return pl.BlockSpec((None, ATT_HALF, hw),
                            lambda b, r, h, i: (b, jnp.minimum((i + 1) * halo_per_q, n_halo - 1),
                                                r * cols_in + base + h))

    out_spec = pl.BlockSpec((None, bq, hw), lambda b, r, h, i: (b, i, r * hblocks + h))
    out_sds = jax.ShapeDtypeStruct((batch, sub_len, dilation * D_MODEL), F32)
    kern = functools.partial(_attn_kernel, bq=bq, sub_len=sub_len, nh=nh)
    o, lse = pl.pallas_call(
        kern,
        grid=(batch, dilation, hblocks, sub_len // bq),
        in_specs=[cur(qb), prev(kb), cur(kb), nxt(kb), prev(vb), cur(vb), nxt(vb)],
        out_specs=[out_spec, out_spec],
        out_shape=[out_sds, out_sds],
        compiler_params=_cparams("parallel", "parallel", "parallel", "arbitrary"),
        name=f"attn_d{dilation}",
    )(pv, pv, pv, pv, pv, pv, pv)
    return o.reshape(batch * seq, D_MODEL), lse.reshape(batch * seq, D_MODEL)


def _proj_norm_residual(y_bf, w_ref, g_ref, x_ref, o_ref):
    out = jnp.dot(y_bf, w_ref[...], preferred_element_type=F32)
    o_ref[...] = x_ref[...] + out * _rms_scale(out) * g_ref[...]


def _att_out_kernel(o0_ref, o1_ref, o2_ref, l0_ref, l1_ref, l2_ref, z_ref, x_ref, w_ref, g_ref, o_ref):
    l0, l1, l2 = l0_ref[...], l1_ref[...], l2_ref[...]
    m = jnp.maximum(jnp.maximum(l0, l1), l2)
    e0, e1, e2 = jnp.exp(l0 - m), jnp.exp(l1 - m), jnp.exp(l2 - m)
    o = (e0 * o0_ref[...] + e1 * o1_ref[...] + e2 * o2_ref[...]) / (e0 + e1 + e2)
    z = z_ref[...].astype(F32)
    y = (o * (z * _sigmoid(z))).astype(BF16)
    _proj_norm_residual(y, w_ref, g_ref, x_ref, o_ref)


def _att_out(outs, lses, proj, x2, w_bf, g):
    rows = x2.shape[0]
    tm = 256
    row_spec = pl.BlockSpec((tm, D_MODEL), lambda i: (i, 0))
    return pl.pallas_call(
        _att_out_kernel,
        grid=(rows // tm,),
        in_specs=[row_spec] * 6 + [
            pl.BlockSpec((tm, D_MODEL), lambda i: (i, ATT_IN_COLS // D_MODEL - 1)),
            row_spec,
            pl.BlockSpec((D_MODEL, D_MODEL), lambda i: (0, 0)),
            pl.BlockSpec((1, D_MODEL), lambda i: (0, 0)),
        ],
        out_specs=row_spec,
        out_shape=jax.ShapeDtypeStruct((rows, D_MODEL), F32),
        compiler_params=_cparams("parallel"),
        name="att_out",
    )(*outs, *lses, proj, x2, w_bf, g)


def _rwkv_in_kernel(*refs, tm, tn, tiles_per_seq, has_vres):
    (x_ref, xp_ref, xn_ref, g_ref, mup_ref, mun_ref,
     wr_ref, wk_ref, wv_ref, wz_ref,
     w1_ref, w2_ref, w0_ref, a1_ref, a2_ref, a0_ref,
     kk_ref, ka_ref, rk_ref) = refs[:19]
    pos = 19
    if has_vres:
        v1_ref, v2_ref, v0_ref, vf_ref = refs[pos:pos + 4]
        pos += 4
    (r_out, v_out, kk_out, lw_out, kd_out, a_out, bonus_out, sz_out) = refs[pos:pos + 8]
    pos += 8
    xs_ref, hw_ref, ha_ref = refs[pos:pos + 3]
    hv_ref = refs[pos + 3] if has_vres else None

    i = pl.program_id(0)
    j = pl.program_id(1)

    @pl.when(j == 0)
    def _():
        t_in_seq = i % tiles_per_seq
        keep_prev = jnp.where(t_in_seq == 0, 0.0, 1.0).astype(F32)
        keep_next = jnp.where(t_in_seq == tiles_per_seq - 1, 0.0, 1.0).astype(F32)
        sx = _rms_scale(x_ref[...])
        xp = xp_ref[7:8, :]
        xn = xn_ref[0:1, :]
        sp = _rms_scale(xp) * keep_prev
        sn = _rms_scale(xn) * keep_next
        cw = 512
        row = lax.broadcasted_iota(jnp.int32, (tm, cw), 0)
        for cb in range(D_MODEL // cw):
            sl = slice(cb * cw, (cb + 1) * cw)
            g = g_ref[:, sl]
            h = x_ref[:, sl] * sx * g
            hp_row = xp[:, sl] * sp * g
            hn_row = xn[:, sl] * sn * g
            h_prev = jnp.where(row == 0, hp_row, pltpu.roll(h, 1, axis=0))
            h_next = jnp.where(row == tm - 1, hn_row, pltpu.roll(h, tm - 1, axis=0))
            dp = h_prev - h
            dn = h_next - h
            for t in range(6):
                xs_ref[t, :, sl] = (h + dp * mup_ref[t:t + 1, sl] + dn * mun_ref[t:t + 1, sl]).astype(BF16)
        for c in range(2):
            hw_ref[c] = jnp.tanh(jnp.dot(xs_ref[1], w1_ref[c], preferred_element_type=F32)).astype(BF16)
            ha_ref[c] = jnp.dot(xs_ref[4], a1_ref[c], preferred_element_type=F32).astype(BF16)
        if has_vres:
            hv_ref[...] = jnp.dot(xs_ref[3], v1_ref[...], preferred_element_type=F32).astype(BF16)

    r = jnp.dot(xs_ref[0], wr_ref[...], preferred_element_type=F32)
    k = jnp.dot(xs_ref[2], wk_ref[...], preferred_element_type=F32)
    v = jnp.dot(xs_ref[3], wv_ref[...], preferred_element_type=F32)
    z = jnp.dot(xs_ref[5], wz_ref[...], preferred_element_type=F32)
    if has_vres:
        gate = _sigmoid(v0_ref[...] + jnp.dot(hv_ref[...], v2_ref[...], preferred_element_type=F32))
        v = v + (vf_ref[...] - v) * gate
    gmat = _group_matrix(tn)
    kk = k * kk_ref[...]
    nrm = jnp.sqrt(_group_sum(kk * kk, gmat))
    kk = kk / jnp.maximum(nrm, 1e-12)
    k_a = ka_ref[...]
    rk_acc = jnp.zeros((tm, tn), F32)
    for c in range(2):
        wl = w0_ref[c:c + 1, :] + jnp.dot(hw_ref[c], w2_ref[c], preferred_element_type=F32)
        lw_out[c] = -math.exp(-0.5) * _sigmoid(wl)
        a = _sigmoid(a0_ref[c:c + 1, :] + jnp.dot(ha_ref[c], a2_ref[c], preferred_element_type=F32))
        kd = k * (1.0 + (a - 1.0) * k_a)
        a_out[c] = a
        kd_out[c] = kd
        rk_acc = rk_acc + r * kd * rk_ref[c:c + 1, :]
    r_out[...] = r
    v_out[...] = v
    kk_out[...] = kk
    bonus_out[...] = _group_sum(rk_acc, gmat) * v
    sz_out[...] = (z * _sigmoid(z)).astype(BF16)


def _pad_lora(w1, w2):
    rank = w1.shape[-1]
    pad1 = [(0, 0)] * (w1.ndim - 1) + [(0, LORA_PAD - rank)]
    pad2 = [(0, 0)] * (w2.ndim - 2) + [(0, LORA_PAD - rank), (0, 0)]
    return jnp.pad(w1, pad1).astype(BF16), jnp.pad(w2, pad2).astype(BF16)


def _rwkv_in(x2, seq, g, mu_prev, mu_next, w_in_bf, w0, w1, w2, a0, a1, a2, k_k, k_a, r_k, vres, v_first):
    rows = x2.shape[0]
    tm, tn = 256, 256
    tiles_per_seq = seq // tm
    ncol = D_MODEL // tn
    has_vres = vres is not None
    w1p, w2p = _pad_lora(w1, w2)
    a1p, a2p = _pad_lora(a1, a2)
    sub = tm // 8
    nsub = rows // 8

    def const2(shape):
        return pl.BlockSpec(shape, lambda i, j: (0, 0))

    def col2(nrow):
        return pl.BlockSpec((nrow, tn), lambda i, j: (0, j))

    def wcol(gi):
        return pl.BlockSpec((D_MODEL, tn), lambda i, j: (0, gi * ncol + j))

    in_specs = [
        pl.BlockSpec((tm, D_MODEL), lambda i, j: (i, 0)),
        pl.BlockSpec((8, D_MODEL), lambda i, j: (jnp.maximum(i * sub - 1, 0), 0)),
        pl.BlockSpec((8, D_MODEL), lambda i, j: (jnp.minimum((i + 1) * sub, nsub - 1), 0)),
        const2((1, D_MODEL)), const2((6, D_MODEL)), const2((6, D_MODEL)),
        wcol(0), wcol(1), wcol(2), wcol(3),
        pl.BlockSpec((2, D_MODEL, LORA_PAD), lambda i, j: (0, 0, 0)),
        pl.BlockSpec((2, LORA_PAD, tn), lambda i, j: (0, 0, j)),
        col2(2),
        pl.BlockSpec((2, D_MODEL, LORA_PAD), lambda i, j: (0, 0, 0)),
        pl.BlockSpec((2, LORA_PAD, tn), lambda i, j: (0, 0, j)),
        col2(2),
        col2(1), col2(1), col2(2),
    ]
    args = [x2, x2, x2, g, mu_prev, mu_next, w_in_bf, w_in_bf, w_in_bf, w_in_bf,
            w1p, w2p, w0, a1p, a2p, a0, k_k, k_a, r_k]
    if has_vres:
        v0, v1, v2 = vres
        v1p, v2p = _pad_lora(v1, v2)
        in_specs += [const2((D_MODEL, LORA_PAD)), col2(LORA_PAD), col2(1),
                     pl.BlockSpec((tm, tn), lambda i, j: (i, j))]
        args += [v1p, v2p, v0, v_first]

    tile = pl.BlockSpec((tm, tn), lambda i, j: (i, j))
    tile2 = pl.BlockSpec((2, tm, tn), lambda i, j: (0, i, j))
    sds = jax.ShapeDtypeStruct((rows, D_MODEL), F32)
    sds2 = jax.ShapeDtypeStruct((2, rows, D_MODEL), F32)
    scratch = [pltpu.VMEM((6, tm, D_MODEL), BF16), pltpu.VMEM((2, tm, LORA_PAD), BF16),
               pltpu.VMEM((2, tm, LORA_PAD), BF16)]
    if has_vres:
        scratch.append(pltpu.VMEM((tm, LORA_PAD), BF16))
    kern = functools.partial(_rwkv_in_kernel, tm=tm, tn=tn, tiles_per_seq=tiles_per_seq, has_vres=has_vres)
    return pl.pallas_call(
        kern,
        grid=(rows // tm, ncol),
        in_specs=in_specs,
        out_specs=[tile, tile, tile, tile2, tile2, tile2, tile, tile],
        out_shape=[sds, sds, sds, sds2, sds2, sds2, sds, jax.ShapeDtypeStruct((rows, D_MODEL), BF16)],
        scratch_shapes=scratch,
        compiler_params=_cparams("parallel", "arbitrary"),
        name="rwkv_in_vres" if has_vres else "rwkv_in",
    )(*args)


def _block_diag(y, first_half):
    zero = jnp.zeros_like(y)
    return jnp.concatenate([jnp.where(first_half, y, zero), jnp.where(first_half, zero, y)], axis=0).astype(BF16)


def _scan_chunk(r, v, kk, lw, kd, a, state, masks):
    cum_mat, strict, incl, eye, first_half, bd_mask = masks

    def pair_mm(x, y):
        return jnp.dot(x.astype(BF16), _block_diag(y, first_half), preferred_element_type=F32)

    g = jnp.dot(cum_mat, lw, preferred_element_type=F32, precision=lax.Precision.HIGHEST)
    g_tot = jnp.sum(lw, axis=0, keepdims=True)
    e_q = jnp.exp(g)
    e_qp = jnp.exp(g - lw)
    e_k = jnp.exp(-g)
    e_end = jnp.exp(g_tot - g)
    kka = kk * a
    q2 = jnp.concatenate([kk * e_qp, r * e_q], axis=0).astype(BF16)
    xt = jnp.concatenate([_block_diag(kka * e_k, first_half), _block_diag(kd * e_k, first_half)], axis=0)
    gram = lax.dot_general(q2, xt, NT_DIMS, preferred_element_type=F32)
    a_mat = jnp.where(strict, gram[0:CHUNK, 0:LANES], 0.0)
    ak = jnp.where(strict, gram[0:CHUNK, LANES:2 * LANES], 0.0)
    bra = jnp.where(incl, gram[CHUNK:2 * CHUNK, 0:LANES], 0.0)
    brk = jnp.where(incl, gram[CHUNK:2 * CHUNK, LANES:2 * LANES], 0.0)
    inv = eye - a_mat
    a_pow = a_mat
    for _ in range(5):
        a_pow = pair_mm(a_pow, a_pow)
        inv = inv + pair_mm(inv, a_pow)
    uy0 = lax.dot_general(q2, state.astype(BF16), NT_DIMS, preferred_element_type=F32)
    u = pair_mm(inv, uy0[0:CHUNK] + pair_mm(ak, v))
    y = uy0[CHUNK:2 * CHUNK] + jnp.dot(
        jnp.concatenate([brk, -bra], axis=1).astype(BF16),
        jnp.concatenate([_block_diag(v, first_half), _block_diag(u, first_half)], axis=0),
        preferred_element_type=F32)
    upd = lax.dot_general(jnp.concatenate([v, u], axis=0).astype(BF16),
                          jnp.concatenate([kd * e_end, -(kka * e_end)], axis=0).astype(BF16),
                          TN_DIMS, preferred_element_type=F32)
    new_state = jnp.where(bd_mask, state * jnp.exp(g_tot) + upd, 0.0)
    return y, new_state


def _rwkv_scan_kernel(r_ref, v_ref, kk_ref, lw_ref, kd_ref, a_ref, y_ref, state_ref, *, tb, lw_lanes):
    rev = pl.program_id(1) == 1
    step = pl.program_id(3)
    npair = lw_lanes // LANES
    nchunk = tb // CHUNK

    @pl.when(step == 0)
    def _():
        state_ref[...] = jnp.zeros_like(state_ref)

    t = lax.broadcasted_iota(jnp.int32, (CHUNK, LANES), 0)
    lane = lax.broadcasted_iota(jnp.int32, (CHUNK, LANES), 1)
    s = lane & (CHUNK - 1)
    d = jnp.where(rev, s - t, t - s)
    strict = d > 0
    incl = d >= 0
    eye = jnp.where(d == 0, 1.0, 0.0).astype(F32)
    first_half = lane < CHUNK
    tt = lax.broadcasted_iota(jnp.int32, (CHUNK, CHUNK), 0)
    ss = lax.broadcasted_iota(jnp.int32, (CHUNK, CHUNK), 1)
    cum_mat = jnp.where(jnp.where(rev, ss - tt, tt - ss) >= 0, 1.0, 0.0).astype(F32)
    rr = lax.broadcasted_iota(jnp.int32, (LANES, LANES), 0)
    cc = lax.broadcasted_iota(jnp.int32, (LANES, LANES), 1)
    bd_mask = (rr < CHUNK) == (cc < CHUNK)
    masks = (cum_mat, strict, incl, eye, first_half, bd_mask)

    def body(ci, carry):
        c = jnp.where(rev, nchunk - 1 - ci, ci)
        rows = pl.ds(pl.multiple_of(c * CHUNK, CHUNK), CHUNK)
        for p in range(npair):
            sl = slice(p * LANES, (p + 1) * LANES)
            y, new_state = _scan_chunk(r_ref[rows, sl], v_ref[rows, sl], kk_ref[rows, sl], lw_ref[rows, sl],
                                       kd_ref[rows, sl], a_ref[rows, sl], state_ref[p], masks)
            y_ref[rows, sl] = y
            state_ref[p] = new_state
        return carry

    lax.fori_loop(0, nchunk, body, 0)


def _rwkv_scan(r, v, kk, lw, kd, a, batch, seq):
    tb, lw_lanes = 512, 512
    nt = seq // tb
    shared = lambda t: t.reshape(batch, seq, D_MODEL)
    direc = lambda t: t.reshape(2, batch, seq, D_MODEL)

    def tmap(i, dr):
        return i + dr * (nt - 1 - 2 * i)

    s_spec = pl.BlockSpec((None, tb, lw_lanes), lambda b, dr, h, i: (b, tmap(i, dr), h))
    d_spec = pl.BlockSpec((None, None, tb, lw_lanes), lambda b, dr, h, i: (dr, b, tmap(i, dr), h))
    kern = functools.partial(_rwkv_scan_kernel, tb=tb, lw_lanes=lw_lanes)
    y = pl.pallas_call(
        kern,
        grid=(batch, 2, D_MODEL // lw_lanes, nt),
        in_specs=[s_spec, s_spec, s_spec, d_spec, d_spec, d_spec],
        out_specs=d_spec,
        out_shape=jax.ShapeDtypeStruct((2, batch, seq, D_MODEL), F32),
        scratch_shapes=[pltpu.VMEM((lw_lanes // LANES, LANES, LANES), F32)],
        compiler_params=_cparams("parallel", "parallel", "parallel", "arbitrary"),
        name="rwkv_scan",
    )(shared(r), shared(v), shared(kk), direc(lw), direc(kd), direc(a))
    return y.reshape(2, batch * seq, D_MODEL)


def _rwkv_out_kernel(y_ref, bonus_ref, sz_ref, gg_ref, gb_ref, x_ref, w_ref, g_ref, o_ref, yb_ref):
    gmat = _group_matrix(LANES)
    inv_n = 1.0 / RWKV_HEAD_DIM
    for cb in range(D_MODEL // LANES):
        sl = slice(cb * LANES, (cb + 1) * LANES)
        y = y_ref[0, :, sl] + y_ref[1, :, sl]
        mu = _group_sum(y, gmat) * inv_n
        yc = y - mu
        var = _group_sum(yc * yc, gmat) * inv_n
        yn = yc * lax.rsqrt(var + GN_EPS) * gg_ref[:, sl] + gb_ref[:, sl]
        yb_ref[:, sl] = ((yn + bonus_ref[:, sl]) * sz_ref[:, sl].astype(F32)).astype(BF16)
    _proj_norm_residual(yb_ref[...], w_ref, g_ref, x_ref, o_ref)


def _rwkv_out(y2, bonus, sz, gn_g, gn_b, x2, w_bf, g):
    rows = x2.shape[0]
    tm = 256
    row_spec = pl.BlockSpec((tm, D_MODEL), lambda i: (i, 0))
    vec = pl.BlockSpec((1, D_MODEL), lambda i: (0, 0))
    return pl.pallas_call(
        _rwkv_out_kernel,
        grid=(rows // tm,),
        in_specs=[pl.BlockSpec((2, tm, D_MODEL), lambda i: (0, i, 0)), row_spec, row_spec, vec, vec, row_spec,
                  pl.BlockSpec((D_MODEL, D_MODEL), lambda i: (0, 0)), vec],
        out_specs=row_spec,
        out_shape=jax.ShapeDtypeStruct((rows, D_MODEL), F32),
        scratch_shapes=[pltpu.VMEM((tm, D_MODEL), BF16)],
        compiler_params=_cparams("parallel"),
        name="rwkv_out",
    )(y2, bonus, sz, gn_g, gn_b, x2, w_bf, g)


def _trunk(x, p, rope):
    batch, seq, _ = x.shape
    x2 = x.reshape(batch * seq, D_MODEL)
    cos_t, sin_t = rope
    v_first = None
    depth = p["norm_pre"].shape[0]
    for layer in range(depth):
        j = layer // 2
        g_pre = p["norm_pre"][layer][None, :]
        g_post = p["norm_post"][layer][None, :]
        if layer % 2 == 0:
            proj = _att_in(x2, g_pre, p["att_w_in"][j], cos_t, sin_t, seq)
            outs, lses = [], []
            for gidx, (_, dilation) in enumerate(ATT_GROUPS):
                o, lse = _attn_group(proj, gidx, dilation, batch, seq)
                outs.append(o)
                lses.append(lse)
            x2 = _att_out(outs, lses, proj, x2, p["att_w_out"][j], g_post)
        else:
            vres = None if j == 0 else (p["rwkv_v0"][j - 1][None, :], p["rwkv_v1"][j - 1], p["rwkv_v2"][j - 1])
            r, v, kk, lw, kd, a, bonus, sz = _rwkv_in(
                x2, seq, g_pre, p["rwkv_mu_prev"][j], p["rwkv_mu_next"][j], p["rwkv_w_in"][j],
                p["rwkv_w0"][j], p["rwkv_w1"][j], p["rwkv_w2"][j],
                p["rwkv_a0"][j], p["rwkv_a1"][j], p["rwkv_a2"][j],
                p["rwkv_k_k"][j][None, :], p["rwkv_k_a"][j][None, :], p["rwkv_r_k"][j].reshape(2, D_MODEL),
                vres, v_first)
            if j == 0:
                v_first = v
            y2 = _rwkv_scan(r, v, kk, lw, kd, a, batch, seq)
            x2 = _rwkv_out(y2, bonus, sz, p["rwkv_gn_g"][j][None, :], p["rwkv_gn_b"][j][None, :], x2,
                           p["rwkv_w_out"][j], g_post)
    return x2.reshape(batch, seq, D_MODEL)


def kernel(x_prompt, x_sample, norm_pre, norm_post, att_w_in, att_w_out, rwkv_mu_prev, rwkv_mu_next, rwkv_w_in, rwkv_w0, rwkv_w1, rwkv_w2, rwkv_a0, rwkv_a1, rwkv_a2, rwkv_v0, rwkv_v1, rwkv_v2, rwkv_k_k, rwkv_k_a, rwkv_r_k, rwkv_gn_g, rwkv_gn_b, rwkv_w_out):
    p = dict(
        norm_pre=norm_pre, norm_post=norm_post,
        att_w_in=att_w_in.astype(BF16), att_w_out=att_w_out.astype(BF16),
        rwkv_mu_prev=rwkv_mu_prev, rwkv_mu_next=rwkv_mu_next, rwkv_w_in=rwkv_w_in.astype(BF16),
        rwkv_w0=rwkv_w0, rwkv_w1=rwkv_w1, rwkv_w2=rwkv_w2,
        rwkv_a0=rwkv_a0, rwkv_a1=rwkv_a1, rwkv_a2=rwkv_a2,
        rwkv_v0=rwkv_v0, rwkv_v1=rwkv_v1, rwkv_v2=rwkv_v2,
        rwkv_k_k=rwkv_k_k, rwkv_k_a=rwkv_k_a, rwkv_r_k=rwkv_r_k,
        rwkv_gn_g=rwkv_gn_g, rwkv_gn_b=rwkv_gn_b, rwkv_w_out=rwkv_w_out.astype(BF16),
    )
    rope = _rope_tables(max(x_prompt.shape[1], x_sample.shape[1]))
    return (_trunk(x_prompt, p, rope), _trunk(x_sample, p, rope))
```

```python
import functools
import math

import jax
import jax.numpy as jnp
from jax import lax
from jax.experimental import pallas as pl
from jax.experimental.pallas import tpu as pltpu

F32 = jnp.float32
BF16 = jnp.bfloat16

D_MODEL = 2048
LANES = 128
ATT_HEAD_DIM = 128
ATT_GROUPS = ((128, 1), (512, 4), (2048, 16))
ATT_HALF = 64
ATT_QKV_COLS = 3 * D_MODEL
ROPE_THETA = 10000.0
RWKV_HEAD_DIM = 64
LORA_PAD = 128
RMS_EPS = 1e-6
GN_EPS = 64e-5
NEG_INF = -1e30
CHUNK = 64
VMEM_LIMIT_BYTES = 56 * 1024 * 1024

NT_DIMS = (((1,), (1,)), ((), ()))
TN_DIMS = (((0,), (0,)), ((), ()))


def _cparams(*sem):
    return pltpu.CompilerParams(dimension_semantics=sem, vmem_limit_bytes=VMEM_LIMIT_BYTES)


def _rms_scale(x):
    return lax.rsqrt(jnp.mean(x * x, axis=-1, keepdims=True) + RMS_EPS)


def _sigmoid(x):
    return 1.0 / (1.0 + jnp.exp(-x))


def _split_dot(lhs_bf, x):
    hi = x.astype(BF16)
    lo = (x - hi.astype(F32)).astype(BF16)
    return (jnp.dot(lhs_bf, hi, preferred_element_type=F32)
            + jnp.dot(lhs_bf, lo, preferred_element_type=F32))


def _group_sum(x, gmat):
    hi = x.astype(BF16)
    lo = (x - hi.astype(F32)).astype(BF16)
    return (jnp.dot(hi, gmat, preferred_element_type=F32)
            + jnp.dot(lo, gmat, preferred_element_type=F32))


def _group_matrix(n):
    r = lax.broadcasted_iota(jnp.int32, (n, n), 0) // RWKV_HEAD_DIM
    c = lax.broadcasted_iota(jnp.int32, (n, n), 1) // RWKV_HEAD_DIM
    return jnp.where(r == c, 1.0, 0.0).astype(BF16)


def _rope_table_kernel(invf_ref, cos_ref, sin_ref):
    rows = cos_ref.shape[0]
    base = pl.program_id(0) * rows
    pos = (base + lax.broadcasted_iota(jnp.int32, (rows, LANES), 0)).astype(F32)
    ang = pos * invf_ref[...]
    lane = lax.broadcasted_iota(jnp.int32, (rows, LANES), 1)
    s = jnp.sin(ang)
    cos_ref[...] = jnp.cos(ang)
    sin_ref[...] = jnp.where(lane < ATT_HEAD_DIM // 2, -s, s)


def _rope_tables(seq):
    half = ATT_HEAD_DIM // 2
    inv_freq = 1.0 / (ROPE_THETA ** (jnp.arange(half, dtype=F32) * 2.0 / ATT_HEAD_DIM))
    invf = jnp.concatenate([inv_freq, inv_freq])[None, :]
    rows = 512
    return pl.pallas_call(
        _rope_table_kernel,
        grid=(seq // rows,),
        in_specs=[pl.BlockSpec((1, LANES), lambda i: (0, 0))],
        out_specs=[pl.BlockSpec((rows, LANES), lambda i: (i, 0))] * 2,
        out_shape=[jax.ShapeDtypeStruct((seq, LANES), F32)] * 2,
        compiler_params=_cparams("arbitrary"),
        name="rope_table",
    )(invf)


def _att_in_kernel(x_ref, g_ref, w_ref, cos_ref, sin_ref, o_ref, h_ref, acc_ref, *, tm, tn, dil, scale):
    j = pl.program_id(1)

    @pl.when(j == 0)
    def _():
        x = x_ref[...]
        h_ref[...] = (x * _rms_scale(x) * g_ref[...]).astype(BF16)

    acc = jnp.dot(h_ref[...], w_ref[...], preferred_element_type=F32)
    part = j // (D_MODEL // tn)

    @pl.when(part < 2)
    def _():
        cos = cos_ref[...]
        sin = sin_ref[...]
        sc = jnp.where(part == 0, scale, 1.0).astype(F32)
        for hh in range(tn // LANES):
            t = acc[:, hh * LANES:(hh + 1) * LANES]
            rt = pltpu.roll(t, ATT_HEAD_DIM // 2, axis=1)
            acc_ref[hh] = (t * cos + rt * sin) * sc

    @pl.when(part == 2)
    def _():
        for hh in range(tn // LANES):
            acc_ref[hh] = acc[:, hh * LANES:(hh + 1) * LANES]

    for hh in range(tn // LANES):
        sl = slice(hh * LANES, (hh + 1) * LANES)
        for r in range(dil):
            o_ref[r, :, sl] = acc_ref[hh, pl.ds(r, tm // dil, stride=dil), :].astype(BF16)


def _att_in(x2, g, w_bf, cos_t, sin_t, seq, gidx, dil):
    rows = x2.shape[0]
    tm, tn = 512, 1024
    tiles_per_seq = seq // tm
    col0 = gidx * (ATT_QKV_COLS // tn)
    kern = functools.partial(_att_in_kernel, tm=tm, tn=tn, dil=dil, scale=ATT_HEAD_DIM ** -0.5)
    return pl.pallas_call(
        kern,
        grid=(rows // tm, ATT_QKV_COLS // tn),
        in_specs=[
            pl.BlockSpec((tm, D_MODEL), lambda i, j: (i, 0)),
            pl.BlockSpec((1, D_MODEL), lambda i, j: (0, 0)),
            pl.BlockSpec((D_MODEL, tn), lambda i, j: (0, col0 + j)),
            pl.BlockSpec((tm, LANES), lambda i, j: (i % tiles_per_seq, 0)),
            pl.BlockSpec((tm, LANES), lambda i, j: (i % tiles_per_seq, 0)),
        ],
        out_specs=pl.BlockSpec((dil, tm // dil, tn), lambda i, j: (0, i, j)),
        out_shape=jax.ShapeDtypeStruct((dil, rows // dil, ATT_QKV_COLS), BF16),
        scratch_shapes=[pltpu.VMEM((tm, D_MODEL), BF16), pltpu.VMEM((tn // LANES, tm, LANES), F32)],
        compiler_params=_cparams("parallel", "arbitrary"),
        name=f"att_in_d{dil}",
    )(x2, g, w_bf, cos_t, sin_t)


def _gate_in_kernel(x_ref, g_ref, w_ref, o_ref):
    x = x_ref[...]
    h = (x * _rms_scale(x) * g_ref[...]).astype(BF16)
    o_ref[...] = jnp.dot(h, w_ref[...], preferred_element_type=F32).astype(BF16)


def _gate_in(x2, g, w_bf):
    rows = x2.shape[0]
    tm = 512
    col0 = 3 * ATT_QKV_COLS // D_MODEL
    return pl.pallas_call(
        _gate_in_kernel,
        grid=(rows // tm,),
        in_specs=[
            pl.BlockSpec((tm, D_MODEL), lambda i: (i, 0)),
            pl.BlockSpec((1, D_MODEL), lambda i: (0, 0)),
            pl.BlockSpec((D_MODEL, D_MODEL), lambda i: (0, col0)),
        ],
        out_specs=pl.BlockSpec((tm, D_MODEL), lambda i: (i, 0)),
        out_shape=jax.ShapeDtypeStruct((rows, D_MODEL), BF16),
        compiler_params=_cparams("parallel"),
        name="att_gate_in",
    )(x2, g, w_bf)


def _attn_kernel(q_ref, kp_ref, kc_ref, kn_ref, vp_ref, vc_ref, vn_ref, o_ref, lse_ref, *, bq, sub_len, nh):
    i = pl.program_id(3)
    nk = bq + 2 * ATT_HALF
    ii = lax.broadcasted_iota(jnp.int32, (bq, nk), 0)
    jj = lax.broadcasted_iota(jnp.int32, (bq, nk), 1)
    rel = jj - ii
    kpos = i * bq - ATT_HALF + jj
    valid = (rel >= 0) & (rel <= 2 * ATT_HALF) & (kpos >= 0) & (kpos < sub_len)
    for hh in range(nh):
        sl = slice(hh * LANES, (hh + 1) * LANES)
        q = q_ref[:, sl]
        kcat = jnp.concatenate([kp_ref[:, sl], kc_ref[:, sl], kn_ref[:, sl]], axis=0)
        vcat = jnp.concatenate([vp_ref[:, sl], vc_ref[:, sl], vn_ref[:, sl]], axis=0)
        s = lax.dot_general(q, kcat, NT_DIMS, preferred_element_type=F32)
        s = jnp.where(valid, s, NEG_INF)
        m = jnp.max(s, axis=-1, keepdims=True)
        p = jnp.exp(s - m)
        l = jnp.sum(p, axis=-1, keepdims=True)
        o = jnp.dot(p.astype(BF16), vcat, preferred_element_type=F32)
        o_ref[:, sl] = (o / l).astype(BF16)
        lse_ref[:, sl] = jnp.broadcast_to(m + jnp.log(l), (bq, LANES))


def _attn_group(qkv, dil, batch, seq):
    sub_len = seq // dil
    bq = min(256, sub_len)
    hw = 1024
    nh = hw // LANES
    hblocks = D_MODEL // hw
    nqb = sub_len // bq
    halo_per_q = bq // ATT_HALF
    n_halo = sub_len // ATT_HALF

    def cur(part):
        return pl.BlockSpec((None, bq, hw), lambda b, r, h, i: (r, b * nqb + i, part * hblocks + h))

    def prev(part):
        return pl.BlockSpec((None, ATT_HALF, hw),
                            lambda b, r, h, i: (r, b * n_halo + jnp.maximum(i * halo_per_q - 1, 0),
                                                part * hblocks + h))

    def nxt(part):
        return pl.BlockSpec((None, ATT_HALF, hw),
                            lambda b, r, h, i: (r, b * n_halo + jnp.minimum((i + 1) * halo_per_q, n_halo - 1),
                                                part * hblocks + h))

    out_spec = pl.BlockSpec((None, bq, hw), lambda b, r, h, i: (r, b * nqb + i, h))
    rows = batch * sub_len
    kern = functools.partial(_attn_kernel, bq=bq, sub_len=sub_len, nh=nh)
    return pl.pallas_call(
        kern,
        grid=(batch, dil, hblocks, nqb),
        in_specs=[cur(0), prev(1), cur(1), nxt(1), prev(2), cur(2), nxt(2)],
        out_specs=[out_spec, out_spec],
        out_shape=[jax.ShapeDtypeStruct((dil, rows, D_MODEL), BF16),
                   jax.ShapeDtypeStruct((dil, rows, D_MODEL), F32)],
        compiler_params=_cparams("parallel", "parallel", "parallel", "arbitrary"),
        name=f"attn_d{dil}",
    )(qkv, qkv, qkv, qkv, qkv, qkv, qkv)


def _proj_norm_residual(y_bf, w_ref, g_ref, x_ref, o_ref):
    out = jnp.dot(y_bf, w_ref[...], preferred_element_type=F32)
    o_ref[...] = x_ref[...] + out * _rms_scale(out) * g_ref[...]


def _att_out_kernel(o0_ref, o1_ref, o2_ref, l0_ref, l1_ref, l2_ref, z_ref, x_ref, w_ref, g_ref, o_ref,
                    so1, so2, sl1, sl2, y_ref, *, tm, dils):
    for cb in range(D_MODEL // LANES):
        sl = slice(cb * LANES, (cb + 1) * LANES)
        for src, dst, d in ((o1_ref, so1, dils[1]), (o2_ref, so2, dils[2]), (l1_ref, sl1, dils[1]),
                            (l2_ref, sl2, dils[2])):
            for r in range(d):
                dst[cb, pl.ds(r, tm // d, stride=d), :] = src[r, :, sl].astype(F32)
        l0, l1, l2 = l0_ref[0, :, sl], sl1[cb], sl2[cb]
        m = jnp.maximum(jnp.maximum(l0, l1), l2)
        e0, e1, e2 = jnp.exp(l0 - m), jnp.exp(l1 - m), jnp.exp(l2 - m)
        o = (e0 * o0_ref[0, :, sl].astype(F32) + e1 * so1[cb] + e2 * so2[cb]) / (e0 + e1 + e2)
        z = z_ref[:, sl].astype(F32)
        y_ref[:, sl] = (o * (z * _sigmoid(z))).astype(BF16)
    _proj_norm_residual(y_ref[...], w_ref, g_ref, x_ref, o_ref)


def _att_out(outs, lses, z, x2, w_bf, g):
    rows = x2.shape[0]
    tm = 256
    dils = tuple(d for _, d in ATT_GROUPS)
    row_spec = pl.BlockSpec((tm, D_MODEL), lambda i: (i, 0))

    def res_spec(d):
        return pl.BlockSpec((d, tm // d, D_MODEL), lambda i: (0, i, 0))

    kern = functools.partial(_att_out_kernel, tm=tm, dils=dils)
    return pl.pallas_call(
        kern,
        grid=(rows // tm,),
        in_specs=[res_spec(d) for d in dils] * 2 + [
            row_spec, row_spec,
            pl.BlockSpec((D_MODEL, D_MODEL), lambda i: (0, 0)),
            pl.BlockSpec((1, D_MODEL), lambda i: (0, 0)),
        ],
        out_specs=row_spec,
        out_shape=jax.ShapeDtypeStruct((rows, D_MODEL), F32),
        scratch_shapes=[pltpu.VMEM((D_MODEL // LANES, tm, LANES), F32)] * 4 + [pltpu.VMEM((tm, D_MODEL), BF16)],
        compiler_params=_cparams("parallel"),
        name="att_out",
    )(*outs, *lses, z, x2, w_bf, g)


def _chunk_tri(n, reverse):
    t = lax.broadcasted_iota(jnp.int32, (n, n), 0)
    s = lax.broadcasted_iota(jnp.int32, (n, n), 1)
    same = (t // CHUNK) == (s // CHUNK)
    order = (s >= t) if reverse else (s <= t)
    return jnp.where(same & order, 1.0, 0.0).astype(BF16)


def _rwkv_in_kernel(*refs, tm, tn, tiles_per_seq, has_vres):
    (x_ref, xp_ref, xn_ref, g_ref, mup_ref, mun_ref,
     wr_ref, wk_ref, wv_ref, wz_ref,
     w1_ref, w2_ref, w0_ref, a1_ref, a2_ref, a0_ref,
     kk_ref, ka_ref, rk_ref) = refs[:19]
    pos = 19
    if has_vres:
        v1_ref, v2_ref, v0_ref, vf_ref = refs[pos:pos + 4]
        pos += 4
    (kkq_out, rq_out, ad_out, kdk_out, kend_out, aend_out, est_out, v_out, bonus_out, sz_out) = refs[pos:pos + 10]
    pos += 10
    xs_ref, hw_ref, ha_ref = refs[pos:pos + 3]
    hv_ref = refs[pos + 3] if has_vres else None

    i = pl.program_id(0)
    j = pl.program_id(1)
    mix_slot = {0: 0, 2: 1, 3: 2, 5: 3}

    @pl.when(j == 0)
    def _():
        t_in_seq = i % tiles_per_seq
        keep_prev = jnp.where(t_in_seq == 0, 0.0, 1.0).astype(F32)
        keep_next = jnp.where(t_in_seq == tiles_per_seq - 1, 0.0, 1.0).astype(F32)
        sx = _rms_scale(x_ref[...])
        xp = xp_ref[7:8, :]
        xn = xn_ref[0:1, :]
        sp = _rms_scale(xp) * keep_prev
        sn = _rms_scale(xn) * keep_next
        cw = 512
        row = lax.broadcasted_iota(jnp.int32, (tm, cw), 0)
        hw_acc = [jnp.zeros((tm, LORA_PAD), F32) for _ in range(2)]
        ha_acc = [jnp.zeros((tm, LORA_PAD), F32) for _ in range(2)]
        hv_acc = jnp.zeros((tm, LORA_PAD), F32)
        for cb in range(D_MODEL // cw):
            sl = slice(cb * cw, (cb + 1) * cw)
            g = g_ref[:, sl]
            h = x_ref[:, sl] * sx * g
            hp_row = xp[:, sl] * sp * g
            hn_row = xn[:, sl] * sn * g
            h_prev = jnp.where(row == 0, hp_row, pltpu.roll(h, 1, axis=0))
            h_next = jnp.where(row == tm - 1, hn_row, pltpu.roll(h, tm - 1, axis=0))
            dp = h_prev - h
            dn = h_next - h
            mixes = {}
            for t in range(6):
                mixes[t] = (h + dp * mup_ref[t:t + 1, sl] + dn * mun_ref[t:t + 1, sl]).astype(BF16)
                if t in mix_slot:
                    xs_ref[mix_slot[t], :, sl] = mixes[t]
            for c in range(2):
                hw_acc[c] = hw_acc[c] + jnp.dot(mixes[1], w1_ref[c, sl, :], preferred_element_type=F32)
                ha_acc[c] = ha_acc[c] + jnp.dot(mixes[4], a1_ref[c, sl, :], preferred_element_type=F32)
            if has_vres:
                hv_acc = hv_acc + jnp.dot(mixes[3], v1_ref[sl, :], preferred_element_type=F32)
        for c in range(2):
            hw_ref[c] = jnp.tanh(hw_acc[c]).astype(BF16)
            ha_ref[c] = ha_acc[c].astype(BF16)
        if has_vres:
            hv_ref[...] = hv_acc.astype(BF16)

    r = jnp.dot(xs_ref[0], wr_ref[...], preferred_element_type=F32)
    k = jnp.dot(xs_ref[1], wk_ref[...], preferred_element_type=F32)
    v = jnp.dot(xs_ref[2], wv_ref[...], preferred_element_type=F32)
    z = jnp.dot(xs_ref[3], wz_ref[...], preferred_element_type=F32)
    if has_vres:
        gate = _sigmoid(v0_ref[...] + jnp.dot(hv_ref[...], v2_ref[...], preferred_element_type=F32))
        v = v + (vf_ref[...] - v) * gate
    gmat = _group_matrix(tn)
    kk = k * kk_ref[...]
    nrm = jnp.sqrt(_group_sum(kk * kk, gmat))
    kk = kk / jnp.maximum(nrm, 1e-12)
    k_a = ka_ref[...]
    rk_acc = jnp.zeros((tm, tn), F32)
    nchunk = tm // CHUNK
    half = 256
    for c in range(2):
        wl = w0_ref[c:c + 1, :] + jnp.dot(hw_ref[c], w2_ref[c], preferred_element_type=F32)
        lw = -math.exp(-0.5) * _sigmoid(wl)
        a = _sigmoid(a0_ref[c:c + 1, :] + jnp.dot(ha_ref[c], a2_ref[c], preferred_element_type=F32))
        kd = k * (1.0 + (a - 1.0) * k_a)
        rk_acc = rk_acc + r * kd * rk_ref[c:c + 1, :]
        tri = _chunk_tri(half, reverse=(c == 1))
        g = jnp.concatenate([_split_dot(tri, lw[hh * half:(hh + 1) * half]) for hh in range(tm // half)], axis=0)
        g3 = g.reshape(nchunk, CHUNK, tn)
        last = 0 if c == 1 else CHUNK - 1
        g_tot = g3[:, last:last + 1, :]
        e_tot = jnp.exp(g_tot)
        est_out[c] = e_tot.reshape(nchunk, tn)
        e_q = jnp.exp(g)
        e_qp = jnp.exp(g - lw)
        e_k = jnp.exp(-g)
        e_end = (e_k.reshape(nchunk, CHUNK, tn) * e_tot).reshape(tm, tn)
        kka = kk * a
        kkq_out[c] = (kk * e_qp).astype(BF16)
        rq_out[c] = (r * e_q).astype(BF16)
        ad_out[c] = (kka * e_k).astype(BF16)
        kdk_out[c] = (kd * e_k).astype(BF16)
        kend_out[c] = (kd * e_end).astype(BF16)
        aend_out[c] = (kka * e_end).astype(BF16)
    v_out[...] = v
    bonus_out[...] = _group_sum(rk_acc, gmat) * v
    sz_out[...] = (z * _sigmoid(z)).astype(BF16)


def _pad_lora(w1, w2):
    rank = w1.shape[-1]
    pad1 = [(0, 0)] * (w1.ndim - 1) + [(0, LORA_PAD - rank)]
    pad2 = [(0, 0)] * (w2.ndim - 2) + [(0, LORA_PAD - rank), (0, 0)]
    return jnp.pad(w1, pad1).astype(BF16), jnp.pad(w2, pad2).astype(BF16)


def _rwkv_in(x2, seq, g, mu_prev, mu_next, w_in_bf, w0, w1, w2, a0, a1, a2, k_k, k_a, r_k, vres, v_first):
    rows = x2.shape[0]
    tm, tn = 512, 256
    tiles_per_seq = seq // tm
    ncol = D_MODEL // tn
    has_vres = vres is not None
    w1p, w2p = _pad_lora(w1, w2)
    a1p, a2p = _pad_lora(a1, a2)
    sub = tm // 8
    nsub = rows // 8

    def const2(shape):
        return pl.BlockSpec(shape, lambda i, j: (0, 0))

    def col2(nrow):
        return pl.BlockSpec((nrow, tn), lambda i, j: (0, j))

    def wcol(gi):
        return pl.BlockSpec((D_MODEL, tn), lambda i, j: (0, gi * ncol + j))

    in_specs = [
        pl.BlockSpec((tm, D_MODEL), lambda i, j: (i, 0)),
        pl.BlockSpec((8, D_MODEL), lambda i, j: (jnp.maximum(i * sub - 1, 0), 0)),
        pl.BlockSpec((8, D_MODEL), lambda i, j: (jnp.minimum((i + 1) * sub, nsub - 1), 0)),
        const2((1, D_MODEL)), const2((6, D_MODEL)), const2((6, D_MODEL)),
        wcol(0), wcol(1), wcol(2), wcol(3),
        pl.BlockSpec((2, D_MODEL, LORA_PAD), lambda i, j: (0, 0, 0)),
        pl.BlockSpec((2, LORA_PAD, tn), lambda i, j: (0, 0, j)),
        col2(2),
        pl.BlockSpec((2, D_MODEL, LORA_PAD), lambda i, j: (0, 0, 0)),
        pl.BlockSpec((2, LORA_PAD, tn), lambda i, j: (0, 0, j)),
        col2(2),
        col2(1), col2(1), col2(2),
    ]
    args = [x2, x2, x2, g, mu_prev, mu_next, w_in_bf, w_in_bf, w_in_bf, w_in_bf,
            w1p, w2p, w0, a1p, a2p, a0, k_k, k_a, r_k]
    if has_vres:
        v0, v1, v2 = vres
        v1p, v2p = _pad_lora(v1, v2)
        in_specs += [const2((D_MODEL, LORA_PAD)), col2(LORA_PAD), col2(1),
                     pl.BlockSpec((tm, tn), lambda i, j: (i, j))]
        args += [v1p, v2p, v0, v_first]

    tile = pl.BlockSpec((tm, tn), lambda i, j: (i, j))
    tile2 = pl.BlockSpec((2, tm, tn), lambda i, j: (0, i, j))
    est_spec = pl.BlockSpec((2, tm // CHUNK, tn), lambda i, j: (0, i, j))
    sds_bf2 = jax.ShapeDtypeStruct((2, rows, D_MODEL), BF16)
    sds = jax.ShapeDtypeStruct((rows, D_MODEL), F32)
    scratch = [pltpu.VMEM((4, tm, D_MODEL), BF16), pltpu.VMEM((2, tm, LORA_PAD), BF16),
               pltpu.VMEM((2, tm, LORA_PAD), BF16)]
    if has_vres:
        scratch.append(pltpu.VMEM((tm, LORA_PAD), BF16))
    kern = functools.partial(_rwkv_in_kernel, tm=tm, tn=tn, tiles_per_seq=tiles_per_seq, has_vres=has_vres)
    return pl.pallas_call(
        kern,
        grid=(rows // tm, ncol),
        in_specs=in_specs,
        out_specs=[tile2] * 6 + [est_spec, tile, tile, tile],
        out_shape=[sds_bf2] * 6 + [jax.ShapeDtypeStruct((2, rows // CHUNK, D_MODEL), F32), sds, sds,
                                   jax.ShapeDtypeStruct((rows, D_MODEL), BF16)],
        scratch_shapes=scratch,
        compiler_params=_cparams("parallel", "arbitrary"),
        name="rwkv_in_vres" if has_vres else "rwkv_in",
    )(*args)


def _rwkv_scan_kernel(kkq_ref, rq_ref, ad_ref, kdk_ref, kend_ref, aend_ref, v_ref, est_ref, y_ref, state_ref,
                      *, tb, npair):
    rev = pl.program_id(1) == 1
    step = pl.program_id(3)
    nchunk = tb // CHUNK
    pairs = range(npair)

    @pl.when(step == 0)
    def _():
        state_ref[...] = jnp.zeros_like(state_ref)

    t = lax.broadcasted_iota(jnp.int32, (CHUNK, LANES), 0)
    lane = lax.broadcasted_iota(jnp.int32, (CHUNK, LANES), 1)
    s = lane & (CHUNK - 1)
    d = jnp.where(rev, s - t, t - s)
    strict = d > 0
    incl = d >= 0
    eye = jnp.where(d == 0, 1.0, 0.0).astype(F32)
    first_half = lane < CHUNK
    half_a = jnp.where(first_half, 1.0, 0.0).astype(BF16)
    half_b = jnp.where(first_half, 0.0, 1.0).astype(BF16)
    rr = lax.broadcasted_iota(jnp.int32, (LANES, LANES), 0)
    cc = lax.broadcasted_iota(jnp.int32, (LANES, LANES), 1)
    bd_mask = (rr < CHUNK) == (cc < CHUNK)
    est_rows = lax.broadcasted_iota(jnp.int32, (nchunk, npair * LANES), 0)

    def bd16(y):
        return jnp.concatenate([y * half_a, y * half_b], axis=0)

    def bd32(y):
        return jnp.concatenate([jnp.where(first_half, y, 0.0), jnp.where(first_half, 0.0, y)],
                               axis=0).astype(BF16)

    def mm(a, b):
        return jnp.dot(a, b, preferred_element_type=F32)

    def body(ci, carry):
        c = jnp.where(rev, nchunk - 1 - ci, ci)
        rows = pl.ds(pl.multiple_of(c * CHUNK, CHUNK), CHUNK)
        lanes = [slice(p * LANES, (p + 1) * LANES) for p in pairs]
        kkq = [kkq_ref[rows, lanes[p]] for p in pairs]
        rq = [rq_ref[rows, lanes[p]] for p in pairs]
        v = [v_ref[rows, lanes[p]] for p in pairs]
        gram = [lax.dot_general(
            jnp.concatenate([kkq[p], rq[p]], axis=0),
            jnp.concatenate([bd16(ad_ref[rows, lanes[p]]), bd16(kdk_ref[rows, lanes[p]])], axis=0),
            NT_DIMS, preferred_element_type=F32) for p in pairs]
        a_mat = [jnp.where(strict, gram[p][0:CHUNK, 0:LANES], 0.0) for p in pairs]
        ak = [jnp.where(strict, gram[p][0:CHUNK, LANES:], 0.0) for p in pairs]
        bra = [jnp.where(incl, gram[p][CHUNK:, 0:LANES], 0.0).astype(BF16) for p in pairs]
        brk = [jnp.where(incl, gram[p][CHUNK:, LANES:], 0.0) for p in pairs]
        early = [mm(jnp.concatenate([ak[p], brk[p]], axis=0).astype(BF16), bd32(v[p])) for p in pairs]
        vk = [lax.dot_general(v[p].astype(BF16), kend_ref[rows, lanes[p]], TN_DIMS,
                              preferred_element_type=F32) for p in pairs]
        a_pow = [mm(a_mat[p].astype(BF16), bd32(a_mat[p])) for p in pairs]
        inv = [eye - a_mat[p] for p in pairs]
        for _ in range(4):
            st = [mm(jnp.concatenate([inv[p], a_pow[p]], axis=0).astype(BF16), bd32(a_pow[p])) for p in pairs]
            inv = [inv[p] + st[p][0:CHUNK] for p in pairs]
            a_pow = [st[p][CHUNK:] for p in pairs]
        inv = [inv[p] + mm(inv[p].astype(BF16), bd32(a_pow[p])) for p in pairs]
        wu = [mm(inv[p].astype(BF16), jnp.concatenate([bd16(kkq[p]), bd32(early[p][0:CHUNK])], axis=1))
              for p in pairs]
        wu16 = [wu[p].astype(BF16) for p in pairs]
        ry = [mm(bra[p], jnp.concatenate([bd32(wu[p][:, 0:LANES]), bd32(wu[p][:, LANES:])], axis=1))
              for p in pairs]
        tn = [lax.dot_general(wu16[p], aend_ref[rows, lanes[p]], TN_DIMS, preferred_element_type=F32)
              for p in pairs]
        rqp = [(rq[p].astype(F32) - ry[p][:, 0:LANES]).astype(BF16) for p in pairs]
        yv = [early[p][CHUNK:] - ry[p][:, LANES:] for p in pairs]
        xmat = [jnp.where(bd_mask, tn[p][0:LANES], 0.0).astype(BF16) for p in pairs]
        nmat = [jnp.where(bd_mask, vk[p] - tn[p][LANES:], 0.0) for p in pairs]
        s0 = [state_ref[p] for p in pairs]
        s16 = [s0[p].astype(BF16) for p in pairs]
        ys = [yv[p] + lax.dot_general(rqp[p], s16[p], NT_DIMS, preferred_element_type=F32) for p in pairs]
        sx = [mm(s16[p], xmat[p]) for p in pairs]
        est_row = jnp.sum(jnp.where(est_rows == c, est_ref[...], 0.0), axis=0, keepdims=True)
        for p in pairs:
            y_ref[rows, lanes[p]] = ys[p]
            state_ref[p] = s0[p] * est_row[:, lanes[p]] - sx[p] + nmat[p]
        return carry

    lax.fori_loop(0, nchunk, body, 0)


def _rwkv_scan(kkq, rq, ad, kdk, kend, aend, est, v, batch, seq):
    tb, lw_lanes = 512, 1024
    nt = seq // tb
    npair = lw_lanes // LANES

    def tmap(i, dr):
        return i + dr * (nt - 1 - 2 * i)

    v_spec = pl.BlockSpec((tb, lw_lanes), lambda b, dr, h, i: (b * nt + tmap(i, dr), h))
    d_spec = pl.BlockSpec((None, tb, lw_lanes), lambda b, dr, h, i: (dr, b * nt + tmap(i, dr), h))
    e_spec = pl.BlockSpec((None, tb // CHUNK, lw_lanes), lambda b, dr, h, i: (dr, b * nt + tmap(i, dr), h))
    kern = functools.partial(_rwkv_scan_kernel, tb=tb, npair=npair)
    return pl.pallas_call(
        kern,
        grid=(batch, 2, D_MODEL // lw_lanes, nt),
        in_specs=[d_spec] * 6 + [v_spec, e_spec],
        out_specs=d_spec,
        out_shape=jax.ShapeDtypeStruct((2, batch * seq, D_MODEL), F32),
        scratch_shapes=[pltpu.VMEM((npair, LANES, LANES), F32)],
        compiler_params=_cparams("parallel", "parallel", "parallel", "arbitrary"),
        name="rwkv_scan",
    )(kkq, rq, ad, kdk, kend, aend, v, est)


def _rwkv_out_kernel(y_ref, bonus_ref, sz_ref, gg_ref, gb_ref, x_ref, w_ref, g_ref, o_ref, yb_ref):
    gmat = _group_matrix(LANES)
    inv_n = 1.0 / RWKV_HEAD_DIM
    for cb in range(D_MODEL // LANES):
        sl = slice(cb * LANES, (cb + 1) * LANES)
        y = y_ref[0, :, sl] + y_ref[1, :, sl]
        mu = _group_sum(y, gmat) * inv_n
        yc = y - mu
        var = _group_sum(yc * yc, gmat) * inv_n
        yn = yc * lax.rsqrt(var + GN_EPS) * gg_ref[:, sl] + gb_ref[:, sl]
        yb_ref[:, sl] = ((yn + bonus_ref[:, sl]) * sz_ref[:, sl].astype(F32)).astype(BF16)
    _proj_norm_residual(yb_ref[...], w_ref, g_ref, x_ref, o_ref)


def _rwkv_out(y2, bonus, sz, gn_g, gn_b, x2, w_bf, g):
    rows = x2.shape[0]
    tm = 256
    row_spec = pl.BlockSpec((tm, D_MODEL), lambda i: (i, 0))
    vec = pl.BlockSpec((1, D_MODEL), lambda i: (0, 0))
    return pl.pallas_call(
        _rwkv_out_kernel,
        grid=(rows // tm,),
        in_specs=[pl.BlockSpec((2, tm, D_MODEL), lambda i: (0, i, 0)), row_spec, row_spec, vec, vec, row_spec,
                  pl.BlockSpec((D_MODEL, D_MODEL), lambda i: (0, 0)), vec],
        out_specs=row_spec,
        out_shape=jax.ShapeDtypeStruct((rows, D_MODEL), F32),
        scratch_shapes=[pltpu.VMEM((tm, D_MODEL), BF16)],
        compiler_params=_cparams("parallel"),
        name="rwkv_out",
    )(y2, bonus, sz, gn_g, gn_b, x2, w_bf, g)


def _trunk(x, p, rope):
    batch, seq, _ = x.shape
    x2 = x.reshape(batch * seq, D_MODEL)
    cos_t, sin_t = rope
    v_first = None
    depth = p["norm_pre"].shape[0]
    for layer in range(depth):
        j = layer // 2
        g_pre = p["norm_pre"][layer][None, :]
        g_post = p["norm_post"][layer][None, :]
        if layer % 2 == 0:
            w_in = p["att_w_in"][j]
            outs, lses = [], []
            for gidx, (_, dil) in enumerate(ATT_GROUPS):
                qkv = _att_in(x2, g_pre, w_in, cos_t, sin_t, seq, gidx, dil)
                o, lse = _attn_group(qkv, dil, batch, seq)
                outs.append(o)
                lses.append(lse)
            z = _gate_in(x2, g_pre, w_in)
            x2 = _att_out(outs, lses, z, x2, p["att_w_out"][j], g_post)
        else:
            vres = None if j == 0 else (p["rwkv_v0"][j - 1][None, :], p["rwkv_v1"][j - 1], p["rwkv_v2"][j - 1])
            kkq, rq, ad, kdk, kend, aend, est, v, bonus, sz = _rwkv_in(
                x2, seq, g_pre, p["rwkv_mu_prev"][j], p["rwkv_mu_next"][j], p["rwkv_w_in"][j],
                p["rwkv_w0"][j], p["rwkv_w1"][j], p["rwkv_w2"][j],
                p["rwkv_a0"][j], p["rwkv_a1"][j], p["rwkv_a2"][j],
                p["rwkv_k_k"][j][None, :], p["rwkv_k_a"][j][None, :], p["rwkv_r_k"][j].reshape(2, D_MODEL),
                vres, v_first)
            if j == 0:
                v_first = v
            y2 = _rwkv_scan(kkq, rq, ad, kdk, kend, aend, est, v, batch, seq)
            x2 = _rwkv_out(y2, bonus, sz, p["rwkv_gn_g"][j][None, :], p["rwkv_gn_b"][j][None, :], x2,
                           p["rwkv_w_out"][j], g_post)
    return x2.reshape(batch, seq, D_MODEL)


def kernel(x_prompt, x_sample, norm_pre, norm_post, att_w_in, att_w_out, rwkv_mu_prev, rwkv_mu_next, rwkv_w_in, rwkv_w0, rwkv_w1, rwkv_w2, rwkv_a0, rwkv_a1, rwkv_a2, rwkv_v0, rwkv_v1, rwkv_v2, rwkv_k_k, rwkv_k_a, rwkv_r_k, rwkv_gn_g, rwkv_gn_b, rwkv_w_out):
    p = dict(
        norm_pre=norm_pre, norm_post=norm_post,
        att_w_in=att_w_in.astype(BF16), att_w_out=att_w_out.astype(BF16),
        rwkv_mu_prev=rwkv_mu_prev, rwkv_mu_next=rwkv_mu_next, rwkv_w_in=rwkv_w_in.astype(BF16),
        rwkv_w0=rwkv_w0, rwkv_w1=rwkv_w1, rwkv_w2=rwkv_w2,
        rwkv_a0=rwkv_a0, rwkv_a1=rwkv_a1, rwkv_a2=rwkv_a2,
        rwkv_v0=rwkv_v0, rwkv_v1=rwkv_v1, rwkv_v2=rwkv_v2,
        rwkv_k_k=rwkv_k_k, rwkv_k_a=rwkv_k_a, rwkv_r_k=rwkv_r_k,
        rwkv_gn_g=rwkv_gn_g, rwkv_gn_b=rwkv_gn_b, rwkv_w_out=rwkv_w_out.astype(BF16),
    )
    rope = _rope_tables(max(x_prompt.shape[1], x_sample.shape[1]))
    return (_trunk(x_prompt, p, rope), _trunk(x_sample, p, rope))
```

```python
import functools
import math

import jax
import jax.numpy as jnp
from jax import lax
from jax.experimental import pallas as pl
from jax.experimental.pallas import tpu as pltpu

F32 = jnp.float32
BF16 = jnp.bfloat16

D_MODEL = 2048
LANES = 128
MXU_COLS = 256
ATT_HEAD_DIM = 128
ATT_GROUPS = ((128, 1), (512, 4), (2048, 16))
ATT_HALF = 64
ROPE_THETA = 10000.0
RWKV_HEAD_DIM = 64
LORA_PAD = 128
RMS_EPS = 1e-6
GN_EPS = 64e-5
NEG_INF = -1e30
CHUNK = 64
CHUNKS_PER_TRIP = 2
VMEM_LIMIT_BYTES = 56 * 1024 * 1024

NT_DIMS = (((1,), (1,)), ((), ()))
TN_DIMS = (((0,), (0,)), ((), ()))


def _cparams(*sem):
    return pltpu.CompilerParams(dimension_semantics=sem, vmem_limit_bytes=VMEM_LIMIT_BYTES)


def _rms_scale(x):
    return lax.rsqrt(jnp.mean(x * x, axis=-1, keepdims=True) + RMS_EPS)


def _sigmoid(x):
    return 0.5 * jnp.tanh(0.5 * x) + 0.5


def _split_dot(lhs_bf, x):
    hi = x.astype(BF16)
    lo = (x - hi.astype(F32)).astype(BF16)
    return (jnp.dot(lhs_bf, hi, preferred_element_type=F32)
            + jnp.dot(lhs_bf, lo, preferred_element_type=F32))


def _group_sum(x, gmat):
    hi = x.astype(BF16)
    lo = (x - hi.astype(F32)).astype(BF16)
    return (jnp.dot(hi, gmat, preferred_element_type=F32)
            + jnp.dot(lo, gmat, preferred_element_type=F32))


def _group_matrix(n):
    r = lax.broadcasted_iota(jnp.int32, (n, n), 0) // RWKV_HEAD_DIM
    c = lax.broadcasted_iota(jnp.int32, (n, n), 1) // RWKV_HEAD_DIM
    return jnp.where(r == c, 1.0, 0.0).astype(BF16)


def _rope_table_kernel(invf_ref, cos_ref, sin_ref):
    rows = cos_ref.shape[0]
    base = pl.program_id(0) * rows
    pos = (base + lax.broadcasted_iota(jnp.int32, (rows, LANES), 0)).astype(F32)
    ang = pos * invf_ref[...]
    lane = lax.broadcasted_iota(jnp.int32, (rows, LANES), 1)
    s = jnp.sin(ang)
    cos_ref[...] = jnp.cos(ang)
    sin_ref[...] = jnp.where(lane < ATT_HEAD_DIM // 2, -s, s)


def _rope_tables(seq):
    half = ATT_HEAD_DIM // 2
    inv_freq = 1.0 / (ROPE_THETA ** (jnp.arange(half, dtype=F32) * 2.0 / ATT_HEAD_DIM))
    invf = jnp.concatenate([inv_freq, inv_freq])[None, :]
    rows = 512
    return pl.pallas_call(
        _rope_table_kernel,
        grid=(seq // rows,),
        in_specs=[pl.BlockSpec((1, LANES), lambda i: (0, 0))],
        out_specs=[pl.BlockSpec((rows, LANES), lambda i: (i, 0))] * 2,
        out_shape=[jax.ShapeDtypeStruct((seq, LANES), F32)] * 2,
        compiler_params=_cparams("arbitrary"),
        name="rope_table",
    )(invf)


def _att_in_kernel(x_ref, g_ref, wq_ref, wk_ref, wv_ref, cos_ref, sin_ref, q_ref, k_ref, v_ref, h_ref, acc_ref,
                   *, tm, tn, dil, scale):
    @pl.when(pl.program_id(1) == 0)
    def _():
        x = x_ref[...]
        h_ref[...] = (x * _rms_scale(x) * g_ref[...]).astype(BF16)

    h = h_ref[...]
    cos = cos_ref[...]
    sin = sin_ref[...]
    slot = 0
    for w_ref, o_ref, rope, sc in ((wq_ref, q_ref, True, scale), (wk_ref, k_ref, True, None),
                                   (wv_ref, v_ref, False, None)):
        for cb in range(tn // MXU_COLS):
            acc = jnp.dot(h, w_ref[:, cb * MXU_COLS:(cb + 1) * MXU_COLS], preferred_element_type=F32)
            for hh in range(MXU_COLS // LANES):
                sl = slice(cb * MXU_COLS + hh * LANES, cb * MXU_COLS + (hh + 1) * LANES)
                t = acc[:, hh * LANES:(hh + 1) * LANES]
                if rope:
                    t = t * cos + pltpu.roll(t, ATT_HEAD_DIM // 2, axis=1) * sin
                if sc is not None:
                    t = t * sc
                if dil == 1:
                    o_ref[0, :, sl] = t.astype(BF16)
                else:
                    acc_ref[slot] = t
                    for r in range(dil):
                        o_ref[r, :, sl] = acc_ref[slot, pl.ds(r, tm // dil, stride=dil), :].astype(BF16)
                slot += 1


def _att_in(x2, g, w_bf, cos_t, sin_t, seq, gidx, dil):
    rows = x2.shape[0]
    tm, tn = 512, 512
    tiles_per_seq = seq // tm
    ncol = D_MODEL // tn
    kern = functools.partial(_att_in_kernel, tm=tm, tn=tn, dil=dil, scale=ATT_HEAD_DIM ** -0.5)

    def wcol(part):
        return pl.BlockSpec((D_MODEL, tn), lambda i, j: (0, (gidx * 3 + part) * ncol + j))

    out_spec = pl.BlockSpec((dil, tm // dil, tn), lambda i, j: (0, i, j))
    out_sds = jax.ShapeDtypeStruct((dil, rows // dil, D_MODEL), BF16)
    return pl.pallas_call(
        kern,
        grid=(rows // tm, ncol),
        in_specs=[
            pl.BlockSpec((tm, D_MODEL), lambda i, j: (i, 0)),
            pl.BlockSpec((1, D_MODEL), lambda i, j: (0, 0)),
            wcol(0), wcol(1), wcol(2),
            pl.BlockSpec((tm, LANES), lambda i, j: (i % tiles_per_seq, 0)),
            pl.BlockSpec((tm, LANES), lambda i, j: (i % tiles_per_seq, 0)),
        ],
        out_specs=[out_spec] * 3,
        out_shape=[out_sds] * 3,
        scratch_shapes=[pltpu.VMEM((tm, D_MODEL), BF16), pltpu.VMEM((3 * tn // LANES, tm, LANES), F32)],
        compiler_params=_cparams("parallel", "arbitrary"),
        name=f"att_in_d{dil}",
    )(x2, g, w_bf, w_bf, w_bf, cos_t, sin_t)


def _gate_in_kernel(x_ref, g_ref, w_ref, o_ref):
    x = x_ref[...]
    h = (x * _rms_scale(x) * g_ref[...]).astype(BF16)
    o_ref[...] = jnp.dot(h, w_ref[...], preferred_element_type=F32).astype(BF16)


def _gate_in(x2, g, w_bf):
    rows = x2.shape[0]
    tm = 512
    col0 = 3 * len(ATT_GROUPS)
    return pl.pallas_call(
        _gate_in_kernel,
        grid=(rows // tm,),
        in_specs=[
            pl.BlockSpec((tm, D_MODEL), lambda i: (i, 0)),
            pl.BlockSpec((1, D_MODEL), lambda i: (0, 0)),
            pl.BlockSpec((D_MODEL, D_MODEL), lambda i: (0, col0)),
        ],
        out_specs=pl.BlockSpec((tm, D_MODEL), lambda i: (i, 0)),
        out_shape=jax.ShapeDtypeStruct((rows, D_MODEL), BF16),
        compiler_params=_cparams("parallel"),
        name="att_gate_in",
    )(x2, g, w_bf)


def _attn_kernel(q_ref, kp_ref, kc_ref, kn_ref, vp_ref, vc_ref, vn_ref, o_ref, lse_ref, *, bq, sub_len, nh):
    i = pl.program_id(3)
    nk = bq + 2 * ATT_HALF
    ii = lax.broadcasted_iota(jnp.int32, (bq, nk), 0)
    jj = lax.broadcasted_iota(jnp.int32, (bq, nk), 1)
    rel = jj - ii
    kpos = i * bq - ATT_HALF + jj
    valid = (rel >= 0) & (rel <= 2 * ATT_HALF) & (kpos >= 0) & (kpos < sub_len)

    def scores(hh):
        sl = slice(hh * LANES, (hh + 1) * LANES)
        kcat = jnp.concatenate([kp_ref[:, sl], kc_ref[:, sl], kn_ref[:, sl]], axis=0)
        s = lax.dot_general(q_ref[:, sl], kcat, NT_DIMS, preferred_element_type=F32)
        return jnp.where(valid, s, NEG_INF)

    def finish(hh, s):
        sl = slice(hh * LANES, (hh + 1) * LANES)
        vcat = jnp.concatenate([vp_ref[:, sl], vc_ref[:, sl], vn_ref[:, sl]], axis=0)
        m = jnp.max(s, axis=-1, keepdims=True)
        p = jnp.exp(s - m)
        l = jnp.sum(p, axis=-1, keepdims=True)
        o = jnp.dot(p.astype(BF16), vcat, preferred_element_type=F32)
        o_ref[:, sl] = (o / l).astype(BF16)
        lse_ref[:, sl] = jnp.broadcast_to(m + jnp.log(l), (bq, LANES))

    s_prev = scores(0)
    for hh in range(1, nh):
        s_next = scores(hh)
        finish(hh - 1, s_prev)
        s_prev = s_next
    finish(nh - 1, s_prev)


def _attn_group(q, k, v, dil, batch, seq):
    sub_len = seq // dil
    bq = min(256, sub_len)
    hw = 1024
    nh = hw // LANES
    hblocks = D_MODEL // hw
    nqb = sub_len // bq
    halo_per_q = bq // ATT_HALF
    n_halo = sub_len // ATT_HALF

    cur = pl.BlockSpec((None, bq, hw), lambda b, r, h, i: (r, b * nqb + i, h))
    prev = pl.BlockSpec((None, ATT_HALF, hw),
                        lambda b, r, h, i: (r, b * n_halo + jnp.maximum(i * halo_per_q - 1, 0), h))
    nxt = pl.BlockSpec((None, ATT_HALF, hw),
                       lambda b, r, h, i: (r, b * n_halo + jnp.minimum((i + 1) * halo_per_q, n_halo - 1), h))
    out_spec = cur
    rows = batch * sub_len
    kern = functools.partial(_attn_kernel, bq=bq, sub_len=sub_len, nh=nh)
    return pl.pallas_call(
        kern,
        grid=(batch, dil, hblocks, nqb),
        in_specs=[cur, prev, cur, nxt, prev, cur, nxt],
        out_specs=[out_spec, out_spec],
        out_shape=[jax.ShapeDtypeStruct((dil, rows, D_MODEL), BF16),
                   jax.ShapeDtypeStruct((dil, rows, D_MODEL), F32)],
        compiler_params=_cparams("parallel", "parallel", "parallel", "arbitrary"),
        name=f"attn_d{dil}",
    )(q, k, k, k, v, v, v)


def _proj_norm_residual(y_bf, w_ref, g_ref, x_ref, o_ref):
    out = jnp.dot(y_bf, w_ref[...], preferred_element_type=F32)
    o_ref[...] = x_ref[...] + out * _rms_scale(out) * g_ref[...]


def _att_out_kernel(o0_ref, o1_ref, o2_ref, l0_ref, l1_ref, l2_ref, z_ref, x_ref, w_ref, g_ref, o_ref,
                    so1, so2, sl1, sl2, y_ref, *, tm, dils):
    for cb in range(D_MODEL // LANES):
        sl = slice(cb * LANES, (cb + 1) * LANES)
        for src, dst, d in ((o1_ref, so1, dils[1]), (o2_ref, so2, dils[2]), (l1_ref, sl1, dils[1]),
                            (l2_ref, sl2, dils[2])):
            for r in range(d):
                dst[cb, pl.ds(r, tm // d, stride=d), :] = src[r, :, sl].astype(F32)
        l0, l1, l2 = l0_ref[0, :, sl], sl1[cb], sl2[cb]
        m = jnp.maximum(jnp.maximum(l0, l1), l2)
        e0, e1, e2 = jnp.exp(l0 - m), jnp.exp(l1 - m), jnp.exp(l2 - m)
        o = (e0 * o0_ref[0, :, sl].astype(F32) + e1 * so1[cb] + e2 * so2[cb]) / (e0 + e1 + e2)
        z = z_ref[:, sl].astype(F32)
        y_ref[:, sl] = (o * (z * _sigmoid(z))).astype(BF16)
    _proj_norm_residual(y_ref[...], w_ref, g_ref, x_ref, o_ref)


def _att_out(outs, lses, z, x2, w_bf, g):
    rows = x2.shape[0]
    tm = 256
    dils = tuple(d for _, d in ATT_GROUPS)
    row_spec = pl.BlockSpec((tm, D_MODEL), lambda i: (i, 0))

    def res_spec(d):
        return pl.BlockSpec((d, tm // d, D_MODEL), lambda i: (0, i, 0))

    kern = functools.partial(_att_out_kernel, tm=tm, dils=dils)
    return pl.pallas_call(
        kern,
        grid=(rows // tm,),
        in_specs=[res_spec(d) for d in dils] * 2 + [
            row_spec, row_spec,
            pl.BlockSpec((D_MODEL, D_MODEL), lambda i: (0, 0)),
            pl.BlockSpec((1, D_MODEL), lambda i: (0, 0)),
        ],
        out_specs=row_spec,
        out_shape=jax.ShapeDtypeStruct((rows, D_MODEL), F32),
        scratch_shapes=[pltpu.VMEM((D_MODEL // LANES, tm, LANES), F32)] * 4 + [pltpu.VMEM((tm, D_MODEL), BF16)],
        compiler_params=_cparams("parallel"),
        name="att_out",
    )(*outs, *lses, z, x2, w_bf, g)


def _chunk_tri(n, reverse):
    t = lax.broadcasted_iota(jnp.int32, (n, n), 0)
    s = lax.broadcasted_iota(jnp.int32, (n, n), 1)
    same = (t // CHUNK) == (s // CHUNK)
    order = (s >= t) if reverse else (s <= t)
    return jnp.where(same & order, 1.0, 0.0).astype(BF16)


def _rwkv_in_kernel(*refs, tm, tn, tiles_per_seq, has_vres):
    (x_ref, xp_ref, xn_ref, g_ref, mup_ref, mun_ref,
     wr_ref, wk_ref, wv_ref, wz_ref,
     w1_ref, w2_ref, w0_ref, a1_ref, a2_ref, a0_ref,
     kk_ref, ka_ref, rk_ref) = refs[:19]
    pos = 19
    if has_vres:
        v1_ref, v2_ref, v0_ref, vf_ref = refs[pos:pos + 4]
        pos += 4
    (kkq_out, rq_out, ad_out, kdk_out, kend_out, aend_out, est_out, v_out, bonus_out, sz_out) = refs[pos:pos + 10]
    pos += 10
    xs_ref, hw_ref, ha_ref = refs[pos:pos + 3]
    hv_ref = refs[pos + 3] if has_vres else None

    i = pl.program_id(0)
    j = pl.program_id(1)
    mix_slot = {0: 0, 2: 1, 3: 2, 5: 3}

    @pl.when(j == 0)
    def _():
        t_in_seq = i % tiles_per_seq
        keep_prev = jnp.where(t_in_seq == 0, 0.0, 1.0).astype(F32)
        keep_next = jnp.where(t_in_seq == tiles_per_seq - 1, 0.0, 1.0).astype(F32)
        sx = _rms_scale(x_ref[...])
        xp = xp_ref[7:8, :]
        xn = xn_ref[0:1, :]
        sp = _rms_scale(xp) * keep_prev
        sn = _rms_scale(xn) * keep_next
        cw = 512
        row = lax.broadcasted_iota(jnp.int32, (tm, cw), 0)
        hw_acc = [jnp.zeros((tm, LORA_PAD), F32) for _ in range(2)]
        ha_acc = [jnp.zeros((tm, LORA_PAD), F32) for _ in range(2)]
        hv_acc = jnp.zeros((tm, LORA_PAD), F32)
        for cb in range(D_MODEL // cw):
            sl = slice(cb * cw, (cb + 1) * cw)
            g = g_ref[:, sl]
            h = x_ref[:, sl] * sx * g
            hp_row = xp[:, sl] * sp * g
            hn_row = xn[:, sl] * sn * g
            h_prev = jnp.where(row == 0, hp_row, pltpu.roll(h, 1, axis=0))
            h_next = jnp.where(row == tm - 1, hn_row, pltpu.roll(h, tm - 1, axis=0))
            dp = h_prev - h
            dn = h_next - h
            mixes = {}
            for t in range(6):
                mixes[t] = (h + dp * mup_ref[t:t + 1, sl] + dn * mun_ref[t:t + 1, sl]).astype(BF16)
                if t in mix_slot:
                    xs_ref[mix_slot[t], :, sl] = mixes[t]
            for c in range(2):
                hw_acc[c] = hw_acc[c] + jnp.dot(mixes[1], w1_ref[c, sl, :], preferred_element_type=F32)
                ha_acc[c] = ha_acc[c] + jnp.dot(mixes[4], a1_ref[c, sl, :], preferred_element_type=F32)
            if has_vres:
                hv_acc = hv_acc + jnp.dot(mixes[3], v1_ref[sl, :], preferred_element_type=F32)
        for c in range(2):
            hw_ref[c] = jnp.tanh(hw_acc[c]).astype(BF16)
            ha_ref[c] = ha_acc[c].astype(BF16)
        if has_vres:
            hv_ref[...] = hv_acc.astype(BF16)

    r = jnp.dot(xs_ref[0], wr_ref[...], preferred_element_type=F32)
    k = jnp.dot(xs_ref[1], wk_ref[...], preferred_element_type=F32)
    v = jnp.dot(xs_ref[2], wv_ref[...], preferred_element_type=F32)
    z = jnp.dot(xs_ref[3], wz_ref[...], preferred_element_type=F32)
    if has_vres:
        gate = _sigmoid(v0_ref[...] + jnp.dot(hv_ref[...], v2_ref[...], preferred_element_type=F32))
        v = v + (vf_ref[...] - v) * gate
    gmat = _group_matrix(tn)
    kk = k * kk_ref[...]
    kk = kk * lax.rsqrt(jnp.maximum(_group_sum(kk * kk, gmat), 1e-24))
    k_a = ka_ref[...]
    rk_acc = jnp.zeros((tm, tn), F32)
    nchunk = tm // CHUNK
    half = 256
    for c in range(2):
        wl = w0_ref[c:c + 1, :] + jnp.dot(hw_ref[c], w2_ref[c], preferred_element_type=F32)
        lw = -math.exp(-0.5) * _sigmoid(wl)
        a = _sigmoid(a0_ref[c:c + 1, :] + jnp.dot(ha_ref[c], a2_ref[c], preferred_element_type=F32))
        kd = k * (1.0 + (a - 1.0) * k_a)
        rk_acc = rk_acc + r * kd * rk_ref[c:c + 1, :]
        tri = _chunk_tri(half, reverse=(c == 1))
        g = jnp.concatenate([_split_dot(tri, lw[hh * half:(hh + 1) * half]) for hh in range(tm // half)], axis=0)
        g3 = g.reshape(nchunk, CHUNK, tn)
        last = 0 if c == 1 else CHUNK - 1
        g_tot = g3[:, last:last + 1, :]
        e_tot = jnp.exp(g_tot)
        est_out[c] = e_tot.reshape(nchunk, tn)
        e_q = jnp.exp(g)
        e_qp = jnp.exp(g - lw)
        e_k = jnp.exp(-g)
        e_end = (e_k.reshape(nchunk, CHUNK, tn) * e_tot).reshape(tm, tn)
        kka = kk * a
        kkq_out[c] = (kk * e_qp).astype(BF16)
        rq_out[c] = (r * e_q).astype(BF16)
        ad_out[c] = (kka * e_k).astype(BF16)
        kdk_out[c] = (kd * e_k).astype(BF16)
        kend_out[c] = (kd * e_end).astype(BF16)
        aend_out[c] = (kka * e_end).astype(BF16)
    v_out[...] = v
    bonus_out[...] = _group_sum(rk_acc, gmat) * v
    sz_out[...] = (z * _sigmoid(z)).astype(BF16)


def _pad_lora(w1, w2):
    rank = w1.shape[-1]
    pad1 = [(0, 0)] * (w1.ndim - 1) + [(0, LORA_PAD - rank)]
    pad2 = [(0, 0)] * (w2.ndim - 2) + [(0, LORA_PAD - rank), (0, 0)]
    return jnp.pad(w1, pad1).astype(BF16), jnp.pad(w2, pad2).astype(BF16)


def _rwkv_in(x2, seq, g, mu_prev, mu_next, w_in_bf, w0, w1, w2, a0, a1, a2, k_k, k_a, r_k, vres, v_first):
    rows = x2.shape[0]
    tm, tn = 512, 256
    tiles_per_seq = seq // tm
    ncol = D_MODEL // tn
    has_vres = vres is not None
    w1p, w2p = _pad_lora(w1, w2)
    a1p, a2p = _pad_lora(a1, a2)
    sub = tm // 8
    nsub = rows // 8

    def const2(shape):
        return pl.BlockSpec(shape, lambda i, j: (0, 0))

    def col2(nrow):
        return pl.BlockSpec((nrow, tn), lambda i, j: (0, j))

    def wcol(gi):
        return pl.BlockSpec((D_MODEL, tn), lambda i, j: (0, gi * ncol + j))

    in_specs = [
        pl.BlockSpec((tm, D_MODEL), lambda i, j: (i, 0)),
        pl.BlockSpec((8, D_MODEL), lambda i, j: (jnp.maximum(i * sub - 1, 0), 0)),
        pl.BlockSpec((8, D_MODEL), lambda i, j: (jnp.minimum((i + 1) * sub, nsub - 1), 0)),
        const2((1, D_MODEL)), const2((6, D_MODEL)), const2((6, D_MODEL)),
        wcol(0), wcol(1), wcol(2), wcol(3),
        pl.BlockSpec((2, D_MODEL, LORA_PAD), lambda i, j: (0, 0, 0)),
        pl.BlockSpec((2, LORA_PAD, tn), lambda i, j: (0, 0, j)),
        col2(2),
        pl.BlockSpec((2, D_MODEL, LORA_PAD), lambda i, j: (0, 0, 0)),
        pl.BlockSpec((2, LORA_PAD, tn), lambda i, j: (0, 0, j)),
        col2(2),
        col2(1), col2(1), col2(2),
    ]
    args = [x2, x2, x2, g, mu_prev, mu_next, w_in_bf, w_in_bf, w_in_bf, w_in_bf,
            w1p, w2p, w0, a1p, a2p, a0, k_k, k_a, r_k]
    if has_vres:
        v0, v1, v2 = vres
        v1p, v2p = _pad_lora(v1, v2)
        in_specs += [const2((D_MODEL, LORA_PAD)), col2(LORA_PAD), col2(1),
                     pl.BlockSpec((tm, tn), lambda i, j: (i, j))]
        args += [v1p, v2p, v0, v_first]

    tile = pl.BlockSpec((tm, tn), lambda i, j: (i, j))
    tile2 = pl.BlockSpec((2, tm, tn), lambda i, j: (0, i, j))
    est_spec = pl.BlockSpec((2, tm // CHUNK, tn), lambda i, j: (0, i, j))
    sds_bf2 = jax.ShapeDtypeStruct((2, rows, D_MODEL), BF16)
    sds = jax.ShapeDtypeStruct((rows, D_MODEL), F32)
    scratch = [pltpu.VMEM((4, tm, D_MODEL), BF16), pltpu.VMEM((2, tm, LORA_PAD), BF16),
               pltpu.VMEM((2, tm, LORA_PAD), BF16)]
    if has_vres:
        scratch.append(pltpu.VMEM((tm, LORA_PAD), BF16))
    kern = functools.partial(_rwkv_in_kernel, tm=tm, tn=tn, tiles_per_seq=tiles_per_seq, has_vres=has_vres)
    return pl.pallas_call(
        kern,
        grid=(rows // tm, ncol),
        in_specs=in_specs,
        out_specs=[tile2] * 6 + [est_spec, tile, tile, tile],
        out_shape=[sds_bf2] * 6 + [jax.ShapeDtypeStruct((2, rows // CHUNK, D_MODEL), F32), sds, sds,
                                   jax.ShapeDtypeStruct((rows, D_MODEL), BF16)],
        scratch_shapes=scratch,
        compiler_params=_cparams("parallel", "arbitrary"),
        name="rwkv_in_vres" if has_vres else "rwkv_in",
    )(*args)


def _rwkv_scan_kernel(kkq_ref, rq_ref, ad_ref, kdk_ref, kend_ref, aend_ref, v_ref, est_ref, y_ref, state_ref,
                      *, tb, npair):
    rev = pl.program_id(1) == 1
    step = pl.program_id(3)
    nchunk = tb // CHUNK
    pairs = range(npair)

    @pl.when(step == 0)
    def _():
        state_ref[...] = jnp.zeros_like(state_ref)

    t = lax.broadcasted_iota(jnp.int32, (CHUNK, LANES), 0)
    lane = lax.broadcasted_iota(jnp.int32, (CHUNK, LANES), 1)
    s = lane & (CHUNK - 1)
    d = jnp.where(rev, s - t, t - s)
    strict = d > 0
    incl = d >= 0
    eye = jnp.where(d == 0, 1.0, 0.0).astype(F32)
    first_half = lane < CHUNK
    half_a = jnp.where(first_half, 1.0, 0.0).astype(BF16)
    half_b = jnp.where(first_half, 0.0, 1.0).astype(BF16)
    rr = lax.broadcasted_iota(jnp.int32, (LANES, LANES), 0)
    cc = lax.broadcasted_iota(jnp.int32, (LANES, LANES), 1)
    bd_mask = (rr < CHUNK) == (cc < CHUNK)
    est_rows = lax.broadcasted_iota(jnp.int32, (nchunk, npair * LANES), 0)

    def bd16(y):
        return jnp.concatenate([y * half_a, y * half_b], axis=0)

    def bd32(y):
        return jnp.concatenate([jnp.where(first_half, y, 0.0), jnp.where(first_half, 0.0, y)],
                               axis=0).astype(BF16)

    def mm(a, b):
        return jnp.dot(a, b, preferred_element_type=F32)

    def body(ci, carry):
        first = ci * CHUNKS_PER_TRIP
        cs = [jnp.where(rev, nchunk - 1 - (first + u), first + u) for u in range(CHUNKS_PER_TRIP)]
        jobs = [(pl.ds(pl.multiple_of(c * CHUNK, CHUNK), CHUNK), slice(p * LANES, (p + 1) * LANES))
                for c in cs for p in pairs]
        n = range(len(jobs))
        kkq = [kkq_ref[rw, ln] for rw, ln in jobs]
        rq = [rq_ref[rw, ln] for rw, ln in jobs]
        v = [v_ref[rw, ln] for rw, ln in jobs]
        gram = [lax.dot_general(
            jnp.concatenate([kkq[i], rq[i]], axis=0),
            jnp.concatenate([bd16(ad_ref[jobs[i]]), bd16(kdk_ref[jobs[i]])], axis=0),
            NT_DIMS, preferred_element_type=F32) for i in n]
        a_mat = [jnp.where(strict, gram[i][0:CHUNK, 0:LANES], 0.0) for i in n]
        ak = [jnp.where(strict, gram[i][0:CHUNK, LANES:], 0.0) for i in n]
        bra = [jnp.where(incl, gram[i][CHUNK:, 0:LANES], 0.0).astype(BF16) for i in n]
        brk = [jnp.where(incl, gram[i][CHUNK:, LANES:], 0.0) for i in n]
        early = [mm(jnp.concatenate([ak[i], brk[i]], axis=0).astype(BF16), bd32(v[i])) for i in n]
        vk = [lax.dot_general(v[i].astype(BF16), kend_ref[jobs[i]], TN_DIMS, preferred_element_type=F32)
              for i in n]
        a_pow = [mm(a_mat[i].astype(BF16), bd32(a_mat[i])) for i in n]
        inv = [eye - a_mat[i] for i in n]
        for _ in range(4):
            st = [mm(jnp.concatenate([inv[i], a_pow[i]], axis=0).astype(BF16), bd32(a_pow[i])) for i in n]
            inv = [inv[i] + st[i][0:CHUNK] for i in n]
            a_pow = [st[i][CHUNK:] for i in n]
        inv = [inv[i] + mm(inv[i].astype(BF16), bd32(a_pow[i])) for i in n]
        wu = [mm(inv[i].astype(BF16), jnp.concatenate([bd16(kkq[i]), bd32(early[i][0:CHUNK])], axis=1))
              for i in n]
        ry = [mm(bra[i], jnp.concatenate([bd32(wu[i][:, 0:LANES]), bd32(wu[i][:, LANES:])], axis=1))
              for i in n]
        tn = [lax.dot_general(wu[i].astype(BF16), aend_ref[jobs[i]], TN_DIMS, preferred_element_type=F32)
              for i in n]
        rqp = [(rq[i].astype(F32) - ry[i][:, 0:LANES]).astype(BF16) for i in n]
        yv = [early[i][CHUNK:] - ry[i][:, LANES:] for i in n]
        xmat = [jnp.where(bd_mask, tn[i][0:LANES], 0.0).astype(BF16) for i in n]
        nmat = [jnp.where(bd_mask, vk[i] - tn[i][LANES:], 0.0) for i in n]
        est_all = est_ref[...]
        state = [state_ref[p] for p in pairs]
        for u in range(CHUNKS_PER_TRIP):
            est_row = jnp.sum(jnp.where(est_rows == cs[u], est_all, 0.0), axis=0, keepdims=True)
            idx = [u * npair + p for p in pairs]
            s16 = [state[p].astype(BF16) for p in pairs]
            ys = [yv[idx[p]] + lax.dot_general(rqp[idx[p]], s16[p], NT_DIMS, preferred_element_type=F32)
                  for p in pairs]
            sx = [mm(s16[p], xmat[idx[p]]) for p in pairs]
            for p in pairs:
                rw, ln = jobs[idx[p]]
                y_ref[rw, ln] = ys[p]
                state[p] = state[p] * est_row[:, ln] - sx[p] + nmat[idx[p]]
        for p in pairs:
            state_ref[p] = state[p]
        return carry

    lax.fori_loop(0, nchunk // CHUNKS_PER_TRIP, body, 0)


def _rwkv_scan(kkq, rq, ad, kdk, kend, aend, est, v, batch, seq):
    tb, lw_lanes = 512, 1024
    nt = seq // tb
    npair = lw_lanes // LANES

    def tmap(i, dr):
        return i + dr * (nt - 1 - 2 * i)

    v_spec = pl.BlockSpec((tb, lw_lanes), lambda b, dr, h, i: (b * nt + tmap(i, dr), h))
    d_spec = pl.BlockSpec((None, tb, lw_lanes), lambda b, dr, h, i: (dr, b * nt + tmap(i, dr), h))
    e_spec = pl.BlockSpec((None, tb // CHUNK, lw_lanes), lambda b, dr, h, i: (dr, b * nt + tmap(i, dr), h))
    kern = functools.partial(_rwkv_scan_kernel, tb=tb, npair=npair)
    return pl.pallas_call(
        kern,
        grid=(batch, 2, D_MODEL // lw_lanes, nt),
        in_specs=[d_spec] * 6 + [v_spec, e_spec],
        out_specs=d_spec,
        out_shape=jax.ShapeDtypeStruct((2, batch * seq, D_MODEL), F32),
        scratch_shapes=[pltpu.VMEM((npair, LANES, LANES), F32)],
        compiler_params=_cparams("parallel", "parallel", "parallel", "arbitrary"),
        name="rwkv_scan",
    )(kkq, rq, ad, kdk, kend, aend, v, est)


def _rwkv_out_kernel(y_ref, bonus_ref, sz_ref, gg_ref, gb_ref, x_ref, w_ref, g_ref, o_ref, yb_ref):
    gmat = _group_matrix(LANES)
    inv_n = 1.0 / RWKV_HEAD_DIM
    for cb in range(D_MODEL // LANES):
        sl = slice(cb * LANES, (cb + 1) * LANES)
        y = y_ref[0, :, sl] + y_ref[1, :, sl]
        mu = _group_sum(y, gmat) * inv_n
        yc = y - mu
        var = _group_sum(yc * yc, gmat) * inv_n
        yn = yc * lax.rsqrt(var + GN_EPS) * gg_ref[:, sl] + gb_ref[:, sl]
        yb_ref[:, sl] = ((yn + bonus_ref[:, sl]) * sz_ref[:, sl].astype(F32)).astype(BF16)
    _proj_norm_residual(yb_ref[...], w_ref, g_ref, x_ref, o_ref)


def _rwkv_out(y2, bonus, sz, gn_g, gn_b, x2, w_bf, g):
    rows = x2.shape[0]
    tm = 256
    row_spec = pl.BlockSpec((tm, D_MODEL), lambda i: (i, 0))
    vec = pl.BlockSpec((1, D_MODEL), lambda i: (0, 0))
    return pl.pallas_call(
        _rwkv_out_kernel,
        grid=(rows // tm,),
        in_specs=[pl.BlockSpec((2, tm, D_MODEL), lambda i: (0, i, 0)), row_spec, row_spec, vec, vec, row_spec,
                  pl.BlockSpec((D_MODEL, D_MODEL), lambda i: (0, 0)), vec],
        out_specs=row_spec,
        out_shape=jax.ShapeDtypeStruct((rows, D_MODEL), F32),
        scratch_shapes=[pltpu.VMEM((tm, D_MODEL), BF16)],
        compiler_params=_cparams("parallel"),
        name="rwkv_out",
    )(y2, bonus, sz, gn_g, gn_b, x2, w_bf, g)


def _trunk(x, p, rope):
    batch, seq, _ = x.shape
    x2 = x.reshape(batch * seq, D_MODEL)
    cos_t, sin_t = rope
    v_first = None
    depth = p["norm_pre"].shape[0]
    for layer in range(depth):
        j = layer // 2
        g_pre = p["norm_pre"][layer][None, :]
        g_post = p["norm_post"][layer][None, :]
        if layer % 2 == 0:
            w_in = p["att_w_in"][j]
            outs, lses = [], []
            for gidx, (_, dil) in enumerate(ATT_GROUPS):
                q, k, v = _att_in(x2, g_pre, w_in, cos_t, sin_t, seq, gidx, dil)
                o, lse = _attn_group(q, k, v, dil, batch, seq)
                outs.append(o)
                lses.append(lse)
            z = _gate_in(x2, g_pre, w_in)
            x2 = _att_out(outs, lses, z, x2, p["att_w_out"][j], g_post)
        else:
            vres = None if j == 0 else (p["rwkv_v0"][j - 1][None, :], p["rwkv_v1"][j - 1], p["rwkv_v2"][j - 1])
            kkq, rq, ad, kdk, kend, aend, est, v, bonus, sz = _rwkv_in(
                x2, seq, g_pre, p["rwkv_mu_prev"][j], p["rwkv_mu_next"][j], p["rwkv_w_in"][j],
                p["rwkv_w0"][j], p["rwkv_w1"][j], p["rwkv_w2"][j],
                p["rwkv_a0"][j], p["rwkv_a1"][j], p["rwkv_a2"][j],
                p["rwkv_k_k"][j][None, :], p["rwkv_k_a"][j][None, :], p["rwkv_r_k"][j].reshape(2, D_MODEL),
                vres, v_first)
            if j == 0:
                v_first = v
            y2 = _rwkv_scan(kkq, rq, ad, kdk, kend, aend, est, v, batch, seq)
            x2 = _rwkv_out(y2, bonus, sz, p["rwkv_gn_g"][j][None, :], p["rwkv_gn_b"][j][None, :], x2,
                           p["rwkv_w_out"][j], g_post)
    return x2.reshape(batch, seq, D_MODEL)


def kernel(x_prompt, x_sample, norm_pre, norm_post, att_w_in, att_w_out, rwkv_mu_prev, rwkv_mu_next, rwkv_w_in, rwkv_w0, rwkv_w1, rwkv_w2, rwkv_a0, rwkv_a1, rwkv_a2, rwkv_v0, rwkv_v1, rwkv_v2, rwkv_k_k, rwkv_k_a, rwkv_r_k, rwkv_gn_g, rwkv_gn_b, rwkv_w_out):
    p = dict(
        norm_pre=norm_pre, norm_post=norm_post,
        att_w_in=att_w_in.astype(BF16), att_w_out=att_w_out.astype(BF16),
        rwkv_mu_prev=rwkv_mu_prev, rwkv_mu_next=rwkv_mu_next, rwkv_w_in=rwkv_w_in.astype(BF16),
        rwkv_w0=rwkv_w0, rwkv_w1=rwkv_w1, rwkv_w2=rwkv_w2,
        rwkv_a0=rwkv_a0, rwkv_a1=rwkv_a1, rwkv_a2=rwkv_a2,
        rwkv_v0=rwkv_v0, rwkv_v1=rwkv_v1, rwkv_v2=rwkv_v2,
        rwkv_k_k=rwkv_k_k, rwkv_k_a=rwkv_k_a, rwkv_r_k=rwkv_r_k,
        rwkv_gn_g=rwkv_gn_g, rwkv_gn_b=rwkv_gn_b, rwkv_w_out=rwkv_w_out.astype(BF16),
    )
    rope = _rope_tables(max(x_prompt.shape[1], x_sample.shape[1]))
    return (_trunk(x_prompt, p, rope), _trunk(x_sample, p, rope))
```

```python
import functools
import math

import jax
import jax.numpy as jnp
from jax import lax
from jax.experimental import pallas as pl
from jax.experimental.pallas import tpu as pltpu

F32 = jnp.float32
BF16 = jnp.bfloat16

D_MODEL = 2048
LANES = 128
MXU_COLS = 256
ATT_HEAD_DIM = 128
ATT_GROUPS = ((128, 1), (512, 4), (2048, 16))
ATT_HALF = 64
ROPE_THETA = 10000.0
RWKV_HEAD_DIM = 64
LORA_PAD = 128
RMS_EPS = 1e-6
GN_EPS = 64e-5
NEG_INF = -1e30
CHUNK = 64
CHUNKS_PER_TRIP = 2
VMEM_LIMIT_BYTES = 56 * 1024 * 1024

NT_DIMS = (((1,), (1,)), ((), ()))
TN_DIMS = (((0,), (0,)), ((), ()))


def _cparams(*sem):
    return pltpu.CompilerParams(dimension_semantics=sem, vmem_limit_bytes=VMEM_LIMIT_BYTES)


def _rms_scale(x):
    return lax.rsqrt(jnp.mean(x * x, axis=-1, keepdims=True) + RMS_EPS)


def _sigmoid(x):
    return 0.5 * jnp.tanh(0.5 * x) + 0.5


def _split_dot(lhs_bf, x):
    hi = x.astype(BF16)
    lo = (x - hi.astype(F32)).astype(BF16)
    return (jnp.dot(lhs_bf, hi, preferred_element_type=F32)
            + jnp.dot(lhs_bf, lo, preferred_element_type=F32))


def _group_sum(x, gmat):
    hi = x.astype(BF16)
    lo = (x - hi.astype(F32)).astype(BF16)
    return (jnp.dot(hi, gmat, preferred_element_type=F32)
            + jnp.dot(lo, gmat, preferred_element_type=F32))


def _group_matrix(n):
    r = lax.broadcasted_iota(jnp.int32, (n, n), 0) // RWKV_HEAD_DIM
    c = lax.broadcasted_iota(jnp.int32, (n, n), 1) // RWKV_HEAD_DIM
    return jnp.where(r == c, 1.0, 0.0).astype(BF16)


def _rope_table_kernel(invf_ref, cos_ref, sin_ref):
    rows = cos_ref.shape[0]
    base = pl.program_id(0) * rows
    pos = (base + lax.broadcasted_iota(jnp.int32, (rows, LANES), 0)).astype(F32)
    ang = pos * invf_ref[...]
    lane = lax.broadcasted_iota(jnp.int32, (rows, LANES), 1)
    s = jnp.sin(ang)
    cos_ref[...] = jnp.cos(ang)
    sin_ref[...] = jnp.where(lane < ATT_HEAD_DIM // 2, -s, s)


def _rope_tables(seq):
    half = ATT_HEAD_DIM // 2
    inv_freq = 1.0 / (ROPE_THETA ** (jnp.arange(half, dtype=F32) * 2.0 / ATT_HEAD_DIM))
    invf = jnp.concatenate([inv_freq, inv_freq])[None, :]
    rows = 512
    return pl.pallas_call(
        _rope_table_kernel,
        grid=(seq // rows,),
        in_specs=[pl.BlockSpec((1, LANES), lambda i: (0, 0))],
        out_specs=[pl.BlockSpec((rows, LANES), lambda i: (i, 0))] * 2,
        out_shape=[jax.ShapeDtypeStruct((seq, LANES), F32)] * 2,
        compiler_params=_cparams("arbitrary"),
        name="rope_table",
    )(invf)


def _att_in_kernel(x_ref, g_ref, wq_ref, wk_ref, wv_ref, cos_ref, sin_ref, q_ref, k_ref, v_ref, h_ref, acc_ref,
                   *, tm, tn, dil, scale):
    @pl.when(pl.program_id(1) == 0)
    def _():
        x = x_ref[...]
        h_ref[...] = (x * _rms_scale(x) * g_ref[...]).astype(BF16)

    h = h_ref[...]
    cos = cos_ref[...]
    sin = sin_ref[...]
    slot = 0
    for w_ref, o_ref, rope, sc in ((wq_ref, q_ref, True, scale), (wk_ref, k_ref, True, None),
                                   (wv_ref, v_ref, False, None)):
        for cb in range(tn // MXU_COLS):
            acc = jnp.dot(h, w_ref[:, cb * MXU_COLS:(cb + 1) * MXU_COLS], preferred_element_type=F32)
            for hh in range(MXU_COLS // LANES):
                sl = slice(cb * MXU_COLS + hh * LANES, cb * MXU_COLS + (hh + 1) * LANES)
                t = acc[:, hh * LANES:(hh + 1) * LANES]
                if rope:
                    t = t * cos + pltpu.roll(t, ATT_HEAD_DIM // 2, axis=1) * sin
                if sc is not None:
                    t = t * sc
                if dil == 1:
                    o_ref[0, :, sl] = t.astype(BF16)
                else:
                    acc_ref[slot] = t
                    for r in range(dil):
                        o_ref[r, :, sl] = acc_ref[slot, pl.ds(r, tm // dil, stride=dil), :].astype(BF16)
                slot += 1


def _att_in(x2, g, w_bf, cos_t, sin_t, seq, gidx, dil):
    rows = x2.shape[0]
    tm, tn = 512, 512
    tiles_per_seq = seq // tm
    ncol = D_MODEL // tn
    kern = functools.partial(_att_in_kernel, tm=tm, tn=tn, dil=dil, scale=ATT_HEAD_DIM ** -0.5)

    def wcol(part):
        return pl.BlockSpec((D_MODEL, tn), lambda i, j: (0, (gidx * 3 + part) * ncol + j))

    out_spec = pl.BlockSpec((dil, tm // dil, tn), lambda i, j: (0, i, j))
    out_sds = jax.ShapeDtypeStruct((dil, rows // dil, D_MODEL), BF16)
    return pl.pallas_call(
        kern,
        grid=(rows // tm, ncol),
        in_specs=[
            pl.BlockSpec((tm, D_MODEL), lambda i, j: (i, 0)),
            pl.BlockSpec((1, D_MODEL), lambda i, j: (0, 0)),
            wcol(0), wcol(1), wcol(2),
            pl.BlockSpec((tm, LANES), lambda i, j: (i % tiles_per_seq, 0)),
            pl.BlockSpec((tm, LANES), lambda i, j: (i % tiles_per_seq, 0)),
        ],
        out_specs=[out_spec] * 3,
        out_shape=[out_sds] * 3,
        scratch_shapes=[pltpu.VMEM((tm, D_MODEL), BF16), pltpu.VMEM((3 * tn // LANES, tm, LANES), F32)],
        compiler_params=_cparams("parallel", "arbitrary"),
        name=f"att_in_d{dil}",
    )(x2, g, w_bf, w_bf, w_bf, cos_t, sin_t)


def _gate_in_kernel(x_ref, g_ref, w_ref, o_ref):
    x = x_ref[...]
    h = (x * _rms_scale(x) * g_ref[...]).astype(BF16)
    o_ref[...] = jnp.dot(h, w_ref[...], preferred_element_type=F32).astype(BF16)


def _gate_in(x2, g, w_bf):
    rows = x2.shape[0]
    tm = 512
    col0 = 3 * len(ATT_GROUPS)
    return pl.pallas_call(
        _gate_in_kernel,
        grid=(rows // tm,),
        in_specs=[
            pl.BlockSpec((tm, D_MODEL), lambda i: (i, 0)),
            pl.BlockSpec((1, D_MODEL), lambda i: (0, 0)),
            pl.BlockSpec((D_MODEL, D_MODEL), lambda i: (0, col0)),
        ],
        out_specs=pl.BlockSpec((tm, D_MODEL), lambda i: (i, 0)),
        out_shape=jax.ShapeDtypeStruct((rows, D_MODEL), BF16),
        compiler_params=_cparams("parallel"),
        name="att_gate_in",
    )(x2, g, w_bf)


def _attn_kernel(q_ref, kp_ref, kc_ref, kn_ref, vp_ref, vc_ref, vn_ref, o_ref, lse_ref, *, bq, sb, sub_len, nh):
    i = pl.program_id(3)
    nk = sb + 2 * ATT_HALF
    nsub = bq // sb
    ii = lax.broadcasted_iota(jnp.int32, (sb, nk), 0)
    jj = lax.broadcasted_iota(jnp.int32, (sb, nk), 1)
    rel = jj - ii
    in_band = (rel >= 0) & (rel <= 2 * ATT_HALF)
    valid = []
    for u in range(nsub):
        kpos = i * bq + u * sb - ATT_HALF + jj
        valid.append(in_band & (kpos >= 0) & (kpos < sub_len))
    lane = lax.broadcasted_iota(jnp.int32, (sb, LANES), 1)
    units = [(hh, u) for hh in range(nh) for u in range(nsub)]

    def scores(hh, u):
        sl = slice(hh * LANES, (hh + 1) * LANES)
        kcat = jnp.concatenate([kp_ref[:, sl], kc_ref[:, sl], kn_ref[:, sl]], axis=0)
        s = lax.dot_general(q_ref[u * sb:(u + 1) * sb, sl], kcat[u * sb:u * sb + nk], NT_DIMS,
                            preferred_element_type=F32)
        return jnp.where(valid[u], s, NEG_INF)

    def finish(hh, u, s, lse_acc):
        sl = slice(hh * LANES, (hh + 1) * LANES)
        vcat = jnp.concatenate([vp_ref[:, sl], vc_ref[:, sl], vn_ref[:, sl]], axis=0)
        m = jnp.max(s, axis=-1, keepdims=True)
        p = jnp.exp(s - m)
        l = jnp.sum(p, axis=-1, keepdims=True)
        o = jnp.dot(p.astype(BF16), vcat[u * sb:u * sb + nk], preferred_element_type=F32)
        o_ref[u * sb:(u + 1) * sb, sl] = (o / l).astype(BF16)
        lse_acc[u] = jnp.where(lane == hh, m + jnp.log(l), lse_acc[u])

    lse_acc = [jnp.zeros((sb, LANES), F32) for _ in range(nsub)]
    s_prev = scores(*units[0])
    for k in range(1, len(units)):
        s_next = scores(*units[k])
        finish(*units[k - 1], s_prev, lse_acc)
        s_prev = s_next
    finish(*units[-1], s_prev, lse_acc)
    for u in range(nsub):
        lse_ref[u * sb:(u + 1) * sb, :] = lse_acc[u]


def _attn_group(q, k, v, dil, batch, seq):
    sub_len = seq // dil
    bq = min(256, sub_len)
    sb = min(128, bq)
    hw = 1024
    nh = hw // LANES
    hblocks = D_MODEL // hw
    nqb = sub_len // bq
    halo_per_q = bq // ATT_HALF
    n_halo = sub_len // ATT_HALF

    cur = pl.BlockSpec((None, bq, hw), lambda b, r, h, i: (r, b * nqb + i, h))
    prev = pl.BlockSpec((None, ATT_HALF, hw),
                        lambda b, r, h, i: (r, b * n_halo + jnp.maximum(i * halo_per_q - 1, 0), h))
    nxt = pl.BlockSpec((None, ATT_HALF, hw),
                       lambda b, r, h, i: (r, b * n_halo + jnp.minimum((i + 1) * halo_per_q, n_halo - 1), h))
    lse_spec = pl.BlockSpec((None, bq, LANES), lambda b, r, h, i: (r, b * nqb + i, h))
    rows = batch * sub_len
    kern = functools.partial(_attn_kernel, bq=bq, sb=sb, sub_len=sub_len, nh=nh)
    return pl.pallas_call(
        kern,
        grid=(batch, dil, hblocks, nqb),
        in_specs=[cur, prev, cur, nxt, prev, cur, nxt],
        out_specs=[cur, lse_spec],
        out_shape=[jax.ShapeDtypeStruct((dil, rows, D_MODEL), BF16),
                   jax.ShapeDtypeStruct((dil, rows, hblocks * LANES), F32)],
        compiler_params=_cparams("parallel", "parallel", "parallel", "arbitrary"),
        name=f"attn_d{dil}",
    )(q, k, k, k, v, v, v)


def _proj_norm_residual(y_bf, w_ref, g_ref, x_ref, o_ref):
    out = jnp.dot(y_bf, w_ref[...], preferred_element_type=F32)
    o_ref[...] = x_ref[...] + out * _rms_scale(out) * g_ref[...]


def _att_out_kernel(o0_ref, o1_ref, o2_ref, l0_ref, l1_ref, l2_ref, z_ref, x_ref, w_ref, g_ref, o_ref,
                    so1, so2, sl1, sl2, y_ref, *, tm, dils, heads_per_block):
    nlb = l0_ref.shape[-1] // LANES
    for src, dst, d in ((l1_ref, sl1, dils[1]), (l2_ref, sl2, dils[2])):
        for lb in range(nlb):
            for r in range(d):
                dst[lb, pl.ds(r, tm // d, stride=d), :] = src[r, :, lb * LANES:(lb + 1) * LANES]
    wts = []
    for lb in range(nlb):
        l0, l1, l2 = l0_ref[0, :, lb * LANES:(lb + 1) * LANES], sl1[lb], sl2[lb]
        m = jnp.maximum(jnp.maximum(l0, l1), l2)
        e0, e1, e2 = jnp.exp(l0 - m), jnp.exp(l1 - m), jnp.exp(l2 - m)
        inv = 1.0 / (e0 + e1 + e2)
        wts.append((e0 * inv, e1 * inv, e2 * inv))
    for cb in range(D_MODEL // LANES):
        sl = slice(cb * LANES, (cb + 1) * LANES)
        for src, dst, d in ((o1_ref, so1, dils[1]), (o2_ref, so2, dils[2])):
            for r in range(d):
                dst[cb, pl.ds(r, tm // d, stride=d), :] = src[r, :, sl].astype(F32)
        w0, w1, w2 = wts[cb // heads_per_block]
        hl = cb % heads_per_block
        bc = lambda w: jnp.broadcast_to(w[:, hl:hl + 1], (tm, LANES))
        o = bc(w0) * o0_ref[0, :, sl].astype(F32) + bc(w1) * so1[cb] + bc(w2) * so2[cb]
        z = z_ref[:, sl].astype(F32)
        y_ref[:, sl] = (o * (z * _sigmoid(z))).astype(BF16)
    _proj_norm_residual(y_ref[...], w_ref, g_ref, x_ref, o_ref)


def _att_out(outs, lses, z, x2, w_bf, g):
    rows = x2.shape[0]
    tm = 256
    dils = tuple(d for _, d in ATT_GROUPS)
    lse_w = lses[0].shape[-1]
    nlb = lse_w // LANES
    row_spec = pl.BlockSpec((tm, D_MODEL), lambda i: (i, 0))

    def res_spec(d, width):
        return pl.BlockSpec((d, tm // d, width), lambda i: (0, i, 0))

    kern = functools.partial(_att_out_kernel, tm=tm, dils=dils, heads_per_block=D_MODEL // LANES // nlb)
    return pl.pallas_call(
        kern,
        grid=(rows // tm,),
        in_specs=[res_spec(d, D_MODEL) for d in dils] + [res_spec(d, lse_w) for d in dils] + [
            row_spec, row_spec,
            pl.BlockSpec((D_MODEL, D_MODEL), lambda i: (0, 0)),
            pl.BlockSpec((1, D_MODEL), lambda i: (0, 0)),
        ],
        out_specs=row_spec,
        out_shape=jax.ShapeDtypeStruct((rows, D_MODEL), F32),
        scratch_shapes=[pltpu.VMEM((D_MODEL // LANES, tm, LANES), F32)] * 2
        + [pltpu.VMEM((nlb, tm, LANES), F32)] * 2 + [pltpu.VMEM((tm, D_MODEL), BF16)],
        compiler_params=_cparams("parallel"),
        name="att_out",
    )(*outs, *lses, z, x2, w_bf, g)


def _chunk_tri(n, reverse):
    t = lax.broadcasted_iota(jnp.int32, (n, n), 0)
    s = lax.broadcasted_iota(jnp.int32, (n, n), 1)
    same = (t // CHUNK) == (s // CHUNK)
    order = (s >= t) if reverse else (s <= t)
    return jnp.where(same & order, 1.0, 0.0).astype(BF16)


def _rwkv_in_kernel(*refs, tm, tn, tiles_per_seq, has_vres):
    (x_ref, xp_ref, xn_ref, g_ref, mup_ref, mun_ref,
     wr_ref, wk_ref, wv_ref, wz_ref,
     w1_ref, w2_ref, w0_ref, a1_ref, a2_ref, a0_ref,
     kk_ref, ka_ref, rk_ref) = refs[:19]
    pos = 19
    if has_vres:
        v1_ref, v2_ref, v0_ref, vf_ref = refs[pos:pos + 4]
        pos += 4
    (kkq_out, rq_out, ad_out, kdk_out, kend_out, aend_out, est_out, v_out, bonus_out, sz_out) = refs[pos:pos + 10]
    pos += 10
    xs_ref, hw_ref, ha_ref = refs[pos:pos + 3]
    hv_ref = refs[pos + 3] if has_vres else None

    i = pl.program_id(0)
    j = pl.program_id(1)
    mix_slot = {0: 0, 2: 1, 3: 2, 5: 3}

    @pl.when(j == 0)
    def _():
        t_in_seq = i % tiles_per_seq
        keep_prev = jnp.where(t_in_seq == 0, 0.0, 1.0).astype(F32)
        keep_next = jnp.where(t_in_seq == tiles_per_seq - 1, 0.0, 1.0).astype(F32)
        sx = _rms_scale(x_ref[...])
        xp = xp_ref[7:8, :]
        xn = xn_ref[0:1, :]
        sp = _rms_scale(xp) * keep_prev
        sn = _rms_scale(xn) * keep_next
        cw = 512
        row = lax.broadcasted_iota(jnp.int32, (tm, cw), 0)
        hw_acc = [jnp.zeros((tm, LORA_PAD), F32) for _ in range(2)]
        ha_acc = [jnp.zeros((tm, LORA_PAD), F32) for _ in range(2)]
        hv_acc = jnp.zeros((tm, LORA_PAD), F32)
        for cb in range(D_MODEL // cw):
            sl = slice(cb * cw, (cb + 1) * cw)
            g = g_ref[:, sl]
            h = x_ref[:, sl] * sx * g
            hp_row = xp[:, sl] * sp * g
            hn_row = xn[:, sl] * sn * g
            h_prev = jnp.where(row == 0, hp_row, pltpu.roll(h, 1, axis=0))
            h_next = jnp.where(row == tm - 1, hn_row, pltpu.roll(h, tm - 1, axis=0))
            h16 = h.astype(BF16)
            dp = (h_prev - h).astype(BF16)
            dn = (h_next - h).astype(BF16)
            mixes = {}
            for t in range(6):
                mixes[t] = h16 + dp * mup_ref[t:t + 1, sl].astype(BF16) + dn * mun_ref[t:t + 1, sl].astype(BF16)
                if t in mix_slot:
                    xs_ref[mix_slot[t], :, sl] = mixes[t]
            for c in range(2):
                hw_acc[c] = hw_acc[c] + jnp.dot(mixes[1], w1_ref[c, sl, :], preferred_element_type=F32)
                ha_acc[c] = ha_acc[c] + jnp.dot(mixes[4], a1_ref[c, sl, :], preferred_element_type=F32)
            if has_vres:
                hv_acc = hv_acc + jnp.dot(mixes[3], v1_ref[sl, :], preferred_element_type=F32)
        for c in range(2):
            hw_ref[c] = jnp.tanh(hw_acc[c]).astype(BF16)
            ha_ref[c] = ha_acc[c].astype(BF16)
        if has_vres:
            hv_ref[...] = hv_acc.astype(BF16)

    r = jnp.dot(xs_ref[0], wr_ref[...], preferred_element_type=F32)
    k = jnp.dot(xs_ref[1], wk_ref[...], preferred_element_type=F32)
    v = jnp.dot(xs_ref[2], wv_ref[...], preferred_element_type=F32)
    z = jnp.dot(xs_ref[3], wz_ref[...], preferred_element_type=F32)
    if has_vres:
        gate = _sigmoid(v0_ref[...] + jnp.dot(hv_ref[...], v2_ref[...], preferred_element_type=F32))
        v = v + (vf_ref[...] - v) * gate
    gmat = _group_matrix(tn)
    kk = k * kk_ref[...]
    kk = kk * lax.rsqrt(jnp.maximum(_group_sum(kk * kk, gmat), 1e-24))
    k_a = ka_ref[...]
    rk_acc = jnp.zeros((tm, tn), F32)
    nchunk = tm // CHUNK
    half = 256
    for c in range(2):
        wl = w0_ref[c:c + 1, :] + jnp.dot(hw_ref[c], w2_ref[c], preferred_element_type=F32)
        lw = -math.exp(-0.5) * _sigmoid(wl)
        a = _sigmoid(a0_ref[c:c + 1, :] + jnp.dot(ha_ref[c], a2_ref[c], preferred_element_type=F32))
        kd = k * (1.0 + (a - 1.0) * k_a)
        rk_acc = rk_acc + r * kd * rk_ref[c:c + 1, :]
        tri = _chunk_tri(half, reverse=(c == 1))
        g = jnp.concatenate([_split_dot(tri, lw[hh * half:(hh + 1) * half]) for hh in range(tm // half)], axis=0)
        g3 = g.reshape(nchunk, CHUNK, tn)
        last = 0 if c == 1 else CHUNK - 1
        g_tot = g3[:, last:last + 1, :]
        e_tot = jnp.exp(g_tot)
        est_out[c] = e_tot.reshape(nchunk, tn)
        e_q = jnp.exp(g)
        e_qp = jnp.exp(g - lw)
        e_k = jnp.exp(-g)
        e_end = (e_k.reshape(nchunk, CHUNK, tn) * e_tot).reshape(tm, tn)
        kka = kk * a
        kkq_out[c] = (kk * e_qp).astype(BF16)
        rq_out[c] = (r * e_q).astype(BF16)
        ad_out[c] = (kka * e_k).astype(BF16)
        kdk_out[c] = (kd * e_k).astype(BF16)
        kend_out[c] = (kd * e_end).astype(BF16)
        aend_out[c] = (kka * e_end).astype(BF16)
    v_out[...] = v
    bonus_out[...] = _group_sum(rk_acc, gmat) * v
    sz_out[...] = (z * _sigmoid(z)).astype(BF16)


def _pad_lora(w1, w2):
    rank = w1.shape[-1]
    pad1 = [(0, 0)] * (w1.ndim - 1) + [(0, LORA_PAD - rank)]
    pad2 = [(0, 0)] * (w2.ndim - 2) + [(0, LORA_PAD - rank), (0, 0)]
    return jnp.pad(w1, pad1).astype(BF16), jnp.pad(w2, pad2).astype(BF16)


def _rwkv_in(x2, seq, g, mu_prev, mu_next, w_in_bf, w0, w1, w2, a0, a1, a2, k_k, k_a, r_k, vres, v_first):
    rows = x2.shape[0]
    tm, tn = 512, 256
    tiles_per_seq = seq // tm
    ncol = D_MODEL // tn
    has_vres = vres is not None
    w1p, w2p = _pad_lora(w1, w2)
    a1p, a2p = _pad_lora(a1, a2)
    sub = tm // 8
    nsub = rows // 8

    def const2(shape):
        return pl.BlockSpec(shape, lambda i, j: (0, 0))

    def col2(nrow):
        return pl.BlockSpec((nrow, tn), lambda i, j: (0, j))

    def wcol(gi):
        return pl.BlockSpec((D_MODEL, tn), lambda i, j: (0, gi * ncol + j))

    in_specs = [
        pl.BlockSpec((tm, D_MODEL), lambda i, j: (i, 0)),
        pl.BlockSpec((8, D_MODEL), lambda i, j: (jnp.maximum(i * sub - 1, 0), 0)),
        pl.BlockSpec((8, D_MODEL), lambda i, j: (jnp.minimum((i + 1) * sub, nsub - 1), 0)),
        const2((1, D_MODEL)), const2((6, D_MODEL)), const2((6, D_MODEL)),
        wcol(0), wcol(1), wcol(2), wcol(3),
        pl.BlockSpec((2, D_MODEL, LORA_PAD), lambda i, j: (0, 0, 0)),
        pl.BlockSpec((2, LORA_PAD, tn), lambda i, j: (0, 0, j)),
        col2(2),
        pl.BlockSpec((2, D_MODEL, LORA_PAD), lambda i, j: (0, 0, 0)),
        pl.BlockSpec((2, LORA_PAD, tn), lambda i, j: (0, 0, j)),
        col2(2),
        col2(1), col2(1), col2(2),
    ]
    args = [x2, x2, x2, g, mu_prev, mu_next, w_in_bf, w_in_bf, w_in_bf, w_in_bf,
            w1p, w2p, w0, a1p, a2p, a0, k_k, k_a, r_k]
    if has_vres:
        v0, v1, v2 = vres
        v1p, v2p = _pad_lora(v1, v2)
        in_specs += [const2((D_MODEL, LORA_PAD)), col2(LORA_PAD), col2(1),
                     pl.BlockSpec((tm, tn), lambda i, j: (i, j))]
        args += [v1p, v2p, v0, v_first]

    tile = pl.BlockSpec((tm, tn), lambda i, j: (i, j))
    tile2 = pl.BlockSpec((2, tm, tn), lambda i, j: (0, i, j))
    est_spec = pl.BlockSpec((2, tm // CHUNK, tn), lambda i, j: (0, i, j))
    sds_bf2 = jax.ShapeDtypeStruct((2, rows, D_MODEL), BF16)
    sds = jax.ShapeDtypeStruct((rows, D_MODEL), F32)
    scratch = [pltpu.VMEM((4, tm, D_MODEL), BF16), pltpu.VMEM((2, tm, LORA_PAD), BF16),
               pltpu.VMEM((2, tm, LORA_PAD), BF16)]
    if has_vres:
        scratch.append(pltpu.VMEM((tm, LORA_PAD), BF16))
    kern = functools.partial(_rwkv_in_kernel, tm=tm, tn=tn, tiles_per_seq=tiles_per_seq, has_vres=has_vres)
    return pl.pallas_call(
        kern,
        grid=(rows // tm, ncol),
        in_specs=in_specs,
        out_specs=[tile2] * 6 + [est_spec, tile, tile, tile],
        out_shape=[sds_bf2] * 6 + [jax.ShapeDtypeStruct((2, rows // CHUNK, D_MODEL), F32), sds, sds,
                                   jax.ShapeDtypeStruct((rows, D_MODEL), BF16)],
        scratch_shapes=scratch,
        compiler_params=_cparams("parallel", "arbitrary"),
        name="rwkv_in_vres" if has_vres else "rwkv_in",
    )(*args)


def _rwkv_scan_kernel(kkq_ref, rq_ref, ad_ref, kdk_ref, kend_ref, aend_ref, v_ref, est_ref, y_ref, state_ref,
                      *, tb, npair):
    rev = pl.program_id(1) == 1
    step = pl.program_id(3)
    nchunk = tb // CHUNK
    pairs = range(npair)

    @pl.when(step == 0)
    def _():
        state_ref[...] = jnp.zeros_like(state_ref)

    t = lax.broadcasted_iota(jnp.int32, (CHUNK, LANES), 0)
    lane = lax.broadcasted_iota(jnp.int32, (CHUNK, LANES), 1)
    s = lane & (CHUNK - 1)
    d = jnp.where(rev, s - t, t - s)
    strict = d > 0
    incl = d >= 0
    eye = jnp.where(d == 0, 1.0, 0.0).astype(F32)
    first_half = lane < CHUNK
    half_a = jnp.where(first_half, 1.0, 0.0).astype(BF16)
    half_b = jnp.where(first_half, 0.0, 1.0).astype(BF16)
    rr = lax.broadcasted_iota(jnp.int32, (LANES, LANES), 0)
    cc = lax.broadcasted_iota(jnp.int32, (LANES, LANES), 1)
    bd_mask = (rr < CHUNK) == (cc < CHUNK)
    est_rows = lax.broadcasted_iota(jnp.int32, (nchunk, npair * LANES), 0)

    def bd16(y):
        return jnp.concatenate([y * half_a, y * half_b], axis=0)

    def bd32(y):
        return jnp.concatenate([jnp.where(first_half, y, 0.0), jnp.where(first_half, 0.0, y)],
                               axis=0).astype(BF16)

    def mm(a, b):
        return jnp.dot(a, b, preferred_element_type=F32)

    def body(ci, carry):
        first = ci * CHUNKS_PER_TRIP
        cs = [jnp.where(rev, nchunk - 1 - (first + u), first + u) for u in range(CHUNKS_PER_TRIP)]
        jobs = [(pl.ds(pl.multiple_of(c * CHUNK, CHUNK), CHUNK), slice(p * LANES, (p + 1) * LANES))
                for c in cs for p in pairs]
        n = range(len(jobs))
        kkq = [kkq_ref[rw, ln] for rw, ln in jobs]
        rq = [rq_ref[rw, ln] for rw, ln in jobs]
        v = [v_ref[rw, ln] for rw, ln in jobs]
        gram = [lax.dot_general(
            jnp.concatenate([kkq[i], rq[i]], axis=0),
            jnp.concatenate([bd16(ad_ref[jobs[i]]), bd16(kdk_ref[jobs[i]])], axis=0),
            NT_DIMS, preferred_element_type=F32) for i in n]
        a_mat = [jnp.where(strict, gram[i][0:CHUNK, 0:LANES], 0.0) for i in n]
        ak = [jnp.where(strict, gram[i][0:CHUNK, LANES:], 0.0) for i in n]
        bra = [jnp.where(incl, gram[i][CHUNK:, 0:LANES], 0.0).astype(BF16) for i in n]
        brk = [jnp.where(incl, gram[i][CHUNK:, LANES:], 0.0) for i in n]
        early = [mm(jnp.concatenate([ak[i], brk[i]], axis=0).astype(BF16), bd32(v[i])) for i in n]
        a_pow = [mm(a_mat[i].astype(BF16), bd32(a_mat[i])) for i in n]
        inv = [eye - a_mat[i] for i in n]
        for _ in range(4):
            st = [mm(jnp.concatenate([inv[i], a_pow[i]], axis=0).astype(BF16), bd32(a_pow[i])) for i in n]
            inv = [inv[i] + st[i][0:CHUNK] for i in n]
            a_pow = [st[i][CHUNK:] for i in n]
        inv = [inv[i] + mm(inv[i].astype(BF16), bd32(a_pow[i])) for i in n]
        wu = [mm(inv[i].astype(BF16), jnp.concatenate([bd16(kkq[i]), bd32(early[i][0:CHUNK])], axis=1))
              for i in n]
        ry = [mm(bra[i], jnp.concatenate([bd32(wu[i][:, 0:LANES]), bd32(wu[i][:, LANES:])], axis=1))
              for i in n]
        aend = [aend_ref[jobs[i]] for i in n]
        xraw = [lax.dot_general(wu[i][:, 0:LANES].astype(BF16), aend[i], TN_DIMS, preferred_element_type=F32)
                for i in n]
        nraw = [lax.dot_general(jnp.concatenate([v[i], -wu[i][:, LANES:]], axis=0).astype(BF16),
                                jnp.concatenate([kend_ref[jobs[i]], aend[i]], axis=0),
                                TN_DIMS, preferred_element_type=F32) for i in n]
        rqp = [(rq[i].astype(F32) - ry[i][:, 0:LANES]).astype(BF16) for i in n]
        yv = [early[i][CHUNK:] - ry[i][:, LANES:] for i in n]
        xmat = [jnp.where(bd_mask, xraw[i], 0.0).astype(BF16) for i in n]
        nmat = [jnp.where(bd_mask, nraw[i], 0.0) for i in n]
        est_all = est_ref[...]
        state = [state_ref[p] for p in pairs]
        for u in range(CHUNKS_PER_TRIP):
            est_row = jnp.sum(jnp.where(est_rows == cs[u], est_all, 0.0), axis=0, keepdims=True)
            idx = [u * npair + p for p in pairs]
            s16 = [state[p].astype(BF16) for p in pairs]
            ys = [yv[idx[p]] + lax.dot_general(rqp[idx[p]], s16[p], NT_DIMS, preferred_element_type=F32)
                  for p in pairs]
            sx = [mm(s16[p], xmat[idx[p]]) for p in pairs]
            for p in pairs:
                rw, ln = jobs[idx[p]]
                y_ref[rw, ln] = ys[p]
                state[p] = state[p] * est_row[:, ln] - sx[p] + nmat[idx[p]]
        for p in pairs:
            state_ref[p] = state[p]
        return carry

    lax.fori_loop(0, nchunk // CHUNKS_PER_TRIP, body, 0)


def _rwkv_scan(kkq, rq, ad, kdk, kend, aend, est, v, batch, seq):
    tb, lw_lanes = 512, 1024
    nt = seq // tb
    npair = lw_lanes // LANES

    def tmap(i, dr):
        return i + dr * (nt - 1 - 2 * i)

    v_spec = pl.BlockSpec((tb, lw_lanes), lambda b, dr, h, i: (b * nt + tmap(i, dr), h))
    d_spec = pl.BlockSpec((None, tb, lw_lanes), lambda b, dr, h, i: (dr, b * nt + tmap(i, dr), h))
    e_spec = pl.BlockSpec((None, tb // CHUNK, lw_lanes), lambda b, dr, h, i: (dr, b * nt + tmap(i, dr), h))
    kern = functools.partial(_rwkv_scan_kernel, tb=tb, npair=npair)
    return pl.pallas_call(
        kern,
        grid=(batch, 2, D_MODEL // lw_lanes, nt),
        in_specs=[d_spec] * 6 + [v_spec, e_spec],
        out_specs=d_spec,
        out_shape=jax.ShapeDtypeStruct((2, batch * seq, D_MODEL), F32),
        scratch_shapes=[pltpu.VMEM((npair, LANES, LANES), F32)],
        compiler_params=_cparams("parallel", "parallel", "parallel", "arbitrary"),
        name="rwkv_scan",
    )(kkq, rq, ad, kdk, kend, aend, v, est)


def _rwkv_out_kernel(y_ref, bonus_ref, sz_ref, gg_ref, gb_ref, x_ref, w_ref, g_ref, o_ref, yb_ref):
    gmat = _group_matrix(LANES)
    inv_n = 1.0 / RWKV_HEAD_DIM
    for cb in range(D_MODEL // LANES):
        sl = slice(cb * LANES, (cb + 1) * LANES)
        y = y_ref[0, :, sl] + y_ref[1, :, sl]
        mu = _group_sum(y, gmat) * inv_n
        yc = y - mu
        var = _group_sum(yc * yc, gmat) * inv_n
        yn = yc * lax.rsqrt(var + GN_EPS) * gg_ref[:, sl] + gb_ref[:, sl]
        yb_ref[:, sl] = ((yn + bonus_ref[:, sl]) * sz_ref[:, sl].astype(F32)).astype(BF16)
    _proj_norm_residual(yb_ref[...], w_ref, g_ref, x_ref, o_ref)


def _rwkv_out(y2, bonus, sz, gn_g, gn_b, x2, w_bf, g):
    rows = x2.shape[0]
    tm = 256
    row_spec = pl.BlockSpec((tm, D_MODEL), lambda i: (i, 0))
    vec = pl.BlockSpec((1, D_MODEL), lambda i: (0, 0))
    return pl.pallas_call(
        _rwkv_out_kernel,
        grid=(rows // tm,),
        in_specs=[pl.BlockSpec((2, tm, D_MODEL), lambda i: (0, i, 0)), row_spec, row_spec, vec, vec, row_spec,
                  pl.BlockSpec((D_MODEL, D_MODEL), lambda i: (0, 0)), vec],
        out_specs=row_spec,
        out_shape=jax.ShapeDtypeStruct((rows, D_MODEL), F32),
        scratch_shapes=[pltpu.VMEM((tm, D_MODEL), BF16)],
        compiler_params=_cparams("parallel"),
        name="rwkv_out",
    )(y2, bonus, sz, gn_g, gn_b, x2, w_bf, g)


def _trunk(x, p, rope):
    batch, seq, _ = x.shape
    x2 = x.reshape(batch * seq, D_MODEL)
    cos_t, sin_t = rope
    v_first = None
    depth = p["norm_pre"].shape[0]
    for layer in range(depth):
        j = layer // 2
        g_pre = p["norm_pre"][layer][None, :]
        g_post = p["norm_post"][layer][None, :]
        if layer % 2 == 0:
            w_in = p["att_w_in"][j]
            outs, lses = [], []
            for gidx, (_, dil) in enumerate(ATT_GROUPS):
                q, k, v = _att_in(x2, g_pre, w_in, cos_t, sin_t, seq, gidx, dil)
                o, lse = _attn_group(q, k, v, dil, batch, seq)
                outs.append(o)
                lses.append(lse)
            z = _gate_in(x2, g_pre, w_in)
            x2 = _att_out(outs, lses, z, x2, p["att_w_out"][j], g_post)
        else:
            vres = None if j == 0 else (p["rwkv_v0"][j - 1][None, :], p["rwkv_v1"][j - 1], p["rwkv_v2"][j - 1])
            kkq, rq, ad, kdk, kend, aend, est, v, bonus, sz = _rwkv_in(
                x2, seq, g_pre, p["rwkv_mu_prev"][j], p["rwkv_mu_next"][j], p["rwkv_w_in"][j],
                p["rwkv_w0"][j], p["rwkv_w1"][j], p["rwkv_w2"][j],
                p["rwkv_a0"][j], p["rwkv_a1"][j], p["rwkv_a2"][j],
                p["rwkv_k_k"][j][None, :], p["rwkv_k_a"][j][None, :], p["rwkv_r_k"][j].reshape(2, D_MODEL),
                vres, v_first)
            if j == 0:
                v_first = v
            y2 = _rwkv_scan(kkq, rq, ad, kdk, kend, aend, est, v, batch, seq)
            x2 = _rwkv_out(y2, bonus, sz, p["rwkv_gn_g"][j][None, :], p["rwkv_gn_b"][j][None, :], x2,
                           p["rwkv_w_out"][j], g_post)
    return x2.reshape(batch, seq, D_MODEL)


def kernel(x_prompt, x_sample, norm_pre, norm_post, att_w_in, att_w_out, rwkv_mu_prev, rwkv_mu_next, rwkv_w_in, rwkv_w0, rwkv_w1, rwkv_w2, rwkv_a0, rwkv_a1, rwkv_a2, rwkv_v0, rwkv_v1, rwkv_v2, rwkv_k_k, rwkv_k_a, rwkv_r_k, rwkv_gn_g, rwkv_gn_b, rwkv_w_out):
    p = dict(
        norm_pre=norm_pre, norm_post=norm_post,
        att_w_in=att_w_in.astype(BF16), att_w_out=att_w_out.astype(BF16),
        rwkv_mu_prev=rwkv_mu_prev, rwkv_mu_next=rwkv_mu_next, rwkv_w_in=rwkv_w_in.astype(BF16),
        rwkv_w0=rwkv_w0, rwkv_w1=rwkv_w1, rwkv_w2=rwkv_w2,
        rwkv_a0=rwkv_a0, rwkv_a1=rwkv_a1, rwkv_a2=rwkv_a2,
        rwkv_v0=rwkv_v0, rwkv_v1=rwkv_v1, rwkv_v2=rwkv_v2,
        rwkv_k_k=rwkv_k_k, rwkv_k_a=rwkv_k_a, rwkv_r_k=rwkv_r_k,
        rwkv_gn_g=rwkv_gn_g, rwkv_gn_b=rwkv_gn_b, rwkv_w_out=rwkv_w_out.astype(BF16),
    )
    rope = _rope_tables(max(x_prompt.shape[1], x_sample.shape[1]))
    return (_trunk(x_prompt, p, rope), _trunk(x_sample, p, rope))
```

```python
import functools
import math

import jax
import jax.numpy as jnp
from jax import lax
from jax.experimental import pallas as pl
from jax.experimental.pallas import tpu as pltpu

F32 = jnp.float32
BF16 = jnp.bfloat16

D_MODEL = 2048
LANES = 128
MXU_COLS = 256
ATT_HEAD_DIM = 128
ATT_GROUPS = ((128, 1), (512, 4), (2048, 16))
ATT_HALF = 64
ROPE_THETA = 10000.0
RWKV_HEAD_DIM = 64
LORA_PAD = 128
RMS_EPS = 1e-6
GN_EPS = 64e-5
NEG_INF = -1e30
CHUNK = 64
ATT_IN_ROWS = 512
CHUNKS_PER_TRIP = 2
VMEM_LIMIT_BYTES = 56 * 1024 * 1024

NT_DIMS = (((1,), (1,)), ((), ()))
TN_DIMS = (((0,), (0,)), ((), ()))


def _cparams(*sem):
    return pltpu.CompilerParams(dimension_semantics=sem, vmem_limit_bytes=VMEM_LIMIT_BYTES)


def _rms_scale(x):
    return lax.rsqrt(jnp.mean(x * x, axis=-1, keepdims=True) + RMS_EPS)


def _sigmoid(x):
    return 0.5 * jnp.tanh(0.5 * x) + 0.5


def _split_dot(lhs_bf, x):
    hi = x.astype(BF16)
    lo = (x - hi.astype(F32)).astype(BF16)
    return (jnp.dot(lhs_bf, hi, preferred_element_type=F32)
            + jnp.dot(lhs_bf, lo, preferred_element_type=F32))


def _group_sum(x, gmat):
    hi = x.astype(BF16)
    lo = (x - hi.astype(F32)).astype(BF16)
    return (jnp.dot(hi, gmat, preferred_element_type=F32)
            + jnp.dot(lo, gmat, preferred_element_type=F32))


def _group_matrix(n):
    r = lax.broadcasted_iota(jnp.int32, (n, n), 0) // RWKV_HEAD_DIM
    c = lax.broadcasted_iota(jnp.int32, (n, n), 1) // RWKV_HEAD_DIM
    return jnp.where(r == c, 1.0, 0.0).astype(BF16)


def _rope_table_kernel(invf_ref, cos_ref, sin_ref, *, dil):
    rows = cos_ref.shape[0]
    base = pl.program_id(0) * rows
    local = lax.broadcasted_iota(jnp.int32, (rows, LANES), 0)
    per_res = rows // dil
    pos = (base + (local % per_res) * dil + local // per_res).astype(F32)
    ang = pos * invf_ref[...]
    lane = lax.broadcasted_iota(jnp.int32, (rows, LANES), 1)
    s = jnp.sin(ang)
    cos_ref[...] = jnp.cos(ang)
    sin_ref[...] = jnp.where(lane < ATT_HEAD_DIM // 2, -s, s)


def _rope_tables(seq, dil):
    half = ATT_HEAD_DIM // 2
    inv_freq = 1.0 / (ROPE_THETA ** (jnp.arange(half, dtype=F32) * 2.0 / ATT_HEAD_DIM))
    invf = jnp.concatenate([inv_freq, inv_freq])[None, :]
    rows = ATT_IN_ROWS
    return pl.pallas_call(
        functools.partial(_rope_table_kernel, dil=dil),
        grid=(seq // rows,),
        in_specs=[pl.BlockSpec((1, LANES), lambda i: (0, 0))],
        out_specs=[pl.BlockSpec((rows, LANES), lambda i: (i, 0))] * 2,
        out_shape=[jax.ShapeDtypeStruct((seq, LANES), F32)] * 2,
        compiler_params=_cparams("arbitrary"),
        name=f"rope_table_d{dil}",
    )(invf)


def _att_in_kernel(x_ref, g_ref, wq_ref, wk_ref, wv_ref, cos_ref, sin_ref, q_ref, k_ref, v_ref, h_ref, *hs_ref,
                   tm, tn, dil, scale):
    per_res = tm // dil

    @pl.when(pl.program_id(1) == 0)
    def _():
        x = x_ref[...]
        h = x * _rms_scale(x) * g_ref[...]
        if dil == 1:
            h_ref[...] = h.astype(BF16)
        else:
            for cb in range(D_MODEL // LANES):
                sl = slice(cb * LANES, (cb + 1) * LANES)
                hs_ref[0][cb] = h[:, sl]
                for r in range(dil):
                    h_ref[r * per_res:(r + 1) * per_res, sl] = (
                        hs_ref[0][cb, pl.ds(r, per_res, stride=dil), :].astype(BF16))

    h = h_ref[...]
    cos = cos_ref[...]
    sin = sin_ref[...]
    for w_ref, o_ref, rope, sc in ((wq_ref, q_ref, True, scale), (wk_ref, k_ref, True, None),
                                   (wv_ref, v_ref, False, None)):
        for cb in range(tn // MXU_COLS):
            acc = jnp.dot(h, w_ref[:, cb * MXU_COLS:(cb + 1) * MXU_COLS], preferred_element_type=F32)
            for hh in range(MXU_COLS // LANES):
                sl = slice(cb * MXU_COLS + hh * LANES, cb * MXU_COLS + (hh + 1) * LANES)
                t = acc[:, hh * LANES:(hh + 1) * LANES]
                if rope:
                    t = t * cos + pltpu.roll(t, ATT_HEAD_DIM // 2, axis=1) * sin
                if sc is not None:
                    t = t * sc
                t = t.astype(BF16)
                for r in range(dil):
                    o_ref[r, :, sl] = t[r * per_res:(r + 1) * per_res]


def _att_in(x2, g, w_bf, layer, cos_t, sin_t, seq, gidx, dil):
    rows = x2.shape[0]
    tm, tn = ATT_IN_ROWS, 512
    tiles_per_seq = seq // tm
    ncol = D_MODEL // tn
    kern = functools.partial(_att_in_kernel, tm=tm, tn=tn, dil=dil, scale=ATT_HEAD_DIM ** -0.5)

    def wcol(part):
        return pl.BlockSpec((None, D_MODEL, tn), lambda i, j: (layer, 0, (gidx * 3 + part) * ncol + j))

    out_spec = pl.BlockSpec((dil, tm // dil, tn), lambda i, j: (0, i, j))
    out_sds = jax.ShapeDtypeStruct((dil, rows // dil, D_MODEL), BF16)
    scratch = [pltpu.VMEM((tm, D_MODEL), BF16)]
    if dil > 1:
        scratch.append(pltpu.VMEM((D_MODEL // LANES, tm, LANES), F32))
    return pl.pallas_call(
        kern,
        grid=(rows // tm, ncol),
        in_specs=[
            pl.BlockSpec((tm, D_MODEL), lambda i, j: (i, 0)),
            pl.BlockSpec((1, D_MODEL), lambda i, j: (0, 0)),
            wcol(0), wcol(1), wcol(2),
            pl.BlockSpec((tm, LANES), lambda i, j: (i % tiles_per_seq, 0)),
            pl.BlockSpec((tm, LANES), lambda i, j: (i % tiles_per_seq, 0)),
        ],
        out_specs=[out_spec] * 3,
        out_shape=[out_sds] * 3,
        scratch_shapes=scratch,
        compiler_params=_cparams("parallel", "arbitrary"),
        name=f"att_in_d{dil}",
    )(x2, g, w_bf, w_bf, w_bf, cos_t, sin_t)


def _gate_in_kernel(x_ref, g_ref, w_ref, o_ref):
    x = x_ref[...]
    h = (x * _rms_scale(x) * g_ref[...]).astype(BF16)
    o_ref[...] = jnp.dot(h, w_ref[...], preferred_element_type=F32).astype(BF16)


def _gate_in(x2, g, w_bf, layer):
    rows = x2.shape[0]
    tm = 512
    col0 = 3 * len(ATT_GROUPS)
    return pl.pallas_call(
        _gate_in_kernel,
        grid=(rows // tm,),
        in_specs=[
            pl.BlockSpec((tm, D_MODEL), lambda i: (i, 0)),
            pl.BlockSpec((1, D_MODEL), lambda i: (0, 0)),
            pl.BlockSpec((None, D_MODEL, D_MODEL), lambda i: (layer, 0, col0)),
        ],
        out_specs=pl.BlockSpec((tm, D_MODEL), lambda i: (i, 0)),
        out_shape=jax.ShapeDtypeStruct((rows, D_MODEL), BF16),
        compiler_params=_cparams("parallel"),
        name="att_gate_in",
    )(x2, g, w_bf)


def _attn_kernel(q_ref, kp_ref, kc_ref, kn_ref, vp_ref, vc_ref, vn_ref, o_ref, lse_ref, *, bq, sb, sub_len, nh):
    i = pl.program_id(3)
    nk = sb + 2 * ATT_HALF
    nsub = bq // sb
    ii = lax.broadcasted_iota(jnp.int32, (sb, nk), 0)
    jj = lax.broadcasted_iota(jnp.int32, (sb, nk), 1)
    rel = jj - ii
    in_band = (rel >= 0) & (rel <= 2 * ATT_HALF)
    valid = []
    for u in range(nsub):
        kpos = i * bq + u * sb - ATT_HALF + jj
        valid.append(in_band & (kpos >= 0) & (kpos < sub_len))
    lane = lax.broadcasted_iota(jnp.int32, (sb, LANES), 1)
    units = [(hh, u) for hh in range(nh) for u in range(nsub)]

    def scores(hh, u):
        sl = slice(hh * LANES, (hh + 1) * LANES)
        kcat = jnp.concatenate([kp_ref[:, sl], kc_ref[:, sl], kn_ref[:, sl]], axis=0)
        s = lax.dot_general(q_ref[u * sb:(u + 1) * sb, sl], kcat[u * sb:u * sb + nk], NT_DIMS,
                            preferred_element_type=F32)
        return jnp.where(valid[u], s, NEG_INF)

    def finish(hh, u, s, lse_acc):
        sl = slice(hh * LANES, (hh + 1) * LANES)
        vcat = jnp.concatenate([vp_ref[:, sl], vc_ref[:, sl], vn_ref[:, sl]], axis=0)
        m = jnp.max(s, axis=-1, keepdims=True)
        p = jnp.exp(s - m)
        l = jnp.sum(p, axis=-1, keepdims=True)
        o = jnp.dot(p.astype(BF16), vcat[u * sb:u * sb + nk], preferred_element_type=F32)
        o_ref[u * sb:(u + 1) * sb, sl] = (o / l).astype(BF16)
        lse_acc[u] = jnp.where(lane == hh, m + jnp.log(l), lse_acc[u])

    lse_acc = [jnp.zeros((sb, LANES), F32) for _ in range(nsub)]
    s_prev = scores(*units[0])
    for k in range(1, len(units)):
        s_next = scores(*units[k])
        finish(*units[k - 1], s_prev, lse_acc)
        s_prev = s_next
    finish(*units[-1], s_prev, lse_acc)
    for u in range(nsub):
        lse_ref[u * sb:(u + 1) * sb, :] = lse_acc[u]


def _attn_group(q, k, v, dil, batch, seq):
    sub_len = seq // dil
    bq = min(256, sub_len)
    sb = min(128, bq)
    hw = 1024
    nh = hw // LANES
    hblocks = D_MODEL // hw
    nqb = sub_len // bq
    halo_per_q = bq // ATT_HALF
    n_halo = sub_len // ATT_HALF

    cur = pl.BlockSpec((None, bq, hw), lambda b, r, h, i: (r, b * nqb + i, h))
    prev = pl.BlockSpec((None, ATT_HALF, hw),
                        lambda b, r, h, i: (r, b * n_halo + jnp.maximum(i * halo_per_q - 1, 0), h))
    nxt = pl.BlockSpec((None, ATT_HALF, hw),
                       lambda b, r, h, i: (r, b * n_halo + jnp.minimum((i + 1) * halo_per_q, n_halo - 1), h))
    lse_spec = pl.BlockSpec((None, bq, LANES), lambda b, r, h, i: (r, b * nqb + i, h))
    rows = batch * sub_len
    kern = functools.partial(_attn_kernel, bq=bq, sb=sb, sub_len=sub_len, nh=nh)
    return pl.pallas_call(
        kern,
        grid=(batch, dil, hblocks, nqb),
        in_specs=[cur, prev, cur, nxt, prev, cur, nxt],
        out_specs=[cur, lse_spec],
        out_shape=[jax.ShapeDtypeStruct((dil, rows, D_MODEL), BF16),
                   jax.ShapeDtypeStruct((dil, rows, hblocks * LANES), F32)],
        compiler_params=_cparams("parallel", "parallel", "parallel", "arbitrary"),
        name=f"attn_d{dil}",
    )(q, k, k, k, v, v, v)


def _proj_norm_residual(y_bf, rows, w_ref, g_ref, x_ref, o_ref):
    out = jnp.dot(y_bf, w_ref[...], preferred_element_type=F32)
    o_ref[rows, :] = x_ref[rows, :] + out * _rms_scale(out) * g_ref[...]


def _att_out_kernel(o0_ref, o1_ref, o2_ref, l0_ref, l1_ref, l2_ref, z_ref, x_ref, w_ref, g_ref, o_ref,
                    so1, so2, sl1, sl2, y_ref, *, tm, dils, heads_per_block):
    nlb = l0_ref.shape[-1] // LANES
    for src, dst, d in ((l1_ref, sl1, dils[1]), (l2_ref, sl2, dils[2])):
        for lb in range(nlb):
            for r in range(d):
                dst[lb, pl.ds(r, tm // d, stride=d), :] = src[r, :, lb * LANES:(lb + 1) * LANES]
    wts = []
    for lb in range(nlb):
        l0, l1, l2 = l0_ref[0, :, lb * LANES:(lb + 1) * LANES], sl1[lb], sl2[lb]
        m = jnp.maximum(jnp.maximum(l0, l1), l2)
        e0, e1, e2 = jnp.exp(l0 - m), jnp.exp(l1 - m), jnp.exp(l2 - m)
        inv = 1.0 / (e0 + e1 + e2)
        wts.append((e0 * inv, e1 * inv, e2 * inv))
    for cb in range(D_MODEL // LANES):
        sl = slice(cb * LANES, (cb + 1) * LANES)
        for src, dst, d in ((o1_ref, so1, dils[1]), (o2_ref, so2, dils[2])):
            for r in range(d):
                dst[cb, pl.ds(r, tm // d, stride=d), :] = src[r, :, sl].astype(F32)
    half = tm // 2
    for hf in range(2):
        rows = slice(hf * half, (hf + 1) * half)
        for cb in range(D_MODEL // LANES):
            sl = slice(cb * LANES, (cb + 1) * LANES)
            w0, w1, w2 = wts[cb // heads_per_block]
            hl = cb % heads_per_block
            bc = lambda w: jnp.broadcast_to(w[rows, hl:hl + 1], (half, LANES))
            o = (bc(w0) * o0_ref[0, rows, sl].astype(F32) + bc(w1) * so1[cb, rows, :]
                 + bc(w2) * so2[cb, rows, :])
            z = z_ref[rows, sl].astype(F32)
            y_ref[rows, sl] = (o * (z * _sigmoid(z))).astype(BF16)
        _proj_norm_residual(y_ref[rows, :], rows, w_ref, g_ref, x_ref, o_ref)


def _att_out(outs, lses, z, x2, w_bf, layer, g):
    rows = x2.shape[0]
    tm = 256
    dils = tuple(d for _, d in ATT_GROUPS)
    lse_w = lses[0].shape[-1]
    nlb = lse_w // LANES
    row_spec = pl.BlockSpec((tm, D_MODEL), lambda i: (i, 0))

    def res_spec(d, width):
        return pl.BlockSpec((d, tm // d, width), lambda i: (0, i, 0))

    kern = functools.partial(_att_out_kernel, tm=tm, dils=dils, heads_per_block=D_MODEL // LANES // nlb)
    return pl.pallas_call(
        kern,
        grid=(rows // tm,),
        in_specs=[res_spec(d, D_MODEL) for d in dils] + [res_spec(d, lse_w) for d in dils] + [
            row_spec, row_spec,
            pl.BlockSpec((None, D_MODEL, D_MODEL), lambda i: (layer, 0, 0)),
            pl.BlockSpec((1, D_MODEL), lambda i: (0, 0)),
        ],
        out_specs=row_spec,
        out_shape=jax.ShapeDtypeStruct((rows, D_MODEL), F32),
        scratch_shapes=[pltpu.VMEM((D_MODEL // LANES, tm, LANES), F32)] * 2
        + [pltpu.VMEM((nlb, tm, LANES), F32)] * 2 + [pltpu.VMEM((tm, D_MODEL), BF16)],
        compiler_params=_cparams("parallel"),
        name="att_out",
    )(*outs, *lses, z, x2, w_bf, g)


def _chunk_tri(n, reverse):
    t = lax.broadcasted_iota(jnp.int32, (n, n), 0)
    s = lax.broadcasted_iota(jnp.int32, (n, n), 1)
    same = (t // CHUNK) == (s // CHUNK)
    order = (s >= t) if reverse else (s <= t)
    return jnp.where(same & order, 1.0, 0.0).astype(BF16)


def _rwkv_in_kernel(*refs, tm, tn, tiles_per_seq, has_vres):
    (x_ref, xp_ref, xn_ref, g_ref, mup_ref, mun_ref,
     wr_ref, wk_ref, wv_ref, wz_ref,
     w1_ref, w2_ref, w0_ref, a1_ref, a2_ref, a0_ref,
     kk_ref, ka_ref, rk_ref) = refs[:19]
    pos = 19
    if has_vres:
        v1_ref, v2_ref, v0_ref, vf_ref = refs[pos:pos + 4]
        pos += 4
    (kkq_out, rq_out, ad_out, kdk_out, kend_out, aend_out, est_out, v_out, bonus_out, sz_out) = refs[pos:pos + 10]
    pos += 10
    xs_ref, hw_ref, ha_ref = refs[pos:pos + 3]
    hv_ref = refs[pos + 3] if has_vres else None

    i = pl.program_id(0)
    j = pl.program_id(1)
    mix_slot = {0: 0, 2: 1, 3: 2, 5: 3}

    @pl.when(j == 0)
    def _():
        t_in_seq = i % tiles_per_seq
        keep_prev = jnp.where(t_in_seq == 0, 0.0, 1.0).astype(F32)
        keep_next = jnp.where(t_in_seq == tiles_per_seq - 1, 0.0, 1.0).astype(F32)
        sx = _rms_scale(x_ref[...])
        xp = xp_ref[7:8, :]
        xn = xn_ref[0:1, :]
        sp = _rms_scale(xp) * keep_prev
        sn = _rms_scale(xn) * keep_next
        cw = 512
        row = lax.broadcasted_iota(jnp.int32, (tm, cw), 0)
        hw_acc = [jnp.zeros((tm, LORA_PAD), F32) for _ in range(2)]
        ha_acc = [jnp.zeros((tm, LORA_PAD), F32) for _ in range(2)]
        hv_acc = jnp.zeros((tm, LORA_PAD), F32)
        for cb in range(D_MODEL // cw):
            sl = slice(cb * cw, (cb + 1) * cw)
            g = g_ref[:, sl]
            h = x_ref[:, sl] * sx * g
            hp_row = xp[:, sl] * sp * g
            hn_row = xn[:, sl] * sn * g
            h_prev = jnp.where(row == 0, hp_row, pltpu.roll(h, 1, axis=0))
            h_next = jnp.where(row == tm - 1, hn_row, pltpu.roll(h, tm - 1, axis=0))
            h16 = h.astype(BF16)
            dp = (h_prev - h).astype(BF16)
            dn = (h_next - h).astype(BF16)
            mixes = {}
            for t in range(6):
                mixes[t] = h16 + dp * mup_ref[t:t + 1, sl].astype(BF16) + dn * mun_ref[t:t + 1, sl].astype(BF16)
                if t in mix_slot:
                    xs_ref[mix_slot[t], :, sl] = mixes[t]
            for c in range(2):
                hw_acc[c] = hw_acc[c] + jnp.dot(mixes[1], w1_ref[c, sl, :], preferred_element_type=F32)
                ha_acc[c] = ha_acc[c] + jnp.dot(mixes[4], a1_ref[c, sl, :], preferred_element_type=F32)
            if has_vres:
                hv_acc = hv_acc + jnp.dot(mixes[3], v1_ref[sl, :], preferred_element_type=F32)
        for c in range(2):
            hw_ref[c] = jnp.tanh(hw_acc[c]).astype(BF16)
            ha_ref[c] = ha_acc[c].astype(BF16)
        if has_vres:
            hv_ref[...] = hv_acc.astype(BF16)

    r = jnp.dot(xs_ref[0], wr_ref[...], preferred_element_type=F32)
    k = jnp.dot(xs_ref[1], wk_ref[...], preferred_element_type=F32)
    v = jnp.dot(xs_ref[2], wv_ref[...], preferred_element_type=F32)
    z = jnp.dot(xs_ref[3], wz_ref[...], preferred_element_type=F32)
    if has_vres:
        gate = _sigmoid(v0_ref[...] + jnp.dot(hv_ref[...], v2_ref[...], preferred_element_type=F32))
        v = v + (vf_ref[...] - v) * gate
    gmat = _group_matrix(tn)
    kk = k * kk_ref[...]
    kk = kk * lax.rsqrt(jnp.maximum(_group_sum(kk * kk, gmat), 1e-24))
    k_a = ka_ref[...]
    rk_acc = jnp.zeros((tm, tn), F32)
    nchunk = tm // CHUNK
    half = 256
    for c in range(2):
        wl = w0_ref[c:c + 1, :] + jnp.dot(hw_ref[c], w2_ref[c], preferred_element_type=F32)
        lw = -math.exp(-0.5) * _sigmoid(wl)
        a = _sigmoid(a0_ref[c:c + 1, :] + jnp.dot(ha_ref[c], a2_ref[c], preferred_element_type=F32))
        kd = k * (1.0 + (a - 1.0) * k_a)
        rk_acc = rk_acc + r * kd * rk_ref[c:c + 1, :]
        tri = _chunk_tri(half, reverse=(c == 1))
        g = jnp.concatenate([_split_dot(tri, lw[hh * half:(hh + 1) * half]) for hh in range(tm // half)], axis=0)
        g3 = g.reshape(nchunk, CHUNK, tn)
        last = 0 if c == 1 else CHUNK - 1
        g_tot = g3[:, last:last + 1, :]
        e_tot = jnp.exp(g_tot)
        est_out[c] = e_tot.reshape(nchunk, tn)
        e_q = jnp.exp(g)
        e_qp = jnp.exp(g - lw)
        e_k = jnp.exp(-g)
        e_end = (e_k.reshape(nchunk, CHUNK, tn) * e_tot).reshape(tm, tn)
        kka = kk * a
        kkq_out[c] = (kk * e_qp).astype(BF16)
        rq_out[c] = (r * e_q).astype(BF16)
        ad_out[c] = (kka * e_k).astype(BF16)
        kdk_out[c] = (kd * e_k).astype(BF16)
        kend_out[c] = (kd * e_end).astype(BF16)
        aend_out[c] = (kka * e_end).astype(BF16)
    v_out[...] = v
    bonus_out[...] = _group_sum(rk_acc, gmat) * v
    sz_out[...] = (z * _sigmoid(z)).astype(BF16)


def _pad_lora(w1, w2):
    rank = w1.shape[-1]
    pad1 = [(0, 0)] * (w1.ndim - 1) + [(0, LORA_PAD - rank)]
    pad2 = [(0, 0)] * (w2.ndim - 2) + [(0, LORA_PAD - rank), (0, 0)]
    return jnp.pad(w1, pad1).astype(BF16), jnp.pad(w2, pad2).astype(BF16)


def _rwkv_in(x2, seq, g, mu_prev, mu_next, w_in_bf, layer, w0, w1, w2, a0, a1, a2, k_k, k_a, r_k, vres, v_first):
    rows = x2.shape[0]
    tm, tn = 512, 256
    tiles_per_seq = seq // tm
    ncol = D_MODEL // tn
    has_vres = vres is not None
    w1p, w2p = _pad_lora(w1, w2)
    a1p, a2p = _pad_lora(a1, a2)
    sub = tm // 8
    nsub = rows // 8

    def const2(shape):
        return pl.BlockSpec(shape, lambda i, j: (0, 0))

    def col2(nrow):
        return pl.BlockSpec((nrow, tn), lambda i, j: (0, j))

    def wcol(gi):
        return pl.BlockSpec((None, D_MODEL, tn), lambda i, j: (layer, 0, gi * ncol + j))

    in_specs = [
        pl.BlockSpec((tm, D_MODEL), lambda i, j: (i, 0)),
        pl.BlockSpec((8, D_MODEL), lambda i, j: (jnp.maximum(i * sub - 1, 0), 0)),
        pl.BlockSpec((8, D_MODEL), lambda i, j: (jnp.minimum((i + 1) * sub, nsub - 1), 0)),
        const2((1, D_MODEL)), const2((6, D_MODEL)), const2((6, D_MODEL)),
        wcol(0), wcol(1), wcol(2), wcol(3),
        pl.BlockSpec((2, D_MODEL, LORA_PAD), lambda i, j: (0, 0, 0)),
        pl.BlockSpec((2, LORA_PAD, tn), lambda i, j: (0, 0, j)),
        col2(2),
        pl.BlockSpec((2, D_MODEL, LORA_PAD), lambda i, j: (0, 0, 0)),
        pl.BlockSpec((2, LORA_PAD, tn), lambda i, j: (0, 0, j)),
        col2(2),
        col2(1), col2(1), col2(2),
    ]
    args = [x2, x2, x2, g, mu_prev, mu_next, w_in_bf, w_in_bf, w_in_bf, w_in_bf,
            w1p, w2p, w0, a1p, a2p, a0, k_k, k_a, r_k]
    if has_vres:
        v0, v1, v2 = vres
        v1p, v2p = _pad_lora(v1, v2)
        in_specs += [const2((D_MODEL, LORA_PAD)), col2(LORA_PAD), col2(1),
                     pl.BlockSpec((tm, tn), lambda i, j: (i, j))]
        args += [v1p, v2p, v0, v_first]

    tile = pl.BlockSpec((tm, tn), lambda i, j: (i, j))
    tile2 = pl.BlockSpec((2, tm, tn), lambda i, j: (0, i, j))
    est_spec = pl.BlockSpec((2, tm // CHUNK, tn), lambda i, j: (0, i, j))
    sds_bf2 = jax.ShapeDtypeStruct((2, rows, D_MODEL), BF16)
    sds = jax.ShapeDtypeStruct((rows, D_MODEL), F32)
    scratch = [pltpu.VMEM((4, tm, D_MODEL), BF16), pltpu.VMEM((2, tm, LORA_PAD), BF16),
               pltpu.VMEM((2, tm, LORA_PAD), BF16)]
    if has_vres:
        scratch.append(pltpu.VMEM((tm, LORA_PAD), BF16))
    kern = functools.partial(_rwkv_in_kernel, tm=tm, tn=tn, tiles_per_seq=tiles_per_seq, has_vres=has_vres)
    return pl.pallas_call(
        kern,
        grid=(rows // tm, ncol),
        in_specs=in_specs,
        out_specs=[tile2] * 6 + [est_spec, tile, tile, tile],
        out_shape=[sds_bf2] * 6 + [jax.ShapeDtypeStruct((2, rows // CHUNK, D_MODEL), F32), sds, sds,
                                   jax.ShapeDtypeStruct((rows, D_MODEL), BF16)],
        scratch_shapes=scratch,
        compiler_params=_cparams("parallel", "arbitrary"),
        name="rwkv_in_vres" if has_vres else "rwkv_in",
    )(*args)


def _rwkv_scan_kernel(kkq_ref, rq_ref, ad_ref, kdk_ref, kend_ref, aend_ref, v_ref, est_ref, y_ref, state_ref,
                      *, tb, npair):
    rev = pl.program_id(1) == 1
    step = pl.program_id(3)
    nchunk = tb // CHUNK
    pairs = range(npair)

    @pl.when(step == 0)
    def _():
        state_ref[...] = jnp.zeros_like(state_ref)

    t = lax.broadcasted_iota(jnp.int32, (CHUNK, LANES), 0)
    lane = lax.broadcasted_iota(jnp.int32, (CHUNK, LANES), 1)
    s = lane & (CHUNK - 1)
    d = jnp.where(rev, s - t, t - s)
    strict = d > 0
    incl = d >= 0
    eye = jnp.where(d == 0, 1.0, 0.0).astype(F32)
    first_half = lane < CHUNK
    half_a = jnp.where(first_half, 1.0, 0.0).astype(BF16)
    half_b = jnp.where(first_half, 0.0, 1.0).astype(BF16)
    rr = lax.broadcasted_iota(jnp.int32, (LANES, LANES), 0)
    cc = lax.broadcasted_iota(jnp.int32, (LANES, LANES), 1)
    bd_mask = (rr < CHUNK) == (cc < CHUNK)
    est_rows = lax.broadcasted_iota(jnp.int32, (nchunk, npair * LANES), 0)

    def bd16(y):
        return jnp.concatenate([y * half_a, y * half_b], axis=0)

    def bd32(y):
        return jnp.concatenate([jnp.where(first_half, y, 0.0), jnp.where(first_half, 0.0, y)],
                               axis=0).astype(BF16)

    def mm(a, b):
        return jnp.dot(a, b, preferred_element_type=F32)

    def body(ci, carry):
        first = ci * CHUNKS_PER_TRIP
        cs = [jnp.where(rev, nchunk - 1 - (first + u), first + u) for u in range(CHUNKS_PER_TRIP)]
        jobs = [(pl.ds(pl.multiple_of(c * CHUNK, CHUNK), CHUNK), slice(p * LANES, (p + 1) * LANES))
                for c in cs for p in pairs]
        n = range(len(jobs))
        kkq = [kkq_ref[rw, ln] for rw, ln in jobs]
        rq = [rq_ref[rw, ln] for rw, ln in jobs]
        v = [v_ref[rw, ln] for rw, ln in jobs]
        gram = [lax.dot_general(
            jnp.concatenate([kkq[i], rq[i]], axis=0),
            jnp.concatenate([bd16(ad_ref[jobs[i]]), bd16(kdk_ref[jobs[i]])], axis=0),
            NT_DIMS, preferred_element_type=F32) for i in n]
        a_mat = [jnp.where(strict, gram[i][0:CHUNK, 0:LANES], 0.0) for i in n]
        ak = [jnp.where(strict, gram[i][0:CHUNK, LANES:], 0.0) for i in n]
        bra = [jnp.where(incl, gram[i][CHUNK:, 0:LANES], 0.0).astype(BF16) for i in n]
        brk = [jnp.where(incl, gram[i][CHUNK:, LANES:], 0.0) for i in n]
        early = [mm(jnp.concatenate([ak[i], brk[i]], axis=0).astype(BF16), bd32(v[i])) for i in n]
        a_pow = [mm(a_mat[i].astype(BF16), bd32(a_mat[i])) for i in n]
        inv = [eye - a_mat[i] for i in n]
        for _ in range(4):
            st = [mm(jnp.concatenate([inv[i], a_pow[i]], axis=0).astype(BF16), bd32(a_pow[i])) for i in n]
            inv = [inv[i] + st[i][0:CHUNK] for i in n]
            a_pow = [st[i][CHUNK:] for i in n]
        inv = [inv[i] + mm(inv[i].astype(BF16), bd32(a_pow[i])) for i in n]
        wu = [mm(inv[i].astype(BF16), jnp.concatenate([bd16(kkq[i]), bd32(early[i][0:CHUNK])], axis=1))
              for i in n]
        ry = [mm(bra[i], jnp.concatenate([bd32(wu[i][:, 0:LANES]), bd32(wu[i][:, LANES:])], axis=1))
              for i in n]
        aend = [aend_ref[jobs[i]] for i in n]
        xraw = [lax.dot_general(wu[i][:, 0:LANES].astype(BF16), aend[i], TN_DIMS, preferred_element_type=F32)
                for i in n]
        nraw = [lax.dot_general(jnp.concatenate([v[i], -wu[i][:, LANES:]], axis=0).astype(BF16),
                                jnp.concatenate([kend_ref[jobs[i]], aend[i]], axis=0),
                                TN_DIMS, preferred_element_type=F32) for i in n]
        rqp = [(rq[i].astype(F32) - ry[i][:, 0:LANES]).astype(BF16) for i in n]
        yv = [early[i][CHUNK:] - ry[i][:, LANES:] for i in n]
        xmat = [jnp.where(bd_mask, xraw[i], 0.0).astype(BF16) for i in n]
        nmat = [jnp.where(bd_mask, nraw[i], 0.0) for i in n]
        est_all = est_ref[...]
        state = [state_ref[p] for p in pairs]
        for u in range(CHUNKS_PER_TRIP):
            est_row = jnp.sum(jnp.where(est_rows == cs[u], est_all, 0.0), axis=0, keepdims=True)
            idx = [u * npair + p for p in pairs]
            s16 = [state[p].astype(BF16) for p in pairs]
            ys = [yv[idx[p]] + lax.dot_general(rqp[idx[p]], s16[p], NT_DIMS, preferred_element_type=F32)
                  for p in pairs]
            sx = [mm(s16[p], xmat[idx[p]]) for p in pairs]
            for p in pairs:
                rw, ln = jobs[idx[p]]
                y_ref[rw, ln] = ys[p]
                state[p] = state[p] * est_row[:, ln] - sx[p] + nmat[idx[p]]
        for p in pairs:
            state_ref[p] = state[p]
        return carry

    lax.fori_loop(0, nchunk // CHUNKS_PER_TRIP, body, 0)


def _rwkv_scan(kkq, rq, ad, kdk, kend, aend, est, v, batch, seq):
    tb, lw_lanes = 512, 1024
    nt = seq // tb
    npair = lw_lanes // LANES

    def tmap(i, dr):
        return i + dr * (nt - 1 - 2 * i)

    v_spec = pl.BlockSpec((tb, lw_lanes), lambda b, dr, h, i: (b * nt + tmap(i, dr), h))
    d_spec = pl.BlockSpec((None, tb, lw_lanes), lambda b, dr, h, i: (dr, b * nt + tmap(i, dr), h))
    e_spec = pl.BlockSpec((None, tb // CHUNK, lw_lanes), lambda b, dr, h, i: (dr, b * nt + tmap(i, dr), h))
    kern = functools.partial(_rwkv_scan_kernel, tb=tb, npair=npair)
    return pl.pallas_call(
        kern,
        grid=(batch, 2, D_MODEL // lw_lanes, nt),
        in_specs=[d_spec] * 6 + [v_spec, e_spec],
        out_specs=d_spec,
        out_shape=jax.ShapeDtypeStruct((2, batch * seq, D_MODEL), F32),
        scratch_shapes=[pltpu.VMEM((npair, LANES, LANES), F32)],
        compiler_params=_cparams("parallel", "parallel", "parallel", "arbitrary"),
        name="rwkv_scan",
    )(kkq, rq, ad, kdk, kend, aend, v, est)


def _rwkv_out_kernel(y_ref, bonus_ref, sz_ref, gg_ref, gb_ref, x_ref, w_ref, g_ref, o_ref, yb_ref, *, tm):
    gmat = _group_matrix(LANES)
    inv_n = 1.0 / RWKV_HEAD_DIM
    half = tm // 2
    for hf in range(2):
        rows = slice(hf * half, (hf + 1) * half)
        for cb in range(D_MODEL // LANES):
            sl = slice(cb * LANES, (cb + 1) * LANES)
            y = y_ref[0, rows, sl] + y_ref[1, rows, sl]
            mu = _group_sum(y, gmat) * inv_n
            yc = y - mu
            var = _group_sum(yc * yc, gmat) * inv_n
            yn = yc * lax.rsqrt(var + GN_EPS) * gg_ref[:, sl] + gb_ref[:, sl]
            yb_ref[rows, sl] = ((yn + bonus_ref[rows, sl]) * sz_ref[rows, sl].astype(F32)).astype(BF16)
        _proj_norm_residual(yb_ref[rows, :], rows, w_ref, g_ref, x_ref, o_ref)


def _rwkv_out(y2, bonus, sz, gn_g, gn_b, x2, w_bf, layer, g):
    rows = x2.shape[0]
    tm = 256
    row_spec = pl.BlockSpec((tm, D_MODEL), lambda i: (i, 0))
    vec = pl.BlockSpec((1, D_MODEL), lambda i: (0, 0))
    return pl.pallas_call(
        functools.partial(_rwkv_out_kernel, tm=tm),
        grid=(rows // tm,),
        in_specs=[pl.BlockSpec((2, tm, D_MODEL), lambda i: (0, i, 0)), row_spec, row_spec, vec, vec, row_spec,
                  pl.BlockSpec((None, D_MODEL, D_MODEL), lambda i: (layer, 0, 0)), vec],
        out_specs=row_spec,
        out_shape=jax.ShapeDtypeStruct((rows, D_MODEL), F32),
        scratch_shapes=[pltpu.VMEM((tm, D_MODEL), BF16)],
        compiler_params=_cparams("parallel"),
        name="rwkv_out",
    )(y2, bonus, sz, gn_g, gn_b, x2, w_bf, g)


def _trunk(x, p, rope):
    batch, seq, _ = x.shape
    x2 = x.reshape(batch * seq, D_MODEL)
    v_first = None
    depth = p["norm_pre"].shape[0]
    for layer in range(depth):
        j = layer // 2
        g_pre = p["norm_pre"][layer][None, :]
        g_post = p["norm_post"][layer][None, :]
        if layer % 2 == 0:
            w_in = p["att_w_in"]
            outs, lses = [], []
            for gidx, (_, dil) in enumerate(ATT_GROUPS):
                q, k, v = _att_in(x2, g_pre, w_in, j, *rope[gidx], seq, gidx, dil)
                o, lse = _attn_group(q, k, v, dil, batch, seq)
                outs.append(o)
                lses.append(lse)
            z = _gate_in(x2, g_pre, w_in, j)
            x2 = _att_out(outs, lses, z, x2, p["att_w_out"], j, g_post)
        else:
            vres = None if j == 0 else (p["rwkv_v0"][j - 1][None, :], p["rwkv_v1"][j - 1], p["rwkv_v2"][j - 1])
            kkq, rq, ad, kdk, kend, aend, est, v, bonus, sz = _rwkv_in(
                x2, seq, g_pre, p["rwkv_mu_prev"][j], p["rwkv_mu_next"][j], p["rwkv_w_in"], j,
                p["rwkv_w0"][j], p["rwkv_w1"][j], p["rwkv_w2"][j],
                p["rwkv_a0"][j], p["rwkv_a1"][j], p["rwkv_a2"][j],
                p["rwkv_k_k"][j][None, :], p["rwkv_k_a"][j][None, :], p["rwkv_r_k"][j].reshape(2, D_MODEL),
                vres, v_first)
            if j == 0:
                v_first = v
            y2 = _rwkv_scan(kkq, rq, ad, kdk, kend, aend, est, v, batch, seq)
            x2 = _rwkv_out(y2, bonus, sz, p["rwkv_gn_g"][j][None, :], p["rwkv_gn_b"][j][None, :], x2,
                           p["rwkv_w_out"], j, g_post)
    return x2.reshape(batch, seq, D_MODEL)


def kernel(x_prompt, x_sample, norm_pre, norm_post, att_w_in, att_w_out, rwkv_mu_prev, rwkv_mu_next, rwkv_w_in, rwkv_w0, rwkv_w1, rwkv_w2, rwkv_a0, rwkv_a1, rwkv_a2, rwkv_v0, rwkv_v1, rwkv_v2, rwkv_k_k, rwkv_k_a, rwkv_r_k, rwkv_gn_g, rwkv_gn_b, rwkv_w_out):
    p = dict(
        norm_pre=norm_pre, norm_post=norm_post,
        att_w_in=att_w_in.astype(BF16), att_w_out=att_w_out.astype(BF16),
        rwkv_mu_prev=rwkv_mu_prev, rwkv_mu_next=rwkv_mu_next, rwkv_w_in=rwkv_w_in.astype(BF16),
        rwkv_w0=rwkv_w0, rwkv_w1=rwkv_w1, rwkv_w2=rwkv_w2,
        rwkv_a0=rwkv_a0, rwkv_a1=rwkv_a1, rwkv_a2=rwkv_a2,
        rwkv_v0=rwkv_v0, rwkv_v1=rwkv_v1, rwkv_v2=rwkv_v2,
        rwkv_k_k=rwkv_k_k, rwkv_k_a=rwkv_k_a, rwkv_r_k=rwkv_r_k,
        rwkv_gn_g=rwkv_gn_g, rwkv_gn_b=rwkv_gn_b, rwkv_w_out=rwkv_w_out.astype(BF16),
    )
    max_seq = max(x_prompt.shape[1], x_sample.shape[1])
    rope = [_rope_tables(max_seq, dil) for _, dil in ATT_GROUPS]
    return (_trunk(x_prompt, p, rope), _trunk(x_sample, p, rope))
```

```python
import functools
import math

import jax
import jax.numpy as jnp
from jax import lax
from jax.experimental import pallas as pl
from jax.experimental.pallas import tpu as pltpu

F32 = jnp.float32
BF16 = jnp.bfloat16

D_MODEL = 2048
LANES = 128
MXU_COLS = 256
ATT_HEAD_DIM = 128
ATT_GROUPS = ((128, 1), (512, 4), (2048, 16))
ATT_HALF = 64
ROPE_THETA = 10000.0
RWKV_HEAD_DIM = 64
LORA_PAD = 128
RMS_EPS = 1e-6
GN_EPS = 64e-5
NEG_INF = -1e30
CHUNK = 64
ATT_IN_ROWS = 512
CHUNKS_PER_TRIP = 2
VMEM_LIMIT_BYTES = 56 * 1024 * 1024

NT_DIMS = (((1,), (1,)), ((), ()))
TN_DIMS = (((0,), (0,)), ((), ()))


def _cparams(*sem):
    return pltpu.CompilerParams(dimension_semantics=sem, vmem_limit_bytes=VMEM_LIMIT_BYTES)


def _rms_scale(x):
    return lax.rsqrt(jnp.mean(x * x, axis=-1, keepdims=True) + RMS_EPS)


def _sigmoid(x):
    return 0.5 * jnp.tanh(0.5 * x) + 0.5


def _split_dot(lhs_bf, x):
    hi = x.astype(BF16)
    lo = (x - hi.astype(F32)).astype(BF16)
    return (jnp.dot(lhs_bf, hi, preferred_element_type=F32)
            + jnp.dot(lhs_bf, lo, preferred_element_type=F32))


def _group_sum(x, gmat):
    hi = x.astype(BF16)
    lo = (x - hi.astype(F32)).astype(BF16)
    return (jnp.dot(hi, gmat, preferred_element_type=F32)
            + jnp.dot(lo, gmat, preferred_element_type=F32))


def _group_matrix(n):
    r = lax.broadcasted_iota(jnp.int32, (n, n), 0) // RWKV_HEAD_DIM
    c = lax.broadcasted_iota(jnp.int32, (n, n), 1) // RWKV_HEAD_DIM
    return jnp.where(r == c, 1.0, 0.0).astype(BF16)


def _rope_table_kernel(invf_ref, cos_ref, sin_ref, *, dil):
    rows = cos_ref.shape[0]
    base = pl.program_id(0) * rows
    local = lax.broadcasted_iota(jnp.int32, (rows, LANES), 0)
    per_res = rows // dil
    pos = (base + (local % per_res) * dil + local // per_res).astype(F32)
    ang = pos * invf_ref[...]
    lane = lax.broadcasted_iota(jnp.int32, (rows, LANES), 1)
    s = jnp.sin(ang)
    cos_ref[...] = jnp.cos(ang)
    sin_ref[...] = jnp.where(lane < ATT_HEAD_DIM // 2, -s, s)


def _rope_tables(seq, dil):
    half = ATT_HEAD_DIM // 2
    inv_freq = 1.0 / (ROPE_THETA ** (jnp.arange(half, dtype=F32) * 2.0 / ATT_HEAD_DIM))
    invf = jnp.concatenate([inv_freq, inv_freq])[None, :]
    rows = ATT_IN_ROWS
    return pl.pallas_call(
        functools.partial(_rope_table_kernel, dil=dil),
        grid=(seq // rows,),
        in_specs=[pl.BlockSpec((1, LANES), lambda i: (0, 0))],
        out_specs=[pl.BlockSpec((rows, LANES), lambda i: (i, 0))] * 2,
        out_shape=[jax.ShapeDtypeStruct((seq, LANES), F32)] * 2,
        compiler_params=_cparams("arbitrary"),
        name=f"rope_table_d{dil}",
    )(invf)


def _att_in_kernel(x_ref, g_ref, wq_ref, wk_ref, wv_ref, cos_ref, sin_ref, q_ref, k_ref, v_ref, h_ref, *hs_ref,
                   tm, tn, dil, scale):
    per_res = tm // dil

    @pl.when(pl.program_id(1) == 0)
    def _():
        x = x_ref[...]
        h = x * _rms_scale(x) * g_ref[...]
        if dil == 1:
            h_ref[...] = h.astype(BF16)
        else:
            for cb in range(D_MODEL // LANES):
                sl = slice(cb * LANES, (cb + 1) * LANES)
                hs_ref[0][cb] = h[:, sl]
                for r in range(dil):
                    h_ref[r * per_res:(r + 1) * per_res, sl] = (
                        hs_ref[0][cb, pl.ds(r, per_res, stride=dil), :].astype(BF16))

    h = h_ref[...]
    cos = cos_ref[...]
    sin = sin_ref[...]
    for w_ref, o_ref, rope, sc in ((wq_ref, q_ref, True, scale), (wk_ref, k_ref, True, None),
                                   (wv_ref, v_ref, False, None)):
        for cb in range(tn // MXU_COLS):
            acc = jnp.dot(h, w_ref[:, cb * MXU_COLS:(cb + 1) * MXU_COLS], preferred_element_type=F32)
            for hh in range(MXU_COLS // LANES):
                sl = slice(cb * MXU_COLS + hh * LANES, cb * MXU_COLS + (hh + 1) * LANES)
                t = acc[:, hh * LANES:(hh + 1) * LANES]
                if rope:
                    t = t * cos + pltpu.roll(t, ATT_HEAD_DIM // 2, axis=1) * sin
                if sc is not None:
                    t = t * sc
                t = t.astype(BF16)
                for r in range(dil):
                    o_ref[r, :, sl] = t[r * per_res:(r + 1) * per_res]


def _att_in(x2, g, w_bf, layer, cos_t, sin_t, seq, gidx, dil):
    rows = x2.shape[0]
    tm, tn = ATT_IN_ROWS, 512
    tiles_per_seq = seq // tm
    ncol = D_MODEL // tn
    kern = functools.partial(_att_in_kernel, tm=tm, tn=tn, dil=dil, scale=ATT_HEAD_DIM ** -0.5)

    def wcol(part):
        return pl.BlockSpec((None, D_MODEL, tn), lambda i, j: (layer, 0, (gidx * 3 + part) * ncol + j))

    out_spec = pl.BlockSpec((dil, tm // dil, tn), lambda i, j: (0, i, j))
    out_sds = jax.ShapeDtypeStruct((dil, rows // dil, D_MODEL), BF16)
    scratch = [pltpu.VMEM((tm, D_MODEL), BF16)]
    if dil > 1:
        scratch.append(pltpu.VMEM((D_MODEL // LANES, tm, LANES), F32))
    return pl.pallas_call(
        kern,
        grid=(rows // tm, ncol),
        in_specs=[
            pl.BlockSpec((tm, D_MODEL), lambda i, j: (i, 0)),
            pl.BlockSpec((1, D_MODEL), lambda i, j: (0, 0)),
            wcol(0), wcol(1), wcol(2),
            pl.BlockSpec((tm, LANES), lambda i, j: (i % tiles_per_seq, 0)),
            pl.BlockSpec((tm, LANES), lambda i, j: (i % tiles_per_seq, 0)),
        ],
        out_specs=[out_spec] * 3,
        out_shape=[out_sds] * 3,
        scratch_shapes=scratch,
        compiler_params=_cparams("parallel", "arbitrary"),
        name=f"att_in_d{dil}",
    )(x2, g, w_bf, w_bf, w_bf, cos_t, sin_t)


def _gate_in_kernel(x_ref, g_ref, w_ref, o_ref):
    x = x_ref[...]
    h = (x * _rms_scale(x) * g_ref[...]).astype(BF16)
    o_ref[...] = jnp.dot(h, w_ref[...], preferred_element_type=F32).astype(BF16)


def _gate_in(x2, g, w_bf, layer):
    rows = x2.shape[0]
    tm = 512
    col0 = 3 * len(ATT_GROUPS)
    return pl.pallas_call(
        _gate_in_kernel,
        grid=(rows // tm,),
        in_specs=[
            pl.BlockSpec((tm, D_MODEL), lambda i: (i, 0)),
            pl.BlockSpec((1, D_MODEL), lambda i: (0, 0)),
            pl.BlockSpec((None, D_MODEL, D_MODEL), lambda i: (layer, 0, col0)),
        ],
        out_specs=pl.BlockSpec((tm, D_MODEL), lambda i: (i, 0)),
        out_shape=jax.ShapeDtypeStruct((rows, D_MODEL), BF16),
        compiler_params=_cparams("parallel"),
        name="att_gate_in",
    )(x2, g, w_bf)


def _attn_kernel(q_ref, kp_ref, kc_ref, kn_ref, vp_ref, vc_ref, vn_ref, o_ref, lse_ref, *, bq, sb, sub_len, nh):
    i = pl.program_id(3)
    nk = sb + 2 * ATT_HALF
    nsub = bq // sb
    ii = lax.broadcasted_iota(jnp.int32, (sb, nk), 0)
    jj = lax.broadcasted_iota(jnp.int32, (sb, nk), 1)
    rel = jj - ii
    in_band = (rel >= 0) & (rel <= 2 * ATT_HALF)
    valid = []
    for u in range(nsub):
        kpos = i * bq + u * sb - ATT_HALF + jj
        valid.append(in_band & (kpos >= 0) & (kpos < sub_len))
    lane = lax.broadcasted_iota(jnp.int32, (sb, LANES), 1)
    units = [(hh, u) for hh in range(nh) for u in range(nsub)]

    def scores(hh, u):
        sl = slice(hh * LANES, (hh + 1) * LANES)
        kcat = jnp.concatenate([kp_ref[:, sl], kc_ref[:, sl], kn_ref[:, sl]], axis=0)
        s = lax.dot_general(q_ref[u * sb:(u + 1) * sb, sl], kcat[u * sb:u * sb + nk], NT_DIMS,
                            preferred_element_type=F32)
        return jnp.where(valid[u], s, NEG_INF)

    def finish(hh, u, s, lse_acc):
        sl = slice(hh * LANES, (hh + 1) * LANES)
        vcat = jnp.concatenate([vp_ref[:, sl], vc_ref[:, sl], vn_ref[:, sl]], axis=0)
        m = jnp.max(s, axis=-1, keepdims=True)
        p = jnp.exp(s - m)
        l = jnp.sum(p, axis=-1, keepdims=True)
        o = jnp.dot(p.astype(BF16), vcat[u * sb:u * sb + nk], preferred_element_type=F32)
        o_ref[u * sb:(u + 1) * sb, sl] = (o / l).astype(BF16)
        lse_acc[u] = jnp.where(lane == hh, m + jnp.log(l), lse_acc[u])

    lse_acc = [jnp.zeros((sb, LANES), F32) for _ in range(nsub)]
    s_prev = scores(*units[0])
    for k in range(1, len(units)):
        s_next = scores(*units[k])
        finish(*units[k - 1], s_prev, lse_acc)
        s_prev = s_next
    finish(*units[-1], s_prev, lse_acc)
    for u in range(nsub):
        lse_ref[u * sb:(u + 1) * sb, :] = lse_acc[u]


def _attn_group(q, k, v, dil, batch, seq):
    sub_len = seq // dil
    bq = min(256, sub_len)
    sb = min(128, bq)
    hw = D_MODEL
    nh = hw // LANES
    hblocks = D_MODEL // hw
    nqb = sub_len // bq
    halo_per_q = bq // ATT_HALF
    n_halo = sub_len // ATT_HALF

    cur = pl.BlockSpec((None, bq, hw), lambda b, r, h, i: (r, b * nqb + i, h))
    prev = pl.BlockSpec((None, ATT_HALF, hw),
                        lambda b, r, h, i: (r, b * n_halo + jnp.maximum(i * halo_per_q - 1, 0), h))
    nxt = pl.BlockSpec((None, ATT_HALF, hw),
                       lambda b, r, h, i: (r, b * n_halo + jnp.minimum((i + 1) * halo_per_q, n_halo - 1), h))
    lse_spec = pl.BlockSpec((None, bq, LANES), lambda b, r, h, i: (r, b * nqb + i, h))
    rows = batch * sub_len
    kern = functools.partial(_attn_kernel, bq=bq, sb=sb, sub_len=sub_len, nh=nh)
    return pl.pallas_call(
        kern,
        grid=(batch, dil, hblocks, nqb),
        in_specs=[cur, prev, cur, nxt, prev, cur, nxt],
        out_specs=[cur, lse_spec],
        out_shape=[jax.ShapeDtypeStruct((dil, rows, D_MODEL), BF16),
                   jax.ShapeDtypeStruct((dil, rows, hblocks * LANES), F32)],
        compiler_params=_cparams("parallel", "parallel", "parallel", "arbitrary"),
        name=f"attn_d{dil}",
    )(q, k, k, k, v, v, v)


def _proj_norm_residual(y_bf, rows, w_ref, g_ref, x_ref, o_ref):
    out = jnp.dot(y_bf, w_ref[...], preferred_element_type=F32)
    o_ref[rows, :] = x_ref[rows, :] + out * _rms_scale(out) * g_ref[...]


def _att_out_kernel(o0_ref, o1_ref, o2_ref, l0_ref, l1_ref, l2_ref, z_ref, x_ref, w_ref, g_ref, o_ref,
                    so1, so2, sl1, sl2, y_ref, *, tm, dils, heads_per_block):
    nlb = l0_ref.shape[-1] // LANES
    for src, dst, d in ((l1_ref, sl1, dils[1]), (l2_ref, sl2, dils[2])):
        for lb in range(nlb):
            for r in range(d):
                dst[lb, pl.ds(r, tm // d, stride=d), :] = src[r, :, lb * LANES:(lb + 1) * LANES]
    wts = []
    for lb in range(nlb):
        l0, l1, l2 = l0_ref[0, :, lb * LANES:(lb + 1) * LANES], sl1[lb], sl2[lb]
        m = jnp.maximum(jnp.maximum(l0, l1), l2)
        e0, e1, e2 = jnp.exp(l0 - m), jnp.exp(l1 - m), jnp.exp(l2 - m)
        inv = 1.0 / (e0 + e1 + e2)
        wts.append((e0 * inv, e1 * inv, e2 * inv))
    for cb in range(D_MODEL // LANES):
        sl = slice(cb * LANES, (cb + 1) * LANES)
        for src, dst, d in ((o1_ref, so1, dils[1]), (o2_ref, so2, dils[2])):
            for r in range(d):
                dst[cb, pl.ds(r, tm // d, stride=d), :] = src[r, :, sl].astype(F32)
    half = tm // 2
    for hf in range(2):
        rows = slice(hf * half, (hf + 1) * half)
        for cb in range(D_MODEL // LANES):
            sl = slice(cb * LANES, (cb + 1) * LANES)
            w0, w1, w2 = wts[cb // heads_per_block]
            hl = cb % heads_per_block
            bc = lambda w: jnp.broadcast_to(w[rows, hl:hl + 1], (half, LANES))
            o = (bc(w0) * o0_ref[0, rows, sl].astype(F32) + bc(w1) * so1[cb, rows, :]
                 + bc(w2) * so2[cb, rows, :])
            z = z_ref[rows, sl].astype(F32)
            y_ref[rows, sl] = (o * (z * _sigmoid(z))).astype(BF16)
        _proj_norm_residual(y_ref[rows, :], rows, w_ref, g_ref, x_ref, o_ref)


def _att_out(outs, lses, z, x2, w_bf, layer, g):
    rows = x2.shape[0]
    tm = 256
    dils = tuple(d for _, d in ATT_GROUPS)
    lse_w = lses[0].shape[-1]
    nlb = lse_w // LANES
    row_spec = pl.BlockSpec((tm, D_MODEL), lambda i: (i, 0))

    def res_spec(d, width):
        return pl.BlockSpec((d, tm // d, width), lambda i: (0, i, 0))

    kern = functools.partial(_att_out_kernel, tm=tm, dils=dils, heads_per_block=D_MODEL // LANES // nlb)
    return pl.pallas_call(
        kern,
        grid=(rows // tm,),
        in_specs=[res_spec(d, D_MODEL) for d in dils] + [res_spec(d, lse_w) for d in dils] + [
            row_spec, row_spec,
            pl.BlockSpec((None, D_MODEL, D_MODEL), lambda i: (layer, 0, 0)),
            pl.BlockSpec((1, D_MODEL), lambda i: (0, 0)),
        ],
        out_specs=row_spec,
        out_shape=jax.ShapeDtypeStruct((rows, D_MODEL), F32),
        scratch_shapes=[pltpu.VMEM((D_MODEL // LANES, tm, LANES), F32)] * 2
        + [pltpu.VMEM((nlb, tm, LANES), F32)] * 2 + [pltpu.VMEM((tm, D_MODEL), BF16)],
        compiler_params=_cparams("parallel"),
        name="att_out",
    )(*outs, *lses, z, x2, w_bf, g)


def _chunk_tri(n, reverse):
    t = lax.broadcasted_iota(jnp.int32, (n, n), 0)
    s = lax.broadcasted_iota(jnp.int32, (n, n), 1)
    same = (t // CHUNK) == (s // CHUNK)
    order = (s >= t) if reverse else (s <= t)
    return jnp.where(same & order, 1.0, 0.0).astype(BF16)


def _rwkv_in_kernel(*refs, tm, tn, tiles_per_seq, has_vres):
    (x_ref, xp_ref, xn_ref, g_ref, mup_ref, mun_ref,
     wr_ref, wk_ref, wv_ref, wz_ref,
     w1_ref, w2_ref, w0_ref, a1_ref, a2_ref, a0_ref,
     kk_ref, ka_ref, rk_ref) = refs[:19]
    pos = 19
    if has_vres:
        v1_ref, v2_ref, v0_ref, vf_ref = refs[pos:pos + 4]
        pos += 4
    (kkq_out, rq_out, ad_out, kdk_out, kend_out, aend_out, est_out, v_out, bonus_out, sz_out) = refs[pos:pos + 10]
    pos += 10
    xs_ref, hw_ref, ha_ref = refs[pos:pos + 3]
    hv_ref = refs[pos + 3] if has_vres else None

    i = pl.program_id(0)
    j = pl.program_id(1)
    mix_slot = {0: 0, 2: 1, 3: 2, 5: 3}

    @pl.when(j == 0)
    def _():
        t_in_seq = i % tiles_per_seq
        keep_prev = jnp.where(t_in_seq == 0, 0.0, 1.0).astype(F32)
        keep_next = jnp.where(t_in_seq == tiles_per_seq - 1, 0.0, 1.0).astype(F32)
        sx = _rms_scale(x_ref[...])
        xp = xp_ref[7:8, :]
        xn = xn_ref[0:1, :]
        sp = _rms_scale(xp) * keep_prev
        sn = _rms_scale(xn) * keep_next
        cw = 512
        row = lax.broadcasted_iota(jnp.int32, (tm, cw), 0)
        hw_acc = [jnp.zeros((tm, LORA_PAD), F32) for _ in range(2)]
        ha_acc = [jnp.zeros((tm, LORA_PAD), F32) for _ in range(2)]
        hv_acc = jnp.zeros((tm, LORA_PAD), F32)
        for cb in range(D_MODEL // cw):
            sl = slice(cb * cw, (cb + 1) * cw)
            g = g_ref[:, sl]
            h = x_ref[:, sl] * sx * g
            hp_row = xp[:, sl] * sp * g
            hn_row = xn[:, sl] * sn * g
            h_prev = jnp.where(row == 0, hp_row, pltpu.roll(h, 1, axis=0))
            h_next = jnp.where(row == tm - 1, hn_row, pltpu.roll(h, tm - 1, axis=0))
            h16 = h.astype(BF16)
            dp = (h_prev - h).astype(BF16)
            dn = (h_next - h).astype(BF16)
            mixes = {}
            for t in range(6):
                mixes[t] = h16 + dp * mup_ref[t:t + 1, sl].astype(BF16) + dn * mun_ref[t:t + 1, sl].astype(BF16)
                if t in mix_slot:
                    xs_ref[mix_slot[t], :, sl] = mixes[t]
            for c in range(2):
                hw_acc[c] = hw_acc[c] + jnp.dot(mixes[1], w1_ref[c, sl, :], preferred_element_type=F32)
                ha_acc[c] = ha_acc[c] + jnp.dot(mixes[4], a1_ref[c, sl, :], preferred_element_type=F32)
            if has_vres:
                hv_acc = hv_acc + jnp.dot(mixes[3], v1_ref[sl, :], preferred_element_type=F32)
        for c in range(2):
            hw_ref[c] = jnp.tanh(hw_acc[c]).astype(BF16)
            ha_ref[c] = ha_acc[c].astype(BF16)
        if has_vres:
            hv_ref[...] = hv_acc.astype(BF16)

    wl = [w0_ref[c:c + 1, :] + jnp.dot(hw_ref[c], w2_ref[c], preferred_element_type=F32) for c in range(2)]
    al = [a0_ref[c:c + 1, :] + jnp.dot(ha_ref[c], a2_ref[c], preferred_element_type=F32) for c in range(2)]
    if has_vres:
        gl = v0_ref[...] + jnp.dot(hv_ref[...], v2_ref[...], preferred_element_type=F32)
    k = jnp.dot(xs_ref[1], wk_ref[...], preferred_element_type=F32)
    r = jnp.dot(xs_ref[0], wr_ref[...], preferred_element_type=F32)
    z = jnp.dot(xs_ref[3], wz_ref[...], preferred_element_type=F32)
    v = jnp.dot(xs_ref[2], wv_ref[...], preferred_element_type=F32)
    if has_vres:
        v = v + (vf_ref[...] - v) * _sigmoid(gl)
    gmat = _group_matrix(tn)
    kk = k * kk_ref[...]
    kk = kk * lax.rsqrt(jnp.maximum(_group_sum(kk * kk, gmat), 1e-24))
    k_a = ka_ref[...]
    rk_acc = jnp.zeros((tm, tn), F32)
    nchunk = tm // CHUNK
    half = 256
    for c in range(2):
        lw = -math.exp(-0.5) * _sigmoid(wl[c])
        a = _sigmoid(al[c])
        kd = k * (1.0 + (a - 1.0) * k_a)
        rk_acc = rk_acc + r * kd * rk_ref[c:c + 1, :]
        tri = _chunk_tri(half, reverse=(c == 1))
        g = jnp.concatenate([_split_dot(tri, lw[hh * half:(hh + 1) * half]) for hh in range(tm // half)], axis=0)
        g3 = g.reshape(nchunk, CHUNK, tn)
        last = 0 if c == 1 else CHUNK - 1
        g_tot = g3[:, last:last + 1, :]
        e_tot = jnp.exp(g_tot)
        est_out[c] = e_tot.reshape(nchunk, tn)
        e_q = jnp.exp(g)
        e_qp = jnp.exp(g - lw)
        e_k = jnp.exp(-g)
        e_end = (e_k.reshape(nchunk, CHUNK, tn) * e_tot).reshape(tm, tn)
        kka = kk * a
        kkq_out[c] = (kk * e_qp).astype(BF16)
        rq_out[c] = (r * e_q).astype(BF16)
        ad_out[c] = (kka * e_k).astype(BF16)
        kdk_out[c] = (kd * e_k).astype(BF16)
        kend_out[c] = (kd * e_end).astype(BF16)
        aend_out[c] = (kka * e_end).astype(BF16)
    v_out[...] = v
    bonus_out[...] = _group_sum(rk_acc, gmat) * v
    sz_out[...] = (z * _sigmoid(z)).astype(BF16)


def _pad_lora(w1, w2):
    rank = w1.shape[-1]
    pad1 = [(0, 0)] * (w1.ndim - 1) + [(0, LORA_PAD - rank)]
    pad2 = [(0, 0)] * (w2.ndim - 2) + [(0, LORA_PAD - rank), (0, 0)]
    return jnp.pad(w1, pad1).astype(BF16), jnp.pad(w2, pad2).astype(BF16)


def _rwkv_in(x2, seq, g, mu_prev, mu_next, w_in_bf, layer, w0, w1, w2, a0, a1, a2, k_k, k_a, r_k, vres, v_first):
    rows = x2.shape[0]
    tm, tn = 512, 256
    tiles_per_seq = seq // tm
    ncol = D_MODEL // tn
    has_vres = vres is not None
    w1p, w2p = _pad_lora(w1, w2)
    a1p, a2p = _pad_lora(a1, a2)
    sub = tm // 8
    nsub = rows // 8

    def const2(shape):
        return pl.BlockSpec(shape, lambda i, j: (0, 0))

    def col2(nrow):
        return pl.BlockSpec((nrow, tn), lambda i, j: (0, j))

    def wcol(gi):
        return pl.BlockSpec((None, D_MODEL, tn), lambda i, j: (layer, 0, gi * ncol + j))

    in_specs = [
        pl.BlockSpec((tm, D_MODEL), lambda i, j: (i, 0)),
        pl.BlockSpec((8, D_MODEL), lambda i, j: (jnp.maximum(i * sub - 1, 0), 0)),
        pl.BlockSpec((8, D_MODEL), lambda i, j: (jnp.minimum((i + 1) * sub, nsub - 1), 0)),
        const2((1, D_MODEL)), const2((6, D_MODEL)), const2((6, D_MODEL)),
        wcol(0), wcol(1), wcol(2), wcol(3),
        pl.BlockSpec((2, D_MODEL, LORA_PAD), lambda i, j: (0, 0, 0)),
        pl.BlockSpec((2, LORA_PAD, tn), lambda i, j: (0, 0, j)),
        col2(2),
        pl.BlockSpec((2, D_MODEL, LORA_PAD), lambda i, j: (0, 0, 0)),
        pl.BlockSpec((2, LORA_PAD, tn), lambda i, j: (0, 0, j)),
        col2(2),
        col2(1), col2(1), col2(2),
    ]
    args = [x2, x2, x2, g, mu_prev, mu_next, w_in_bf, w_in_bf, w_in_bf, w_in_bf,
            w1p, w2p, w0, a1p, a2p, a0, k_k, k_a, r_k]
    if has_vres:
        v0, v1, v2 = vres
        v1p, v2p = _pad_lora(v1, v2)
        in_specs += [const2((D_MODEL, LORA_PAD)), col2(LORA_PAD), col2(1),
                     pl.BlockSpec((tm, tn), lambda i, j: (i, j))]
        args += [v1p, v2p, v0, v_first]

    tile = pl.BlockSpec((tm, tn), lambda i, j: (i, j))
    tile2 = pl.BlockSpec((2, tm, tn), lambda i, j: (0, i, j))
    est_spec = pl.BlockSpec((2, tm // CHUNK, tn), lambda i, j: (0, i, j))
    sds_bf2 = jax.ShapeDtypeStruct((2, rows, D_MODEL), BF16)
    sds = jax.ShapeDtypeStruct((rows, D_MODEL), F32)
    scratch = [pltpu.VMEM((4, tm, D_MODEL), BF16), pltpu.VMEM((2, tm, LORA_PAD), BF16),
               pltpu.VMEM((2, tm, LORA_PAD), BF16)]
    if has_vres:
        scratch.append(pltpu.VMEM((tm, LORA_PAD), BF16))
    kern = functools.partial(_rwkv_in_kernel, tm=tm, tn=tn, tiles_per_seq=tiles_per_seq, has_vres=has_vres)
    return pl.pallas_call(
        kern,
        grid=(rows // tm, ncol),
        in_specs=in_specs,
        out_specs=[tile2] * 6 + [est_spec, tile, tile, tile],
        out_shape=[sds_bf2] * 6 + [jax.ShapeDtypeStruct((2, rows // CHUNK, D_MODEL), F32), sds, sds,
                                   jax.ShapeDtypeStruct((rows, D_MODEL), BF16)],
        scratch_shapes=scratch,
        compiler_params=_cparams("parallel", "arbitrary"),
        name="rwkv_in_vres" if has_vres else "rwkv_in",
    )(*args)


def _rwkv_scan_kernel(kkq_ref, rq_ref, ad_ref, kdk_ref, kend_ref, aend_ref, v_ref, est_ref, y_ref, state_ref,
                      *, tb, npair):
    rev = pl.program_id(1) == 1
    step = pl.program_id(3)
    nchunk = tb // CHUNK
    pairs = range(npair)

    @pl.when(step == 0)
    def _():
        state_ref[...] = jnp.zeros_like(state_ref)

    t = lax.broadcasted_iota(jnp.int32, (CHUNK, LANES), 0)
    lane = lax.broadcasted_iota(jnp.int32, (CHUNK, LANES), 1)
    s = lane & (CHUNK - 1)
    d = jnp.where(rev, s - t, t - s)
    strict = d > 0
    incl = d >= 0
    eye = jnp.where(d == 0, 1.0, 0.0).astype(F32)
    first_half = lane < CHUNK
    half_a = jnp.where(first_half, 1.0, 0.0).astype(BF16)
    half_b = jnp.where(first_half, 0.0, 1.0).astype(BF16)
    rr = lax.broadcasted_iota(jnp.int32, (LANES, LANES), 0)
    cc = lax.broadcasted_iota(jnp.int32, (LANES, LANES), 1)
    bd_mask = (rr < CHUNK) == (cc < CHUNK)
    est_rows = lax.broadcasted_iota(jnp.int32, (nchunk, npair * LANES), 0)

    def bd16(y):
        return jnp.concatenate([y * half_a, y * half_b], axis=0)

    def bd32(y):
        return jnp.concatenate([jnp.where(first_half, y, 0.0), jnp.where(first_half, 0.0, y)],
                               axis=0).astype(BF16)

    def mm(a, b):
        return jnp.dot(a, b, preferred_element_type=F32)

    def body(ci, carry):
        first = ci * CHUNKS_PER_TRIP
        cs = [jnp.where(rev, nchunk - 1 - (first + u), first + u) for u in range(CHUNKS_PER_TRIP)]
        jobs = [(pl.ds(pl.multiple_of(c * CHUNK, CHUNK), CHUNK), slice(p * LANES, (p + 1) * LANES))
                for c in cs for p in pairs]
        n = range(len(jobs))
        kkq = [kkq_ref[rw, ln] for rw, ln in jobs]
        rq = [rq_ref[rw, ln] for rw, ln in jobs]
        v = [v_ref[rw, ln] for rw, ln in jobs]
        gram = [lax.dot_general(
            jnp.concatenate([kkq[i], rq[i]], axis=0),
            jnp.concatenate([bd16(ad_ref[jobs[i]]), bd16(kdk_ref[jobs[i]])], axis=0),
            NT_DIMS, preferred_element_type=F32) for i in n]
        a_mat = [jnp.where(strict, gram[i][0:CHUNK, 0:LANES], 0.0) for i in n]
        ak = [jnp.where(strict, gram[i][0:CHUNK, LANES:], 0.0) for i in n]
        bra = [jnp.where(incl, gram[i][CHUNK:, 0:LANES], 0.0).astype(BF16) for i in n]
        brk = [jnp.where(incl, gram[i][CHUNK:, LANES:], 0.0) for i in n]
        early = [mm(jnp.concatenate([ak[i], brk[i]], axis=0).astype(BF16), bd32(v[i])) for i in n]
        a_pow = [mm(a_mat[i].astype(BF16), bd32(a_mat[i])) for i in n]
        inv = [eye - a_mat[i] for i in n]
        for _ in range(4):
            st = [mm(jnp.concatenate([inv[i], a_pow[i]], axis=0).astype(BF16), bd32(a_pow[i])) for i in n]
            inv = [inv[i] + st[i][0:CHUNK] for i in n]
            a_pow = [st[i][CHUNK:] for i in n]
        inv = [inv[i] + mm(inv[i].astype(BF16), bd32(a_pow[i])) for i in n]
        wu = [mm(inv[i].astype(BF16), jnp.concatenate([bd16(kkq[i]), bd32(early[i][0:CHUNK])], axis=1))
              for i in n]
        ry = [mm(bra[i], jnp.concatenate([bd32(wu[i][:, 0:LANES]), bd32(wu[i][:, LANES:])], axis=1))
              for i in n]
        aend = [aend_ref[jobs[i]] for i in n]
        xraw = [lax.dot_general(wu[i][:, 0:LANES].astype(BF16), aend[i], TN_DIMS, preferred_element_type=F32)
                for i in n]
        nraw = [lax.dot_general(jnp.concatenate([v[i], -wu[i][:, LANES:]], axis=0).astype(BF16),
                                jnp.concatenate([kend_ref[jobs[i]], aend[i]], axis=0),
                                TN_DIMS, preferred_element_type=F32) for i in n]
        rqp = [(rq[i].astype(F32) - ry[i][:, 0:LANES]).astype(BF16) for i in n]
        yv = [early[i][CHUNK:] - ry[i][:, LANES:] for i in n]
        xmat = [jnp.where(bd_mask, xraw[i], 0.0).astype(BF16) for i in n]
        nmat = [jnp.where(bd_mask, nraw[i], 0.0) for i in n]
        est_all = est_ref[...]
        state = [state_ref[p] for p in pairs]
        for u in range(CHUNKS_PER_TRIP):
            est_row = jnp.sum(jnp.where(est_rows == cs[u], est_all, 0.0), axis=0, keepdims=True)
            idx = [u * npair + p for p in pairs]
            s16 = [state[p].astype(BF16) for p in pairs]
            ys = [yv[idx[p]] + lax.dot_general(rqp[idx[p]], s16[p], NT_DIMS, preferred_element_type=F32)
                  for p in pairs]
            sx = [mm(s16[p], xmat[idx[p]]) for p in pairs]
            for p in pairs:
                rw, ln = jobs[idx[p]]
                y_ref[rw, ln] = ys[p]
                state[p] = state[p] * est_row[:, ln] - sx[p] + nmat[idx[p]]
        for p in pairs:
            state_ref[p] = state[p]
        return carry

    lax.fori_loop(0, nchunk // CHUNKS_PER_TRIP, body, 0)


def _rwkv_scan(kkq, rq, ad, kdk, kend, aend, est, v, batch, seq):
    tb, lw_lanes = 512, 1024
    nt = seq // tb
    npair = lw_lanes // LANES

    def tmap(i, dr):
        return i + dr * (nt - 1 - 2 * i)

    v_spec = pl.BlockSpec((tb, lw_lanes), lambda b, dr, h, i: (b * nt + tmap(i, dr), h))
    d_spec = pl.BlockSpec((None, tb, lw_lanes), lambda b, dr, h, i: (dr, b * nt + tmap(i, dr), h))
    e_spec = pl.BlockSpec((None, tb // CHUNK, lw_lanes), lambda b, dr, h, i: (dr, b * nt + tmap(i, dr), h))
    kern = functools.partial(_rwkv_scan_kernel, tb=tb, npair=npair)
    return pl.pallas_call(
        kern,
        grid=(batch, 2, D_MODEL // lw_lanes, nt),
        in_specs=[d_spec] * 6 + [v_spec, e_spec],
        out_specs=d_spec,
        out_shape=jax.ShapeDtypeStruct((2, batch * seq, D_MODEL), F32),
        scratch_shapes=[pltpu.VMEM((npair, LANES, LANES), F32)],
        compiler_params=_cparams("parallel", "parallel", "parallel", "arbitrary"),
        name="rwkv_scan",
    )(kkq, rq, ad, kdk, kend, aend, v, est)


def _rwkv_out_kernel(y_ref, bonus_ref, sz_ref, gg_ref, gb_ref, x_ref, w_ref, g_ref, o_ref, yb_ref, *, tm):
    gmat = _group_matrix(LANES)
    inv_n = 1.0 / RWKV_HEAD_DIM
    half = tm // 2
    for hf in range(2):
        rows = slice(hf * half, (hf + 1) * half)
        for cb in range(D_MODEL // LANES):
            sl = slice(cb * LANES, (cb + 1) * LANES)
            y = y_ref[0, rows, sl] + y_ref[1, rows, sl]
            mu = _group_sum(y, gmat) * inv_n
            yc = y - mu
            var = _group_sum(yc * yc, gmat) * inv_n
            yn = yc * lax.rsqrt(var + GN_EPS) * gg_ref[:, sl] + gb_ref[:, sl]
            yb_ref[rows, sl] = ((yn + bonus_ref[rows, sl]) * sz_ref[rows, sl].astype(F32)).astype(BF16)
        _proj_norm_residual(yb_ref[rows, :], rows, w_ref, g_ref, x_ref, o_ref)


def _rwkv_out(y2, bonus, sz, gn_g, gn_b, x2, w_bf, layer, g):
    rows = x2.shape[0]
    tm = 256
    row_spec = pl.BlockSpec((tm, D_MODEL), lambda i: (i, 0))
    vec = pl.BlockSpec((1, D_MODEL), lambda i: (0, 0))
    return pl.pallas_call(
        functools.partial(_rwkv_out_kernel, tm=tm),
        grid=(rows // tm,),
        in_specs=[pl.BlockSpec((2, tm, D_MODEL), lambda i: (0, i, 0)), row_spec, row_spec, vec, vec, row_spec,
                  pl.BlockSpec((None, D_MODEL, D_MODEL), lambda i: (layer, 0, 0)), vec],
        out_specs=row_spec,
        out_shape=jax.ShapeDtypeStruct((rows, D_MODEL), F32),
        scratch_shapes=[pltpu.VMEM((tm, D_MODEL), BF16)],
        compiler_params=_cparams("parallel"),
        name="rwkv_out",
    )(y2, bonus, sz, gn_g, gn_b, x2, w_bf, g)


def _trunk(x, p, rope):
    batch, seq, _ = x.shape
    x2 = x.reshape(batch * seq, D_MODEL)
    v_first = None
    depth = p["norm_pre"].shape[0]
    for layer in range(depth):
        j = layer // 2
        g_pre = p["norm_pre"][layer][None, :]
        g_post = p["norm_post"][layer][None, :]
        if layer % 2 == 0:
            w_in = p["att_w_in"]
            outs, lses = [], []
            for gidx, (_, dil) in enumerate(ATT_GROUPS):
                q, k, v = _att_in(x2, g_pre, w_in, j, *rope[gidx], seq, gidx, dil)
                o, lse = _attn_group(q, k, v, dil, batch, seq)
                outs.append(o)
                lses.append(lse)
            z = _gate_in(x2, g_pre, w_in, j)
            x2 = _att_out(outs, lses, z, x2, p["att_w_out"], j, g_post)
        else:
            vres = None if j == 0 else (p["rwkv_v0"][j - 1][None, :], p["rwkv_v1"][j - 1], p["rwkv_v2"][j - 1])
            kkq, rq, ad, kdk, kend, aend, est, v, bonus, sz = _rwkv_in(
                x2, seq, g_pre, p["rwkv_mu_prev"][j], p["rwkv_mu_next"][j], p["rwkv_w_in"], j,
                p["rwkv_w0"][j], p["rwkv_w1"][j], p["rwkv_w2"][j],
                p["rwkv_a0"][j], p["rwkv_a1"][j], p["rwkv_a2"][j],
                p["rwkv_k_k"][j][None, :], p["rwkv_k_a"][j][None, :], p["rwkv_r_k"][j].reshape(2, D_MODEL),
                vres, v_first)
            if j == 0:
                v_first = v
            y2 = _rwkv_scan(kkq, rq, ad, kdk, kend, aend, est, v, batch, seq)
            x2 = _rwkv_out(y2, bonus, sz, p["rwkv_gn_g"][j][None, :], p["rwkv_gn_b"][j][None, :], x2,
                           p["rwkv_w_out"], j, g_post)
    return x2.reshape(batch, seq, D_MODEL)


def kernel(x_prompt, x_sample, norm_pre, norm_post, att_w_in, att_w_out, rwkv_mu_prev, rwkv_mu_next, rwkv_w_in, rwkv_w0, rwkv_w1, rwkv_w2, rwkv_a0, rwkv_a1, rwkv_a2, rwkv_v0, rwkv_v1, rwkv_v2, rwkv_k_k, rwkv_k_a, rwkv_r_k, rwkv_gn_g, rwkv_gn_b, rwkv_w_out):
    p = dict(
        norm_pre=norm_pre, norm_post=norm_post,
        att_w_in=att_w_in.astype(BF16), att_w_out=att_w_out.astype(BF16),
        rwkv_mu_prev=rwkv_mu_prev, rwkv_mu_next=rwkv_mu_next, rwkv_w_in=rwkv_w_in.astype(BF16),
        rwkv_w0=rwkv_w0, rwkv_w1=rwkv_w1, rwkv_w2=rwkv_w2,
        rwkv_a0=rwkv_a0, rwkv_a1=rwkv_a1, rwkv_a2=rwkv_a2,
        rwkv_v0=rwkv_v0, rwkv_v1=rwkv_v1, rwkv_v2=rwkv_v2,
        rwkv_k_k=rwkv_k_k, rwkv_k_a=rwkv_k_a, rwkv_r_k=rwkv_r_k,
        rwkv_gn_g=rwkv_gn_g, rwkv_gn_b=rwkv_gn_b, rwkv_w_out=rwkv_w_out.astype(BF16),
    )
    max_seq = max(x_prompt.shape[1], x_sample.shape[1])
    rope = [_rope_tables(max_seq, dil) for _, dil in ATT_GROUPS]
    return (_trunk(x_prompt, p, rope), _trunk(x_sample, p, rope))
```

```python
import functools
import math

import jax
import jax.numpy as jnp
from jax import lax
from jax.experimental import pallas as pl
from jax.experimental.pallas import tpu as pltpu

F32 = jnp.float32
BF16 = jnp.bfloat16

D_MODEL = 2048
LANES = 128
MXU_COLS = 256
ATT_HEAD_DIM = 128
ATT_GROUPS = ((128, 1), (512, 4), (2048, 16))
ATT_HALF = 64
ROPE_THETA = 10000.0
RWKV_HEAD_DIM = 64
LORA_PAD = 128
RMS_EPS = 1e-6
GN_EPS = 64e-5
NEG_INF = -1e30
CHUNK = 64
ATT_IN_ROWS = 512
OUT_PARTS = 2
CHUNKS_PER_TRIP = 2
VMEM_LIMIT_BYTES = 56 * 1024 * 1024

NT_DIMS = (((1,), (1,)), ((), ()))
TN_DIMS = (((0,), (0,)), ((), ()))


def _cparams(*sem):
    return pltpu.CompilerParams(dimension_semantics=sem, vmem_limit_bytes=VMEM_LIMIT_BYTES)


def _rms_scale(x):
    return lax.rsqrt(jnp.mean(x * x, axis=-1, keepdims=True) + RMS_EPS)


def _sigmoid(x):
    return 0.5 * jnp.tanh(0.5 * x) + 0.5


def _split_dot(lhs_bf, x):
    hi = x.astype(BF16)
    lo = (x - hi.astype(F32)).astype(BF16)
    return (jnp.dot(lhs_bf, hi, preferred_element_type=F32)
            + jnp.dot(lhs_bf, lo, preferred_element_type=F32))


def _group_sum(x, gmat):
    hi = x.astype(BF16)
    lo = (x - hi.astype(F32)).astype(BF16)
    return (jnp.dot(hi, gmat, preferred_element_type=F32)
            + jnp.dot(lo, gmat, preferred_element_type=F32))


def _group_matrix(n):
    r = lax.broadcasted_iota(jnp.int32, (n, n), 0) // RWKV_HEAD_DIM
    c = lax.broadcasted_iota(jnp.int32, (n, n), 1) // RWKV_HEAD_DIM
    return jnp.where(r == c, 1.0, 0.0).astype(BF16)


def _rope_table_kernel(invf_ref, cos_ref, sin_ref, *, dil):
    rows = cos_ref.shape[0]
    base = pl.program_id(0) * rows
    local = lax.broadcasted_iota(jnp.int32, (rows, LANES), 0)
    per_res = rows // dil
    pos = (base + (local % per_res) * dil + local // per_res).astype(F32)
    ang = pos * invf_ref[...]
    lane = lax.broadcasted_iota(jnp.int32, (rows, LANES), 1)
    s = jnp.sin(ang)
    cos_ref[...] = jnp.cos(ang)
    sin_ref[...] = jnp.where(lane < ATT_HEAD_DIM // 2, -s, s)


def _rope_tables(seq, dil):
    half = ATT_HEAD_DIM // 2
    inv_freq = 1.0 / (ROPE_THETA ** (jnp.arange(half, dtype=F32) * 2.0 / ATT_HEAD_DIM))
    invf = jnp.concatenate([inv_freq, inv_freq])[None, :]
    rows = ATT_IN_ROWS
    return pl.pallas_call(
        functools.partial(_rope_table_kernel, dil=dil),
        grid=(seq // rows,),
        in_specs=[pl.BlockSpec((1, LANES), lambda i: (0, 0))],
        out_specs=[pl.BlockSpec((rows, LANES), lambda i: (i, 0))] * 2,
        out_shape=[jax.ShapeDtypeStruct((seq, LANES), F32)] * 2,
        compiler_params=_cparams("arbitrary"),
        name=f"rope_table_d{dil}",
    )(invf)


def _att_in_kernel(x_ref, g_ref, wq_ref, wk_ref, wv_ref, cos_ref, sin_ref, q_ref, k_ref, v_ref, h_ref, *hs_ref,
                   tm, tn, dil, scale):
    per_res = tm // dil

    @pl.when(pl.program_id(1) == 0)
    def _():
        x = x_ref[...]
        h = x * _rms_scale(x) * g_ref[...]
        if dil == 1:
            h_ref[...] = h.astype(BF16)
        else:
            for cb in range(D_MODEL // LANES):
                sl = slice(cb * LANES, (cb + 1) * LANES)
                hs_ref[0][cb] = h[:, sl]
                for r in range(dil):
                    h_ref[r * per_res:(r + 1) * per_res, sl] = (
                        hs_ref[0][cb, pl.ds(r, per_res, stride=dil), :].astype(BF16))

    h = h_ref[...]
    cos = cos_ref[...]
    sin = sin_ref[...]
    for w_ref, o_ref, rope, sc in ((wq_ref, q_ref, True, scale), (wk_ref, k_ref, True, None),
                                   (wv_ref, v_ref, False, None)):
        for cb in range(tn // MXU_COLS):
            acc = jnp.dot(h, w_ref[:, cb * MXU_COLS:(cb + 1) * MXU_COLS], preferred_element_type=F32)
            for hh in range(MXU_COLS // LANES):
                sl = slice(cb * MXU_COLS + hh * LANES, cb * MXU_COLS + (hh + 1) * LANES)
                t = acc[:, hh * LANES:(hh + 1) * LANES]
                if rope:
                    t = t * cos + pltpu.roll(t, ATT_HEAD_DIM // 2, axis=1) * sin
                if sc is not None:
                    t = t * sc
                t = t.astype(BF16)
                for r in range(dil):
                    o_ref[r, :, sl] = t[r * per_res:(r + 1) * per_res]


def _att_in(x2, g, w_bf, layer, cos_t, sin_t, seq, gidx, dil):
    rows = x2.shape[0]
    tm, tn = ATT_IN_ROWS, 512
    tiles_per_seq = seq // tm
    ncol = D_MODEL // tn
    kern = functools.partial(_att_in_kernel, tm=tm, tn=tn, dil=dil, scale=ATT_HEAD_DIM ** -0.5)

    def wcol(part):
        return pl.BlockSpec((None, D_MODEL, tn), lambda i, j: (layer, 0, (gidx * 3 + part) * ncol + j))

    out_spec = pl.BlockSpec((dil, tm // dil, tn), lambda i, j: (0, i, j))
    out_sds = jax.ShapeDtypeStruct((dil, rows // dil, D_MODEL), BF16)
    scratch = [pltpu.VMEM((tm, D_MODEL), BF16)]
    if dil > 1:
        scratch.append(pltpu.VMEM((D_MODEL // LANES, tm, LANES), F32))
    return pl.pallas_call(
        kern,
        grid=(rows // tm, ncol),
        in_specs=[
            pl.BlockSpec((tm, D_MODEL), lambda i, j: (i, 0)),
            pl.BlockSpec((1, D_MODEL), lambda i, j: (0, 0)),
            wcol(0), wcol(1), wcol(2),
            pl.BlockSpec((tm, LANES), lambda i, j: (i % tiles_per_seq, 0)),
            pl.BlockSpec((tm, LANES), lambda i, j: (i % tiles_per_seq, 0)),
        ],
        out_specs=[out_spec] * 3,
        out_shape=[out_sds] * 3,
        scratch_shapes=scratch,
        compiler_params=_cparams("parallel", "arbitrary"),
        name=f"att_in_d{dil}",
    )(x2, g, w_bf, w_bf, w_bf, cos_t, sin_t)


def _gate_in_kernel(x_ref, g_ref, w_ref, o_ref):
    x = x_ref[...]
    h = (x * _rms_scale(x) * g_ref[...]).astype(BF16)
    o_ref[...] = jnp.dot(h, w_ref[...], preferred_element_type=F32).astype(BF16)


def _gate_in(x2, g, w_bf, layer):
    rows = x2.shape[0]
    tm = 512
    col0 = 3 * len(ATT_GROUPS)
    return pl.pallas_call(
        _gate_in_kernel,
        grid=(rows // tm,),
        in_specs=[
            pl.BlockSpec((tm, D_MODEL), lambda i: (i, 0)),
            pl.BlockSpec((1, D_MODEL), lambda i: (0, 0)),
            pl.BlockSpec((None, D_MODEL, D_MODEL), lambda i: (layer, 0, col0)),
        ],
        out_specs=pl.BlockSpec((tm, D_MODEL), lambda i: (i, 0)),
        out_shape=jax.ShapeDtypeStruct((rows, D_MODEL), BF16),
        compiler_params=_cparams("parallel"),
        name="att_gate_in",
    )(x2, g, w_bf)


def _attn_kernel(q_ref, kp_ref, kc_ref, kn_ref, vp_ref, vc_ref, vn_ref, o_ref, lse_ref, *, bq, sb, sub_len, nh):
    i = pl.program_id(3)
    nk = sb + 2 * ATT_HALF
    nsub = bq // sb
    ii = lax.broadcasted_iota(jnp.int32, (sb, nk), 0)
    jj = lax.broadcasted_iota(jnp.int32, (sb, nk), 1)
    rel = jj - ii
    in_band = (rel >= 0) & (rel <= 2 * ATT_HALF)
    valid = []
    for u in range(nsub):
        kpos = i * bq + u * sb - ATT_HALF + jj
        valid.append(in_band & (kpos >= 0) & (kpos < sub_len))
    lane = lax.broadcasted_iota(jnp.int32, (sb, LANES), 1)
    units = [(hh, u) for hh in range(nh) for u in range(nsub)]

    def scores(hh, u):
        sl = slice(hh * LANES, (hh + 1) * LANES)
        kcat = jnp.concatenate([kp_ref[:, sl], kc_ref[:, sl], kn_ref[:, sl]], axis=0)
        s = lax.dot_general(q_ref[u * sb:(u + 1) * sb, sl], kcat[u * sb:u * sb + nk], NT_DIMS,
                            preferred_element_type=F32)
        return jnp.where(valid[u], s, NEG_INF)

    def finish(hh, u, s, lse_acc):
        sl = slice(hh * LANES, (hh + 1) * LANES)
        vcat = jnp.concatenate([vp_ref[:, sl], vc_ref[:, sl], vn_ref[:, sl]], axis=0)
        m = jnp.max(s, axis=-1, keepdims=True)
        p = jnp.exp(s - m)
        l = jnp.sum(p, axis=-1, keepdims=True)
        o = jnp.dot(p.astype(BF16), vcat[u * sb:u * sb + nk], preferred_element_type=F32)
        o_ref[u * sb:(u + 1) * sb, sl] = (o / l).astype(BF16)
        lse_acc[u] = jnp.where(lane == hh, m + jnp.log(l), lse_acc[u])

    lse_acc = [jnp.zeros((sb, LANES), F32) for _ in range(nsub)]
    s_prev = scores(*units[0])
    for k in range(1, len(units)):
        s_next = scores(*units[k])
        finish(*units[k - 1], s_prev, lse_acc)
        s_prev = s_next
    finish(*units[-1], s_prev, lse_acc)
    for u in range(nsub):
        lse_ref[u * sb:(u + 1) * sb, :] = lse_acc[u]


def _attn_group(q, k, v, dil, batch, seq):
    sub_len = seq // dil
    bq = min(256, sub_len)
    sb = min(128, bq)
    hw = D_MODEL
    nh = hw // LANES
    hblocks = D_MODEL // hw
    nqb = sub_len // bq
    halo_per_q = bq // ATT_HALF
    n_halo = sub_len // ATT_HALF

    cur = pl.BlockSpec((None, bq, hw), lambda b, r, h, i: (r, b * nqb + i, h))
    prev = pl.BlockSpec((None, ATT_HALF, hw),
                        lambda b, r, h, i: (r, b * n_halo + jnp.maximum(i * halo_per_q - 1, 0), h))
    nxt = pl.BlockSpec((None, ATT_HALF, hw),
                       lambda b, r, h, i: (r, b * n_halo + jnp.minimum((i + 1) * halo_per_q, n_halo - 1), h))
    lse_spec = pl.BlockSpec((None, bq, LANES), lambda b, r, h, i: (r, b * nqb + i, h))
    rows = batch * sub_len
    kern = functools.partial(_attn_kernel, bq=bq, sb=sb, sub_len=sub_len, nh=nh)
    return pl.pallas_call(
        kern,
        grid=(batch, dil, hblocks, nqb),
        in_specs=[cur, prev, cur, nxt, prev, cur, nxt],
        out_specs=[cur, lse_spec],
        out_shape=[jax.ShapeDtypeStruct((dil, rows, D_MODEL), BF16),
                   jax.ShapeDtypeStruct((dil, rows, hblocks * LANES), F32)],
        compiler_params=_cparams("parallel", "parallel", "parallel", "arbitrary"),
        name=f"attn_d{dil}",
    )(q, k, k, k, v, v, v)


def _project_parts(prologue, nparts, part_rows, y_ref, w_ref, g_ref, x_ref, o_ref):
    ncb = D_MODEL // LANES
    cb_per_k = MXU_COLS // LANES

    def finish(out, p):
        rows = slice(p * part_rows, (p + 1) * part_rows)
        o_ref[rows, :] = x_ref[rows, :] + out * _rms_scale(out) * g_ref[...]

    for cb in range(ncb):
        prologue(0, cb)
    for p in range(nparts):
        acc = None
        for kc in range(D_MODEL // MXU_COLS):
            ks = slice(kc * MXU_COLS, (kc + 1) * MXU_COLS)
            part = jnp.dot(y_ref[p, :, ks], w_ref[ks, :], preferred_element_type=F32)
            acc = part if acc is None else acc + part
            if p + 1 < nparts:
                for cb in range(kc * cb_per_k, (kc + 1) * cb_per_k):
                    prologue(p + 1, cb)
        finish(acc, p)


def _att_out_kernel(o0_ref, o1_ref, o2_ref, l0_ref, l1_ref, l2_ref, z_ref, x_ref, w_ref, g_ref, o_ref,
                    so1, so2, sl1, sl2, y_ref, *, tm, dils, heads_per_block):
    nlb = l0_ref.shape[-1] // LANES
    for src, dst, d in ((l1_ref, sl1, dils[1]), (l2_ref, sl2, dils[2])):
        for lb in range(nlb):
            for r in range(d):
                dst[lb, pl.ds(r, tm // d, stride=d), :] = src[r, :, lb * LANES:(lb + 1) * LANES]
    wts = []
    for lb in range(nlb):
        l0, l1, l2 = l0_ref[0, :, lb * LANES:(lb + 1) * LANES], sl1[lb], sl2[lb]
        m = jnp.maximum(jnp.maximum(l0, l1), l2)
        e0, e1, e2 = jnp.exp(l0 - m), jnp.exp(l1 - m), jnp.exp(l2 - m)
        inv = 1.0 / (e0 + e1 + e2)
        wts.append((e0 * inv, e1 * inv, e2 * inv))
    for cb in range(D_MODEL // LANES):
        sl = slice(cb * LANES, (cb + 1) * LANES)
        for src, dst, d in ((o1_ref, so1, dils[1]), (o2_ref, so2, dils[2])):
            for r in range(d):
                dst[cb, pl.ds(r, tm // d, stride=d), :] = src[r, :, sl].astype(F32)
    part_rows = tm // OUT_PARTS

    def combine(p, cb):
        rows = slice(p * part_rows, (p + 1) * part_rows)
        sl = slice(cb * LANES, (cb + 1) * LANES)
        w0, w1, w2 = wts[cb // heads_per_block]
        hl = cb % heads_per_block
        bc = lambda w: jnp.broadcast_to(w[rows, hl:hl + 1], (part_rows, LANES))
        o = (bc(w0) * o0_ref[0, rows, sl].astype(F32) + bc(w1) * so1[cb, rows, :]
             + bc(w2) * so2[cb, rows, :])
        z = z_ref[rows, sl].astype(F32)
        y_ref[p, :, sl] = (o * (z * _sigmoid(z))).astype(BF16)

    _project_parts(combine, OUT_PARTS, part_rows, y_ref, w_ref, g_ref, x_ref, o_ref)


def _att_out(outs, lses, z, x2, w_bf, layer, g):
    rows = x2.shape[0]
    tm = 256
    dils = tuple(d for _, d in ATT_GROUPS)
    lse_w = lses[0].shape[-1]
    nlb = lse_w // LANES
    row_spec = pl.BlockSpec((tm, D_MODEL), lambda i: (i, 0))

    def res_spec(d, width):
        return pl.BlockSpec((d, tm // d, width), lambda i: (0, i, 0))

    kern = functools.partial(_att_out_kernel, tm=tm, dils=dils, heads_per_block=D_MODEL // LANES // nlb)
    return pl.pallas_call(
        kern,
        grid=(rows // tm,),
        in_specs=[res_spec(d, D_MODEL) for d in dils] + [res_spec(d, lse_w) for d in dils] + [
            row_spec, row_spec,
            pl.BlockSpec((None, D_MODEL, D_MODEL), lambda i: (layer, 0, 0)),
            pl.BlockSpec((1, D_MODEL), lambda i: (0, 0)),
        ],
        out_specs=row_spec,
        out_shape=jax.ShapeDtypeStruct((rows, D_MODEL), F32),
        scratch_shapes=[pltpu.VMEM((D_MODEL // LANES, tm, LANES), F32)] * 2
        + [pltpu.VMEM((nlb, tm, LANES), F32)] * 2 + [pltpu.VMEM((OUT_PARTS, tm // OUT_PARTS, D_MODEL), BF16)],
        compiler_params=_cparams("parallel"),
        name="att_out",
    )(*outs, *lses, z, x2, w_bf, g)


def _chunk_tri(n, reverse):
    t = lax.broadcasted_iota(jnp.int32, (n, n), 0)
    s = lax.broadcasted_iota(jnp.int32, (n, n), 1)
    same = (t // CHUNK) == (s // CHUNK)
    order = (s >= t) if reverse else (s <= t)
    return jnp.where(same & order, 1.0, 0.0).astype(BF16)


def _rwkv_in_kernel(*refs, tm, tn, tiles_per_seq, has_vres):
    (x_ref, xp_ref, xn_ref, g_ref, mup_ref, mun_ref,
     wr_ref, wk_ref, wv_ref, wz_ref,
     w1_ref, w2_ref, w0_ref, a1_ref, a2_ref, a0_ref,
     kk_ref, ka_ref, rk_ref) = refs[:19]
    pos = 19
    if has_vres:
        v1_ref, v2_ref, v0_ref, vf_ref = refs[pos:pos + 4]
        pos += 4
    (kkq_out, rq_out, ad_out, kdk_out, kend_out, aend_out, est_out, v_out, bonus_out, sz_out) = refs[pos:pos + 10]
    pos += 10
    xs_ref, hw_ref, ha_ref = refs[pos:pos + 3]
    hv_ref = refs[pos + 3] if has_vres else None

    i = pl.program_id(0)
    j = pl.program_id(1)
    mix_slot = {0: 0, 2: 1, 3: 2, 5: 3}

    @pl.when(j == 0)
    def _():
        t_in_seq = i % tiles_per_seq
        keep_prev = jnp.where(t_in_seq == 0, 0.0, 1.0).astype(F32)
        keep_next = jnp.where(t_in_seq == tiles_per_seq - 1, 0.0, 1.0).astype(F32)
        sx = _rms_scale(x_ref[...])
        xp = xp_ref[7:8, :]
        xn = xn_ref[0:1, :]
        sp = _rms_scale(xp) * keep_prev
        sn = _rms_scale(xn) * keep_next
        cw = 512
        row = lax.broadcasted_iota(jnp.int32, (tm, cw), 0)
        hw_acc = [jnp.zeros((tm, LORA_PAD), F32) for _ in range(2)]
        ha_acc = [jnp.zeros((tm, LORA_PAD), F32) for _ in range(2)]
        hv_acc = jnp.zeros((tm, LORA_PAD), F32)
        for cb in range(D_MODEL // cw):
            sl = slice(cb * cw, (cb + 1) * cw)
            g = g_ref[:, sl]
            h = x_ref[:, sl] * sx * g
            hp_row = xp[:, sl] * sp * g
            hn_row = xn[:, sl] * sn * g
            h_prev = jnp.where(row == 0, hp_row, pltpu.roll(h, 1, axis=0))
            h_next = jnp.where(row == tm - 1, hn_row, pltpu.roll(h, tm - 1, axis=0))
            h16 = h.astype(BF16)
            dp = (h_prev - h).astype(BF16)
            dn = (h_next - h).astype(BF16)
            mixes = {}
            for t in range(6):
                mixes[t] = h16 + dp * mup_ref[t:t + 1, sl].astype(BF16) + dn * mun_ref[t:t + 1, sl].astype(BF16)
                if t in mix_slot:
                    xs_ref[mix_slot[t], :, sl] = mixes[t]
            for c in range(2):
                hw_acc[c] = hw_acc[c] + jnp.dot(mixes[1], w1_ref[c, sl, :], preferred_element_type=F32)
                ha_acc[c] = ha_acc[c] + jnp.dot(mixes[4], a1_ref[c, sl, :], preferred_element_type=F32)
            if has_vres:
                hv_acc = hv_acc + jnp.dot(mixes[3], v1_ref[sl, :], preferred_element_type=F32)
        for c in range(2):
            hw_ref[c] = jnp.tanh(hw_acc[c]).astype(BF16)
            ha_ref[c] = ha_acc[c].astype(BF16)
        if has_vres:
            hv_ref[...] = hv_acc.astype(BF16)

    wl = [w0_ref[c:c + 1, :] + jnp.dot(hw_ref[c], w2_ref[c], preferred_element_type=F32) for c in range(2)]
    al = [a0_ref[c:c + 1, :] + jnp.dot(ha_ref[c], a2_ref[c], preferred_element_type=F32) for c in range(2)]
    if has_vres:
        gl = v0_ref[...] + jnp.dot(hv_ref[...], v2_ref[...], preferred_element_type=F32)
    k = jnp.dot(xs_ref[1], wk_ref[...], preferred_element_type=F32)
    r = jnp.dot(xs_ref[0], wr_ref[...], preferred_element_type=F32)
    z = jnp.dot(xs_ref[3], wz_ref[...], preferred_element_type=F32)
    v = jnp.dot(xs_ref[2], wv_ref[...], preferred_element_type=F32)
    if has_vres:
        v = v + (vf_ref[...] - v) * _sigmoid(gl)
    gmat = _group_matrix(tn)
    kk = k * kk_ref[...]
    kk = kk * lax.rsqrt(jnp.maximum(_group_sum(kk * kk, gmat), 1e-24))
    k_a = ka_ref[...]
    rk_acc = jnp.zeros((tm, tn), F32)
    nchunk = tm // CHUNK
    half = 256
    for c in range(2):
        lw = -math.exp(-0.5) * _sigmoid(wl[c])
        a = _sigmoid(al[c])
        kd = k * (1.0 + (a - 1.0) * k_a)
        rk_acc = rk_acc + r * kd * rk_ref[c:c + 1, :]
        tri = _chunk_tri(half, reverse=(c == 1))
        g = jnp.concatenate([_split_dot(tri, lw[hh * half:(hh + 1) * half]) for hh in range(tm // half)], axis=0)
        g3 = g.reshape(nchunk, CHUNK, tn)
        last = 0 if c == 1 else CHUNK - 1
        g_tot = g3[:, last:last + 1, :]
        e_tot = jnp.exp(g_tot)
        est_out[c] = e_tot.reshape(nchunk, tn)
        e_q = jnp.exp(g)
        e_qp = jnp.exp(g - lw)
        e_k = jnp.exp(-g)
        e_end = (e_k.reshape(nchunk, CHUNK, tn) * e_tot).reshape(tm, tn)
        kka = kk * a
        kkq_out[c] = (kk * e_qp).astype(BF16)
        rq_out[c] = (r * e_q).astype(BF16)
        ad_out[c] = (kka * e_k).astype(BF16)
        kdk_out[c] = (kd * e_k).astype(BF16)
        kend_out[c] = (kd * e_end).astype(BF16)
        aend_out[c] = (kka * e_end).astype(BF16)
    v_out[...] = v
    bonus_out[...] = _group_sum(rk_acc, gmat) * v
    sz_out[...] = (z * _sigmoid(z)).astype(BF16)


def _pad_lora(w1, w2):
    rank = w1.shape[-1]
    pad1 = [(0, 0)] * (w1.ndim - 1) + [(0, LORA_PAD - rank)]
    pad2 = [(0, 0)] * (w2.ndim - 2) + [(0, LORA_PAD - rank), (0, 0)]
    return jnp.pad(w1, pad1).astype(BF16), jnp.pad(w2, pad2).astype(BF16)


def _rwkv_in(x2, seq, g, mu_prev, mu_next, w_in_bf, layer, w0, w1, w2, a0, a1, a2, k_k, k_a, r_k, vres, v_first):
    rows = x2.shape[0]
    tm, tn = 512, 256
    tiles_per_seq = seq // tm
    ncol = D_MODEL // tn
    has_vres = vres is not None
    w1p, w2p = _pad_lora(w1, w2)
    a1p, a2p = _pad_lora(a1, a2)
    sub = tm // 8
    nsub = rows // 8

    def const2(shape):
        return pl.BlockSpec(shape, lambda i, j: (0, 0))

    def col2(nrow):
        return pl.BlockSpec((nrow, tn), lambda i, j: (0, j))

    def wcol(gi):
        return pl.BlockSpec((None, D_MODEL, tn), lambda i, j: (layer, 0, gi * ncol + j))

    in_specs = [
        pl.BlockSpec((tm, D_MODEL), lambda i, j: (i, 0)),
        pl.BlockSpec((8, D_MODEL), lambda i, j: (jnp.maximum(i * sub - 1, 0), 0)),
        pl.BlockSpec((8, D_MODEL), lambda i, j: (jnp.minimum((i + 1) * sub, nsub - 1), 0)),
        const2((1, D_MODEL)), const2((6, D_MODEL)), const2((6, D_MODEL)),
        wcol(0), wcol(1), wcol(2), wcol(3),
        pl.BlockSpec((2, D_MODEL, LORA_PAD), lambda i, j: (0, 0, 0)),
        pl.BlockSpec((2, LORA_PAD, tn), lambda i, j: (0, 0, j)),
        col2(2),
        pl.BlockSpec((2, D_MODEL, LORA_PAD), lambda i, j: (0, 0, 0)),
        pl.BlockSpec((2, LORA_PAD, tn), lambda i, j: (0, 0, j)),
        col2(2),
        col2(1), col2(1), col2(2),
    ]
    args = [x2, x2, x2, g, mu_prev, mu_next, w_in_bf, w_in_bf, w_in_bf, w_in_bf,
            w1p, w2p, w0, a1p, a2p, a0, k_k, k_a, r_k]
    if has_vres:
        v0, v1, v2 = vres
        v1p, v2p = _pad_lora(v1, v2)
        in_specs += [const2((D_MODEL, LORA_PAD)), col2(LORA_PAD), col2(1),
                     pl.BlockSpec((tm, tn), lambda i, j: (i, j))]
        args += [v1p, v2p, v0, v_first]

    tile = pl.BlockSpec((tm, tn), lambda i, j: (i, j))
    tile2 = pl.BlockSpec((2, tm, tn), lambda i, j: (0, i, j))
    est_spec = pl.BlockSpec((2, tm // CHUNK, tn), lambda i, j: (0, i, j))
    sds_bf2 = jax.ShapeDtypeStruct((2, rows, D_MODEL), BF16)
    sds = jax.ShapeDtypeStruct((rows, D_MODEL), F32)
    scratch = [pltpu.VMEM((4, tm, D_MODEL), BF16), pltpu.VMEM((2, tm, LORA_PAD), BF16),
               pltpu.VMEM((2, tm, LORA_PAD), BF16)]
    if has_vres:
        scratch.append(pltpu.VMEM((tm, LORA_PAD), BF16))
    kern = functools.partial(_rwkv_in_kernel, tm=tm, tn=tn, tiles_per_seq=tiles_per_seq, has_vres=has_vres)
    return pl.pallas_call(
        kern,
        grid=(rows // tm, ncol),
        in_specs=in_specs,
        out_specs=[tile2] * 6 + [est_spec, tile, tile, tile],
        out_shape=[sds_bf2] * 6 + [jax.ShapeDtypeStruct((2, rows // CHUNK, D_MODEL), F32), sds, sds,
                                   jax.ShapeDtypeStruct((rows, D_MODEL), BF16)],
        scratch_shapes=scratch,
        compiler_params=_cparams("parallel", "arbitrary"),
        name="rwkv_in_vres" if has_vres else "rwkv_in",
    )(*args)


def _rwkv_scan_kernel(kkq_ref, rq_ref, ad_ref, kdk_ref, kend_ref, aend_ref, v_ref, est_ref, y_ref, state_ref,
                      *, tb, npair):
    rev = pl.program_id(1) == 1
    step = pl.program_id(3)
    nchunk = tb // CHUNK
    pairs = range(npair)

    @pl.when(step == 0)
    def _():
        state_ref[...] = jnp.zeros_like(state_ref)

    t = lax.broadcasted_iota(jnp.int32, (CHUNK, LANES), 0)
    lane = lax.broadcasted_iota(jnp.int32, (CHUNK, LANES), 1)
    s = lane & (CHUNK - 1)
    d = jnp.where(rev, s - t, t - s)
    strict = d > 0
    incl = d >= 0
    eye = jnp.where(d == 0, 1.0, 0.0).astype(F32)
    first_half = lane < CHUNK
    half_a = jnp.where(first_half, 1.0, 0.0).astype(BF16)
    half_b = jnp.where(first_half, 0.0, 1.0).astype(BF16)
    rr = lax.broadcasted_iota(jnp.int32, (LANES, LANES), 0)
    cc = lax.broadcasted_iota(jnp.int32, (LANES, LANES), 1)
    bd_mask = (rr < CHUNK) == (cc < CHUNK)
    est_rows = lax.broadcasted_iota(jnp.int32, (nchunk, npair * LANES), 0)

    def bd16(y):
        return jnp.concatenate([y * half_a, y * half_b], axis=0)

    def bd32(y):
        return jnp.concatenate([jnp.where(first_half, y, 0.0), jnp.where(first_half, 0.0, y)],
                               axis=0).astype(BF16)

    def mm(a, b):
        return jnp.dot(a, b, preferred_element_type=F32)

    def body(ci, carry):
        first = ci * CHUNKS_PER_TRIP
        cs = [jnp.where(rev, nchunk - 1 - (first + u), first + u) for u in range(CHUNKS_PER_TRIP)]
        jobs = [(pl.ds(pl.multiple_of(c * CHUNK, CHUNK), CHUNK), slice(p * LANES, (p + 1) * LANES))
                for c in cs for p in pairs]
        n = range(len(jobs))
        kkq = [kkq_ref[rw, ln] for rw, ln in jobs]
        rq = [rq_ref[rw, ln] for rw, ln in jobs]
        v = [v_ref[rw, ln] for rw, ln in jobs]
        gram = [lax.dot_general(
            jnp.concatenate([kkq[i], rq[i]], axis=0),
            jnp.concatenate([bd16(ad_ref[jobs[i]]), bd16(kdk_ref[jobs[i]])], axis=0),
            NT_DIMS, preferred_element_type=F32) for i in n]
        a_mat = [jnp.where(strict, gram[i][0:CHUNK, 0:LANES], 0.0) for i in n]
        ak = [jnp.where(strict, gram[i][0:CHUNK, LANES:], 0.0) for i in n]
        bra = [jnp.where(incl, gram[i][CHUNK:, 0:LANES], 0.0).astype(BF16) for i in n]
        brk = [jnp.where(incl, gram[i][CHUNK:, LANES:], 0.0) for i in n]
        early = [mm(jnp.concatenate([ak[i], brk[i]], axis=0).astype(BF16), bd32(v[i])) for i in n]
        a_pow = [mm(a_mat[i].astype(BF16), bd32(a_mat[i])) for i in n]
        inv = [eye - a_mat[i] for i in n]
        for _ in range(4):
            st = [mm(jnp.concatenate([inv[i], a_pow[i]], axis=0).astype(BF16), bd32(a_pow[i])) for i in n]
            inv = [inv[i] + st[i][0:CHUNK] for i in n]
            a_pow = [st[i][CHUNK:] for i in n]
        inv = [inv[i] + mm(inv[i].astype(BF16), bd32(a_pow[i])) for i in n]
        wu = [mm(inv[i].astype(BF16), jnp.concatenate([bd16(kkq[i]), bd32(early[i][0:CHUNK])], axis=1))
              for i in n]
        ry = [mm(bra[i], jnp.concatenate([bd32(wu[i][:, 0:LANES]), bd32(wu[i][:, LANES:])], axis=1))
              for i in n]
        aend = [aend_ref[jobs[i]] for i in n]
        xraw = [lax.dot_general(wu[i][:, 0:LANES].astype(BF16), aend[i], TN_DIMS, preferred_element_type=F32)
                for i in n]
        nraw = [lax.dot_general(jnp.concatenate([v[i], -wu[i][:, LANES:]], axis=0).astype(BF16),
                                jnp.concatenate([kend_ref[jobs[i]], aend[i]], axis=0),
                                TN_DIMS, preferred_element_type=F32) for i in n]
        rqp = [(rq[i].astype(F32) - ry[i][:, 0:LANES]).astype(BF16) for i in n]
        yv = [early[i][CHUNK:] - ry[i][:, LANES:] for i in n]
        xmat = [jnp.where(bd_mask, xraw[i], 0.0).astype(BF16) for i in n]
        nmat = [jnp.where(bd_mask, nraw[i], 0.0) for i in n]
        est_all = est_ref[...]
        state = [state_ref[p] for p in pairs]
        for u in range(CHUNKS_PER_TRIP):
            est_row = jnp.sum(jnp.where(est_rows == cs[u], est_all, 0.0), axis=0, keepdims=True)
            idx = [u * npair + p for p in pairs]
            s16 = [state[p].astype(BF16) for p in pairs]
            ys = [yv[idx[p]] + lax.dot_general(rqp[idx[p]], s16[p], NT_DIMS, preferred_element_type=F32)
                  for p in pairs]
            sx = [mm(s16[p], xmat[idx[p]]) for p in pairs]
            for p in pairs:
                rw, ln = jobs[idx[p]]
                y_ref[rw, ln] = ys[p]
                state[p] = state[p] * est_row[:, ln] - sx[p] + nmat[idx[p]]
        for p in pairs:
            state_ref[p] = state[p]
        return carry

    lax.fori_loop(0, nchunk // CHUNKS_PER_TRIP, body, 0)


def _rwkv_scan(kkq, rq, ad, kdk, kend, aend, est, v, batch, seq):
    tb, lw_lanes = 512, 1024
    nt = seq // tb
    npair = lw_lanes // LANES

    def tmap(i, dr):
        return i + dr * (nt - 1 - 2 * i)

    v_spec = pl.BlockSpec((tb, lw_lanes), lambda b, dr, h, i: (b * nt + tmap(i, dr), h))
    d_spec = pl.BlockSpec((None, tb, lw_lanes), lambda b, dr, h, i: (dr, b * nt + tmap(i, dr), h))
    e_spec = pl.BlockSpec((None, tb // CHUNK, lw_lanes), lambda b, dr, h, i: (dr, b * nt + tmap(i, dr), h))
    kern = functools.partial(_rwkv_scan_kernel, tb=tb, npair=npair)
    return pl.pallas_call(
        kern,
        grid=(batch, 2, D_MODEL // lw_lanes, nt),
        in_specs=[d_spec] * 6 + [v_spec, e_spec],
        out_specs=d_spec,
        out_shape=jax.ShapeDtypeStruct((2, batch * seq, D_MODEL), F32),
        scratch_shapes=[pltpu.VMEM((npair, LANES, LANES), F32)],
        compiler_params=_cparams("parallel", "parallel", "parallel", "arbitrary"),
        name="rwkv_scan",
    )(kkq, rq, ad, kdk, kend, aend, v, est)


def _rwkv_out_kernel(y_ref, bonus_ref, sz_ref, gg_ref, gb_ref, x_ref, w_ref, g_ref, o_ref, yb_ref, *, tm):
    gmat = _group_matrix(LANES)
    inv_n = 1.0 / RWKV_HEAD_DIM
    part_rows = tm // OUT_PARTS

    def groupnorm(p, cb):
        rows = slice(p * part_rows, (p + 1) * part_rows)
        sl = slice(cb * LANES, (cb + 1) * LANES)
        y = y_ref[0, rows, sl] + y_ref[1, rows, sl]
        mu = _group_sum(y, gmat) * inv_n
        yc = y - mu
        var = _group_sum(yc * yc, gmat) * inv_n
        yn = yc * lax.rsqrt(var + GN_EPS) * gg_ref[:, sl] + gb_ref[:, sl]
        yb_ref[p, :, sl] = ((yn + bonus_ref[rows, sl]) * sz_ref[rows, sl].astype(F32)).astype(BF16)

    _project_parts(groupnorm, OUT_PARTS, part_rows, yb_ref, w_ref, g_ref, x_ref, o_ref)


def _rwkv_out(y2, bonus, sz, gn_g, gn_b, x2, w_bf, layer, g):
    rows = x2.shape[0]
    tm = 256
    row_spec = pl.BlockSpec((tm, D_MODEL), lambda i: (i, 0))
    vec = pl.BlockSpec((1, D_MODEL), lambda i: (0, 0))
    return pl.pallas_call(
        functools.partial(_rwkv_out_kernel, tm=tm),
        grid=(rows // tm,),
        in_specs=[pl.BlockSpec((2, tm, D_MODEL), lambda i: (0, i, 0)), row_spec, row_spec, vec, vec, row_spec,
                  pl.BlockSpec((None, D_MODEL, D_MODEL), lambda i: (layer, 0, 0)), vec],
        out_specs=row_spec,
        out_shape=jax.ShapeDtypeStruct((rows, D_MODEL), F32),
        scratch_shapes=[pltpu.VMEM((OUT_PARTS, tm // OUT_PARTS, D_MODEL), BF16)],
        compiler_params=_cparams("parallel"),
        name="rwkv_out",
    )(y2, bonus, sz, gn_g, gn_b, x2, w_bf, g)


def _trunk(x, p, rope):
    batch, seq, _ = x.shape
    x2 = x.reshape(batch * seq, D_MODEL)
    v_first = None
    depth = p["norm_pre"].shape[0]
    for layer in range(depth):
        j = layer // 2
        g_pre = p["norm_pre"][layer][None, :]
        g_post = p["norm_post"][layer][None, :]
        if layer % 2 == 0:
            w_in = p["att_w_in"]
            outs, lses = [], []
            for gidx, (_, dil) in enumerate(ATT_GROUPS):
                q, k, v = _att_in(x2, g_pre, w_in, j, *rope[gidx], seq, gidx, dil)
                o, lse = _attn_group(q, k, v, dil, batch, seq)
                outs.append(o)
                lses.append(lse)
            z = _gate_in(x2, g_pre, w_in, j)
            x2 = _att_out(outs, lses, z, x2, p["att_w_out"], j, g_post)
        else:
            vres = None if j == 0 else (p["rwkv_v0"][j - 1][None, :], p["rwkv_v1"][j - 1], p["rwkv_v2"][j - 1])
            kkq, rq, ad, kdk, kend, aend, est, v, bonus, sz = _rwkv_in(
                x2, seq, g_pre, p["rwkv_mu_prev"][j], p["rwkv_mu_next"][j], p["rwkv_w_in"], j,
                p["rwkv_w0"][j], p["rwkv_w1"][j], p["rwkv_w2"][j],
                p["rwkv_a0"][j], p["rwkv_a1"][j], p["rwkv_a2"][j],
                p["rwkv_k_k"][j][None, :], p["rwkv_k_a"][j][None, :], p["rwkv_r_k"][j].reshape(2, D_MODEL),
                vres, v_first)
            if j == 0:
                v_first = v
            y2 = _rwkv_scan(kkq, rq, ad, kdk, kend, aend, est, v, batch, seq)
            x2 = _rwkv_out(y2, bonus, sz, p["rwkv_gn_g"][j][None, :], p["rwkv_gn_b"][j][None, :], x2,
                           p["rwkv_w_out"], j, g_post)
    return x2.reshape(batch, seq, D_MODEL)


def kernel(x_prompt, x_sample, norm_pre, norm_post, att_w_in, att_w_out, rwkv_mu_prev, rwkv_mu_next, rwkv_w_in, rwkv_w0, rwkv_w1, rwkv_w2, rwkv_a0, rwkv_a1, rwkv_a2, rwkv_v0, rwkv_v1, rwkv_v2, rwkv_k_k, rwkv_k_a, rwkv_r_k, rwkv_gn_g, rwkv_gn_b, rwkv_w_out):
    p = dict(
        norm_pre=norm_pre, norm_post=norm_post,
        att_w_in=att_w_in.astype(BF16), att_w_out=att_w_out.astype(BF16),
        rwkv_mu_prev=rwkv_mu_prev, rwkv_mu_next=rwkv_mu_next, rwkv_w_in=rwkv_w_in.astype(BF16),
        rwkv_w0=rwkv_w0, rwkv_w1=rwkv_w1, rwkv_w2=rwkv_w2,
        rwkv_a0=rwkv_a0, rwkv_a1=rwkv_a1, rwkv_a2=rwkv_a2,
        rwkv_v0=rwkv_v0, rwkv_v1=rwkv_v1, rwkv_v2=rwkv_v2,
        rwkv_k_k=rwkv_k_k, rwkv_k_a=rwkv_k_a, rwkv_r_k=rwkv_r_k,
        rwkv_gn_g=rwkv_gn_g, rwkv_gn_b=rwkv_gn_b, rwkv_w_out=rwkv_w_out.astype(BF16),
    )
    max_seq = max(x_prompt.shape[1], x_sample.shape[1])
    rope = [_rope_tables(max_seq, dil) for _, dil in ATT_GROUPS]
    return (_trunk(x_prompt, p, rope), _trunk(x_sample, p, rope))
```

```python
import functools
import math

import jax
import jax.numpy as jnp
from jax import lax
from jax.experimental import pallas as pl
from jax.experimental.pallas import tpu as pltpu

F32 = jnp.float32
BF16 = jnp.bfloat16

D_MODEL = 2048
LANES = 128
MXU_COLS = 256
ATT_HEAD_DIM = 128
ATT_GROUPS = ((128, 1), (512, 4), (2048, 16))
ATT_HALF = 64
ROPE_THETA = 10000.0
RWKV_HEAD_DIM = 64
LORA_PAD = 128
RMS_EPS = 1e-6
GN_EPS = 64e-5
NEG_INF = -1e30
CHUNK = 64
ATT_IN_ROWS = 512
OUT_PARTS = 2
CHUNKS_PER_TRIP = 2
VMEM_LIMIT_BYTES = 56 * 1024 * 1024

NT_DIMS = (((1,), (1,)), ((), ()))
TN_DIMS = (((0,), (0,)), ((), ()))


def _cparams(*sem):
    return pltpu.CompilerParams(dimension_semantics=sem, vmem_limit_bytes=VMEM_LIMIT_BYTES)


def _rms_scale(x):
    return lax.rsqrt(jnp.mean(x * x, axis=-1, keepdims=True) + RMS_EPS)


def _sigmoid(x):
    return 0.5 * jnp.tanh(0.5 * x) + 0.5


def _split_dot(lhs_bf, x):
    hi = x.astype(BF16)
    lo = (x - hi.astype(F32)).astype(BF16)
    return (jnp.dot(lhs_bf, hi, preferred_element_type=F32)
            + jnp.dot(lhs_bf, lo, preferred_element_type=F32))


def _group_sum(x, gmat):
    hi = x.astype(BF16)
    lo = (x - hi.astype(F32)).astype(BF16)
    return (jnp.dot(hi, gmat, preferred_element_type=F32)
            + jnp.dot(lo, gmat, preferred_element_type=F32))


def _group_matrix(n):
    r = lax.broadcasted_iota(jnp.int32, (n, n), 0) // RWKV_HEAD_DIM
    c = lax.broadcasted_iota(jnp.int32, (n, n), 1) // RWKV_HEAD_DIM
    return jnp.where(r == c, 1.0, 0.0).astype(BF16)


def _rope_table_kernel(invf_ref, cos_ref, sin_ref, *, dil):
    rows = cos_ref.shape[0]
    base = pl.program_id(0) * rows
    local = lax.broadcasted_iota(jnp.int32, (rows, LANES), 0)
    per_res = rows // dil
    pos = (base + (local % per_res) * dil + local // per_res).astype(F32)
    ang = pos * invf_ref[...]
    lane = lax.broadcasted_iota(jnp.int32, (rows, LANES), 1)
    s = jnp.sin(ang)
    cos_ref[...] = jnp.cos(ang)
    sin_ref[...] = jnp.where(lane < ATT_HEAD_DIM // 2, -s, s)


def _rope_tables(seq, dil):
    half = ATT_HEAD_DIM // 2
    inv_freq = 1.0 / (ROPE_THETA ** (jnp.arange(half, dtype=F32) * 2.0 / ATT_HEAD_DIM))
    invf = jnp.concatenate([inv_freq, inv_freq])[None, :]
    rows = ATT_IN_ROWS
    return pl.pallas_call(
        functools.partial(_rope_table_kernel, dil=dil),
        grid=(seq // rows,),
        in_specs=[pl.BlockSpec((1, LANES), lambda i: (0, 0))],
        out_specs=[pl.BlockSpec((rows, LANES), lambda i: (i, 0))] * 2,
        out_shape=[jax.ShapeDtypeStruct((seq, LANES), F32)] * 2,
        compiler_params=_cparams("arbitrary"),
        name=f"rope_table_d{dil}",
    )(invf)


def _att_in_kernel(x_ref, g_ref, wq_ref, wk_ref, wv_ref, cos_ref, sin_ref, q_ref, k_ref, v_ref, h_ref, *hs_ref,
                   tm, tn, dil, scale):
    per_res = tm // dil

    @pl.when(pl.program_id(1) == 0)
    def _():
        x = x_ref[...]
        h = x * _rms_scale(x) * g_ref[...]
        if dil == 1:
            h_ref[...] = h.astype(BF16)
        else:
            for cb in range(D_MODEL // LANES):
                sl = slice(cb * LANES, (cb + 1) * LANES)
                hs_ref[0][cb] = h[:, sl]
                for r in range(dil):
                    h_ref[r * per_res:(r + 1) * per_res, sl] = (
                        hs_ref[0][cb, pl.ds(r, per_res, stride=dil), :].astype(BF16))

    h = h_ref[...]
    cos = cos_ref[...]
    sin = sin_ref[...]
    for w_ref, o_ref, rope, sc in ((wq_ref, q_ref, True, scale), (wk_ref, k_ref, True, None),
                                   (wv_ref, v_ref, False, None)):
        for cb in range(tn // MXU_COLS):
            acc = jnp.dot(h, w_ref[:, cb * MXU_COLS:(cb + 1) * MXU_COLS], preferred_element_type=F32)
            for hh in range(MXU_COLS // LANES):
                sl = slice(cb * MXU_COLS + hh * LANES, cb * MXU_COLS + (hh + 1) * LANES)
                t = acc[:, hh * LANES:(hh + 1) * LANES]
                if rope:
                    t = t * cos + pltpu.roll(t, ATT_HEAD_DIM // 2, axis=1) * sin
                if sc is not None:
                    t = t * sc
                t = t.astype(BF16)
                for r in range(dil):
                    o_ref[r, :, sl] = t[r * per_res:(r + 1) * per_res]


def _att_in(x2, g, w_bf, layer, cos_t, sin_t, seq, gidx, dil):
    rows = x2.shape[0]
    tm, tn = ATT_IN_ROWS, 512
    tiles_per_seq = seq // tm
    ncol = D_MODEL // tn
    kern = functools.partial(_att_in_kernel, tm=tm, tn=tn, dil=dil, scale=ATT_HEAD_DIM ** -0.5)

    def wcol(part):
        return pl.BlockSpec((None, D_MODEL, tn), lambda i, j: (layer, 0, (gidx * 3 + part) * ncol + j))

    out_spec = pl.BlockSpec((dil, tm // dil, tn), lambda i, j: (0, i, j))
    out_sds = jax.ShapeDtypeStruct((dil, rows // dil, D_MODEL), BF16)
    scratch = [pltpu.VMEM((tm, D_MODEL), BF16)]
    if dil > 1:
        scratch.append(pltpu.VMEM((D_MODEL // LANES, tm, LANES), F32))
    return pl.pallas_call(
        kern,
        grid=(rows // tm, ncol),
        in_specs=[
            pl.BlockSpec((tm, D_MODEL), lambda i, j: (i, 0)),
            pl.BlockSpec((1, D_MODEL), lambda i, j: (0, 0)),
            wcol(0), wcol(1), wcol(2),
            pl.BlockSpec((tm, LANES), lambda i, j: (i % tiles_per_seq, 0)),
            pl.BlockSpec((tm, LANES), lambda i, j: (i % tiles_per_seq, 0)),
        ],
        out_specs=[out_spec] * 3,
        out_shape=[out_sds] * 3,
        scratch_shapes=scratch,
        compiler_params=_cparams("parallel", "arbitrary"),
        name=f"att_in_d{dil}",
    )(x2, g, w_bf, w_bf, w_bf, cos_t, sin_t)


def _gate_in_kernel(x_ref, g_ref, w_ref, o_ref):
    x = x_ref[...]
    h = (x * _rms_scale(x) * g_ref[...]).astype(BF16)
    o_ref[...] = jnp.dot(h, w_ref[...], preferred_element_type=F32).astype(BF16)


def _gate_in(x2, g, w_bf, layer):
    rows = x2.shape[0]
    tm = 512
    col0 = 3 * len(ATT_GROUPS)
    return pl.pallas_call(
        _gate_in_kernel,
        grid=(rows // tm,),
        in_specs=[
            pl.BlockSpec((tm, D_MODEL), lambda i: (i, 0)),
            pl.BlockSpec((1, D_MODEL), lambda i: (0, 0)),
            pl.BlockSpec((None, D_MODEL, D_MODEL), lambda i: (layer, 0, col0)),
        ],
        out_specs=pl.BlockSpec((tm, D_MODEL), lambda i: (i, 0)),
        out_shape=jax.ShapeDtypeStruct((rows, D_MODEL), BF16),
        compiler_params=_cparams("parallel"),
        name="att_gate_in",
    )(x2, g, w_bf)


def _attn_kernel(q_ref, kp_ref, kc_ref, kn_ref, vp_ref, vc_ref, vn_ref, o_ref, lse_ref, *, bq, sb, sub_len, nh):
    i = pl.program_id(3)
    nk = sb + 2 * ATT_HALF
    nsub = bq // sb
    ii = lax.broadcasted_iota(jnp.int32, (sb, nk), 0)
    jj = lax.broadcasted_iota(jnp.int32, (sb, nk), 1)
    rel = jj - ii
    in_band = (rel >= 0) & (rel <= 2 * ATT_HALF)
    valid = []
    for u in range(nsub):
        kpos = i * bq + u * sb - ATT_HALF + jj
        valid.append(in_band & (kpos >= 0) & (kpos < sub_len))
    lane = lax.broadcasted_iota(jnp.int32, (sb, LANES), 1)
    units = [(hh, u) for hh in range(nh) for u in range(nsub)]

    def scores(hh, u):
        sl = slice(hh * LANES, (hh + 1) * LANES)
        kcat = jnp.concatenate([kp_ref[:, sl], kc_ref[:, sl], kn_ref[:, sl]], axis=0)
        s = lax.dot_general(q_ref[u * sb:(u + 1) * sb, sl], kcat[u * sb:u * sb + nk], NT_DIMS,
                            preferred_element_type=F32)
        return jnp.where(valid[u], s, NEG_INF)

    def finish(hh, u, s, lse_acc):
        sl = slice(hh * LANES, (hh + 1) * LANES)
        vcat = jnp.concatenate([vp_ref[:, sl], vc_ref[:, sl], vn_ref[:, sl]], axis=0)
        m = jnp.max(s, axis=-1, keepdims=True)
        p = jnp.exp(s - m)
        l = jnp.sum(p, axis=-1, keepdims=True)
        o = jnp.dot(p.astype(BF16), vcat[u * sb:u * sb + nk], preferred_element_type=F32)
        o_ref[u * sb:(u + 1) * sb, sl] = (o / l).astype(BF16)
        lse_acc[u] = jnp.where(lane == hh, m + jnp.log(l), lse_acc[u])

    lse_acc = [jnp.zeros((sb, LANES), F32) for _ in range(nsub)]
    s_prev = scores(*units[0])
    for k in range(1, len(units)):
        s_next = scores(*units[k])
        finish(*units[k - 1], s_prev, lse_acc)
        s_prev = s_next
    finish(*units[-1], s_prev, lse_acc)
    for u in range(nsub):
        lse_ref[u * sb:(u + 1) * sb, :] = lse_acc[u]


def _attn_group(q, k, v, dil, batch, seq):
    sub_len = seq // dil
    bq = min(256, sub_len)
    sb = min(128, bq)
    hw = D_MODEL
    nh = hw // LANES
    hblocks = D_MODEL // hw
    nqb = sub_len // bq
    halo_per_q = bq // ATT_HALF
    n_halo = sub_len // ATT_HALF

    cur = pl.BlockSpec((None, bq, hw), lambda b, r, h, i: (r, b * nqb + i, h))
    prev = pl.BlockSpec((None, ATT_HALF, hw),
                        lambda b, r, h, i: (r, b * n_halo + jnp.maximum(i * halo_per_q - 1, 0), h))
    nxt = pl.BlockSpec((None, ATT_HALF, hw),
                       lambda b, r, h, i: (r, b * n_halo + jnp.minimum((i + 1) * halo_per_q, n_halo - 1), h))
    lse_spec = pl.BlockSpec((None, bq, LANES), lambda b, r, h, i: (r, b * nqb + i, h))
    rows = batch * sub_len
    kern = functools.partial(_attn_kernel, bq=bq, sb=sb, sub_len=sub_len, nh=nh)
    return pl.pallas_call(
        kern,
        grid=(batch, dil, hblocks, nqb),
        in_specs=[cur, prev, cur, nxt, prev, cur, nxt],
        out_specs=[cur, lse_spec],
        out_shape=[jax.ShapeDtypeStruct((dil, rows, D_MODEL), BF16),
                   jax.ShapeDtypeStruct((dil, rows, hblocks * LANES), F32)],
        compiler_params=_cparams("parallel", "parallel", "parallel", "arbitrary"),
        name=f"attn_d{dil}",
    )(q, k, k, k, v, v, v)


def _project_parts(prologue, nparts, part_rows, y_ref, w_ref, g_ref, x_ref, o_ref):
    ncb = D_MODEL // LANES
    cb_per_k = MXU_COLS // LANES

    def finish(out, p):
        rows = slice(p * part_rows, (p + 1) * part_rows)
        o_ref[rows, :] = x_ref[rows, :] + out * _rms_scale(out) * g_ref[...]

    for cb in range(ncb):
        prologue(0, cb)
    for p in range(nparts):
        acc = None
        for kc in range(D_MODEL // MXU_COLS):
            ks = slice(kc * MXU_COLS, (kc + 1) * MXU_COLS)
            part = jnp.dot(y_ref[p, :, ks], w_ref[ks, :], preferred_element_type=F32)
            acc = part if acc is None else acc + part
            if p + 1 < nparts:
                for cb in range(kc * cb_per_k, (kc + 1) * cb_per_k):
                    prologue(p + 1, cb)
        finish(acc, p)


def _att_out_kernel(o0_ref, o1_ref, o2_ref, l0_ref, l1_ref, l2_ref, z_ref, x_ref, w_ref, g_ref, o_ref,
                    so1, so2, sl1, sl2, y_ref, *, tm, dils, heads_per_block):
    nlb = l0_ref.shape[-1] // LANES
    for src, dst, d in ((l1_ref, sl1, dils[1]), (l2_ref, sl2, dils[2])):
        for lb in range(nlb):
            for r in range(d):
                dst[lb, pl.ds(r, tm // d, stride=d), :] = src[r, :, lb * LANES:(lb + 1) * LANES]
    wts = []
    for lb in range(nlb):
        l0, l1, l2 = l0_ref[0, :, lb * LANES:(lb + 1) * LANES], sl1[lb], sl2[lb]
        m = jnp.maximum(jnp.maximum(l0, l1), l2)
        e0, e1, e2 = jnp.exp(l0 - m), jnp.exp(l1 - m), jnp.exp(l2 - m)
        inv = 1.0 / (e0 + e1 + e2)
        wts.append((e0 * inv, e1 * inv, e2 * inv))
    for cb in range(D_MODEL // LANES):
        sl = slice(cb * LANES, (cb + 1) * LANES)
        for src, dst, d in ((o1_ref, so1, dils[1]), (o2_ref, so2, dils[2])):
            for r in range(d):
                dst[cb, pl.ds(r, tm // d, stride=d), :] = src[r, :, sl].astype(F32)
    part_rows = tm // OUT_PARTS

    def combine(p, cb):
        rows = slice(p * part_rows, (p + 1) * part_rows)
        sl = slice(cb * LANES, (cb + 1) * LANES)
        w0, w1, w2 = wts[cb // heads_per_block]
        hl = cb % heads_per_block
        bc = lambda w: jnp.broadcast_to(w[rows, hl:hl + 1], (part_rows, LANES))
        o = (bc(w0) * o0_ref[0, rows, sl].astype(F32) + bc(w1) * so1[cb, rows, :]
             + bc(w2) * so2[cb, rows, :])
        z = z_ref[rows, sl].astype(F32)
        y_ref[p, :, sl] = (o * (z * _sigmoid(z))).astype(BF16)

    _project_parts(combine, OUT_PARTS, part_rows, y_ref, w_ref, g_ref, x_ref, o_ref)


def _att_out(outs, lses, z, x2, w_bf, layer, g):
    rows = x2.shape[0]
    tm = 256
    dils = tuple(d for _, d in ATT_GROUPS)
    lse_w = lses[0].shape[-1]
    nlb = lse_w // LANES
    row_spec = pl.BlockSpec((tm, D_MODEL), lambda i: (i, 0))

    def res_spec(d, width):
        return pl.BlockSpec((d, tm // d, width), lambda i: (0, i, 0))

    kern = functools.partial(_att_out_kernel, tm=tm, dils=dils, heads_per_block=D_MODEL // LANES // nlb)
    return pl.pallas_call(
        kern,
        grid=(rows // tm,),
        in_specs=[res_spec(d, D_MODEL) for d in dils] + [res_spec(d, lse_w) for d in dils] + [
            row_spec, row_spec,
            pl.BlockSpec((None, D_MODEL, D_MODEL), lambda i: (layer, 0, 0)),
            pl.BlockSpec((1, D_MODEL), lambda i: (0, 0)),
        ],
        out_specs=row_spec,
        out_shape=jax.ShapeDtypeStruct((rows, D_MODEL), F32),
        scratch_shapes=[pltpu.VMEM((D_MODEL // LANES, tm, LANES), F32)] * 2
        + [pltpu.VMEM((nlb, tm, LANES), F32)] * 2 + [pltpu.VMEM((OUT_PARTS, tm // OUT_PARTS, D_MODEL), BF16)],
        compiler_params=_cparams("parallel"),
        name="att_out",
    )(*outs, *lses, z, x2, w_bf, g)


def _chunk_tri(n, reverse):
    t = lax.broadcasted_iota(jnp.int32, (n, n), 0)
    s = lax.broadcasted_iota(jnp.int32, (n, n), 1)
    same = (t // CHUNK) == (s // CHUNK)
    order = (s >= t) if reverse else (s <= t)
    return jnp.where(same & order, 1.0, 0.0).astype(BF16)


def _rwkv_in_kernel(*refs, tm, tn, tiles_per_seq, has_vres):
    (x_ref, xp_ref, xn_ref, g_ref, mup_ref, mun_ref,
     wr_ref, wk_ref, wv_ref, wz_ref,
     w1_ref, w2_ref, w0_ref, a1_ref, a2_ref, a0_ref,
     kk_ref, ka_ref, rk_ref) = refs[:19]
    pos = 19
    if has_vres:
        v1_ref, v2_ref, v0_ref, vf_ref = refs[pos:pos + 4]
        pos += 4
    (kkq_out, rq_out, ad_out, kdk_out, kend_out, aend_out, est_out, v_out, bonus_out, sz_out) = refs[pos:pos + 10]
    pos += 10
    xs_ref, hw_ref, ha_ref = refs[pos:pos + 3]
    hv_ref = refs[pos + 3] if has_vres else None

    i = pl.program_id(0)
    j = pl.program_id(1)
    mix_slot = {0: 0, 2: 1, 3: 2, 5: 3}

    @pl.when(j == 0)
    def _():
        t_in_seq = i % tiles_per_seq
        keep_prev = jnp.where(t_in_seq == 0, 0.0, 1.0).astype(F32)
        keep_next = jnp.where(t_in_seq == tiles_per_seq - 1, 0.0, 1.0).astype(F32)
        sx = _rms_scale(x_ref[...])
        xp = xp_ref[7:8, :]
        xn = xn_ref[0:1, :]
        sp = _rms_scale(xp) * keep_prev
        sn = _rms_scale(xn) * keep_next
        cw = 512
        row = lax.broadcasted_iota(jnp.int32, (tm, cw), 0)
        hw_acc = [jnp.zeros((tm, LORA_PAD), F32) for _ in range(2)]
        ha_acc = [jnp.zeros((tm, LORA_PAD), F32) for _ in range(2)]
        hv_acc = jnp.zeros((tm, LORA_PAD), F32)
        for cb in range(D_MODEL // cw):
            sl = slice(cb * cw, (cb + 1) * cw)
            g = g_ref[:, sl]
            h = x_ref[:, sl] * sx * g
            hp_row = xp[:, sl] * sp * g
            hn_row = xn[:, sl] * sn * g
            h_prev = jnp.where(row == 0, hp_row, pltpu.roll(h, 1, axis=0))
            h_next = jnp.where(row == tm - 1, hn_row, pltpu.roll(h, tm - 1, axis=0))
            h16 = h.astype(BF16)
            dp = (h_prev - h).astype(BF16)
            dn = (h_next - h).astype(BF16)
            mixes = {}
            for t in range(6):
                mixes[t] = h16 + dp * mup_ref[t:t + 1, sl].astype(BF16) + dn * mun_ref[t:t + 1, sl].astype(BF16)
                if t in mix_slot:
                    xs_ref[mix_slot[t], :, sl] = mixes[t]
            for c in range(2):
                hw_acc[c] = hw_acc[c] + jnp.dot(mixes[1], w1_ref[c, sl, :], preferred_element_type=F32)
                ha_acc[c] = ha_acc[c] + jnp.dot(mixes[4], a1_ref[c, sl, :], preferred_element_type=F32)
            if has_vres:
                hv_acc = hv_acc + jnp.dot(mixes[3], v1_ref[sl, :], preferred_element_type=F32)
        for c in range(2):
            hw_ref[c] = jnp.tanh(hw_acc[c]).astype(BF16)
            ha_ref[c] = ha_acc[c].astype(BF16)
        if has_vres:
            hv_ref[...] = hv_acc.astype(BF16)

    wl = [w0_ref[c:c + 1, :] + jnp.dot(hw_ref[c], w2_ref[c], preferred_element_type=F32) for c in range(2)]
    al = [a0_ref[c:c + 1, :] + jnp.dot(ha_ref[c], a2_ref[c], preferred_element_type=F32) for c in range(2)]
    if has_vres:
        gl = v0_ref[...] + jnp.dot(hv_ref[...], v2_ref[...], preferred_element_type=F32)
    k = jnp.dot(xs_ref[1], wk_ref[...], preferred_element_type=F32)
    r = jnp.dot(xs_ref[0], wr_ref[...], preferred_element_type=F32)
    z = jnp.dot(xs_ref[3], wz_ref[...], preferred_element_type=F32)
    v = jnp.dot(xs_ref[2], wv_ref[...], preferred_element_type=F32)
    if has_vres:
        v = v + (vf_ref[...] - v) * _sigmoid(gl)
    gmat = _group_matrix(tn)
    kk = k * kk_ref[...]
    kk = kk * lax.rsqrt(jnp.maximum(_group_sum(kk * kk, gmat), 1e-24))
    k_a = ka_ref[...]
    rk_acc = jnp.zeros((tm, tn), F32)
    nchunk = tm // CHUNK
    half = 256
    for c in range(2):
        lw = -math.exp(-0.5) * _sigmoid(wl[c])
        a = _sigmoid(al[c])
        kd = k * (1.0 + (a - 1.0) * k_a)
        rk_acc = rk_acc + r * kd * rk_ref[c:c + 1, :]
        tri = _chunk_tri(half, reverse=(c == 1))
        g = jnp.concatenate([_split_dot(tri, lw[hh * half:(hh + 1) * half]) for hh in range(tm // half)], axis=0)
        g3 = g.reshape(nchunk, CHUNK, tn)
        last = 0 if c == 1 else CHUNK - 1
        g_tot = g3[:, last:last + 1, :]
        e_tot = jnp.exp(g_tot)
        est_out[c] = e_tot.reshape(nchunk, tn)
        e_q = jnp.exp(g)
        e_qp = jnp.exp(g - lw)
        e_k = jnp.exp(-g)
        e_end = (e_k.reshape(nchunk, CHUNK, tn) * e_tot).reshape(tm, tn)
        kka = kk * a
        kkq_out[c] = (kk * e_qp).astype(BF16)
        rq_out[c] = (r * e_q).astype(BF16)
        ad_out[c] = (kka * e_k).astype(BF16)
        kdk_out[c] = (kd * e_k).astype(BF16)
        kend_out[c] = (kd * e_end).astype(BF16)
        aend_out[c] = (kka * e_end).astype(BF16)
    v_out[...] = v
    bonus_out[...] = _group_sum(rk_acc, gmat) * v
    sz_out[...] = (z * _sigmoid(z)).astype(BF16)


def _pad_lora(w1, w2):
    rank = w1.shape[-1]
    pad1 = [(0, 0)] * (w1.ndim - 1) + [(0, LORA_PAD - rank)]
    pad2 = [(0, 0)] * (w2.ndim - 2) + [(0, LORA_PAD - rank), (0, 0)]
    return jnp.pad(w1, pad1).astype(BF16), jnp.pad(w2, pad2).astype(BF16)


def _rwkv_in(x2, seq, g, mu_prev, mu_next, w_in_bf, layer, w0, w1, w2, a0, a1, a2, k_k, k_a, r_k, vres, v_first):
    rows = x2.shape[0]
    tm, tn = 512, 256
    tiles_per_seq = seq // tm
    ncol = D_MODEL // tn
    has_vres = vres is not None
    w1p, w2p = _pad_lora(w1, w2)
    a1p, a2p = _pad_lora(a1, a2)
    sub = tm // 8
    nsub = rows // 8

    def const2(shape):
        return pl.BlockSpec(shape, lambda i, j: (0, 0))

    def col2(nrow):
        return pl.BlockSpec((nrow, tn), lambda i, j: (0, j))

    def wcol(gi):
        return pl.BlockSpec((None, D_MODEL, tn), lambda i, j: (layer, 0, gi * ncol + j))

    in_specs = [
        pl.BlockSpec((tm, D_MODEL), lambda i, j: (i, 0)),
        pl.BlockSpec((8, D_MODEL), lambda i, j: (jnp.maximum(i * sub - 1, 0), 0)),
        pl.BlockSpec((8, D_MODEL), lambda i, j: (jnp.minimum((i + 1) * sub, nsub - 1), 0)),
        const2((1, D_MODEL)), const2((6, D_MODEL)), const2((6, D_MODEL)),
        wcol(0), wcol(1), wcol(2), wcol(3),
        pl.BlockSpec((2, D_MODEL, LORA_PAD), lambda i, j: (0, 0, 0)),
        pl.BlockSpec((2, LORA_PAD, tn), lambda i, j: (0, 0, j)),
        col2(2),
        pl.BlockSpec((2, D_MODEL, LORA_PAD), lambda i, j: (0, 0, 0)),
        pl.BlockSpec((2, LORA_PAD, tn), lambda i, j: (0, 0, j)),
        col2(2),
        col2(1), col2(1), col2(2),
    ]
    args = [x2, x2, x2, g, mu_prev, mu_next, w_in_bf, w_in_bf, w_in_bf, w_in_bf,
            w1p, w2p, w0, a1p, a2p, a0, k_k, k_a, r_k]
    if has_vres:
        v0, v1, v2 = vres
        v1p, v2p = _pad_lora(v1, v2)
        in_specs += [const2((D_MODEL, LORA_PAD)), col2(LORA_PAD), col2(1),
                     pl.BlockSpec((tm, tn), lambda i, j: (i, j))]
        args += [v1p, v2p, v0, v_first]

    tile = pl.BlockSpec((tm, tn), lambda i, j: (i, j))
    tile2 = pl.BlockSpec((2, tm, tn), lambda i, j: (0, i, j))
    est_spec = pl.BlockSpec((2, tm // CHUNK, tn), lambda i, j: (0, i, j))
    sds_bf2 = jax.ShapeDtypeStruct((2, rows, D_MODEL), BF16)
    sds = jax.ShapeDtypeStruct((rows, D_MODEL), F32)
    scratch = [pltpu.VMEM((4, tm, D_MODEL), BF16), pltpu.VMEM((2, tm, LORA_PAD), BF16),
               pltpu.VMEM((2, tm, LORA_PAD), BF16)]
    if has_vres:
        scratch.append(pltpu.VMEM((tm, LORA_PAD), BF16))
    kern = functools.partial(_rwkv_in_kernel, tm=tm, tn=tn, tiles_per_seq=tiles_per_seq, has_vres=has_vres)
    return pl.pallas_call(
        kern,
        grid=(rows // tm, ncol),
        in_specs=in_specs,
        out_specs=[tile2] * 6 + [est_spec, tile, tile, tile],
        out_shape=[sds_bf2] * 6 + [jax.ShapeDtypeStruct((2, rows // CHUNK, D_MODEL), F32), sds, sds,
                                   jax.ShapeDtypeStruct((rows, D_MODEL), BF16)],
        scratch_shapes=scratch,
        compiler_params=_cparams("parallel", "arbitrary"),
        name="rwkv_in_vres" if has_vres else "rwkv_in",
    )(*args)


def _rwkv_scan_kernel(kkq_ref, rq_ref, ad_ref, kdk_ref, kend_ref, aend_ref, v_ref, est_ref, y_ref, state_ref,
                      *, tb, npair):
    rev = pl.program_id(1) == 1
    step = pl.program_id(3)
    nchunk = tb // CHUNK
    pairs = range(npair)

    @pl.when(step == 0)
    def _():
        state_ref[...] = jnp.zeros_like(state_ref)

    t = lax.broadcasted_iota(jnp.int32, (CHUNK, LANES), 0)
    lane = lax.broadcasted_iota(jnp.int32, (CHUNK, LANES), 1)
    s = lane & (CHUNK - 1)
    d = jnp.where(rev, s - t, t - s)
    strict = d > 0
    incl = d >= 0
    eye = jnp.where(d == 0, 1.0, 0.0).astype(F32)
    first_half = lane < CHUNK
    half_a = jnp.where(first_half, 1.0, 0.0).astype(BF16)
    half_b = jnp.where(first_half, 0.0, 1.0).astype(BF16)
    rr = lax.broadcasted_iota(jnp.int32, (LANES, LANES), 0)
    cc = lax.broadcasted_iota(jnp.int32, (LANES, LANES), 1)
    bd_mask = (rr < CHUNK) == (cc < CHUNK)
    est_rows = lax.broadcasted_iota(jnp.int32, (nchunk, npair * LANES), 0)

    def bd16(y):
        return jnp.concatenate([y * half_a, y * half_b], axis=0)

    def bd32(y):
        return jnp.concatenate([jnp.where(first_half, y, 0.0), jnp.where(first_half, 0.0, y)],
                               axis=0).astype(BF16)

    def mm(a, b):
        return jnp.dot(a, b, preferred_element_type=F32)

    def body(ci, carry):
        first = ci * CHUNKS_PER_TRIP
        cs = [jnp.where(rev, nchunk - 1 - (first + u), first + u) for u in range(CHUNKS_PER_TRIP)]
        jobs = [(pl.ds(pl.multiple_of(c * CHUNK, CHUNK), CHUNK), slice(p * LANES, (p + 1) * LANES))
                for c in cs for p in pairs]
        n = range(len(jobs))
        kkq = [kkq_ref[rw, ln] for rw, ln in jobs]
        rq = [rq_ref[rw, ln] for rw, ln in jobs]
        v = [v_ref[rw, ln] for rw, ln in jobs]
        gram = [lax.dot_general(
            jnp.concatenate([kkq[i], rq[i]], axis=0),
            jnp.concatenate([bd16(ad_ref[jobs[i]]), bd16(kdk_ref[jobs[i]])], axis=0),
            NT_DIMS, preferred_element_type=F32) for i in n]
        a_mat = [jnp.where(strict, gram[i][0:CHUNK, 0:LANES], 0.0) for i in n]
        ak = [jnp.where(strict, gram[i][0:CHUNK, LANES:], 0.0) for i in n]
        bra = [jnp.where(incl, gram[i][CHUNK:, 0:LANES], 0.0).astype(BF16) for i in n]
        brk = [jnp.where(incl, gram[i][CHUNK:, LANES:], 0.0) for i in n]
        early = [mm(jnp.concatenate([ak[i], brk[i]], axis=0).astype(BF16), bd32(v[i])) for i in n]
        a_pow = [mm(a_mat[i].astype(BF16), bd32(a_mat[i])) for i in n]
        inv = [eye - a_mat[i] for i in n]
        for _ in range(4):
            st = [mm(jnp.concatenate([inv[i], a_pow[i]], axis=0).astype(BF16), bd32(a_pow[i])) for i in n]
            inv = [inv[i] + st[i][0:CHUNK] for i in n]
            a_pow = [st[i][CHUNK:] for i in n]
        inv = [inv[i] + mm(inv[i].astype(BF16), bd32(a_pow[i])) for i in n]
        wu = [mm(inv[i].astype(BF16), jnp.concatenate([bd16(kkq[i]), bd32(early[i][0:CHUNK])], axis=1))
              for i in n]
        ry = [mm(bra[i], jnp.concatenate([bd32(wu[i][:, 0:LANES]), bd32(wu[i][:, LANES:])], axis=1))
              for i in n]
        aend = [aend_ref[jobs[i]] for i in n]
        xraw = [lax.dot_general(wu[i][:, 0:LANES].astype(BF16), aend[i], TN_DIMS, preferred_element_type=F32)
                for i in n]
        nraw = [lax.dot_general(jnp.concatenate([v[i], -wu[i][:, LANES:]], axis=0).astype(BF16),
                                jnp.concatenate([kend_ref[jobs[i]], aend[i]], axis=0),
                                TN_DIMS, preferred_element_type=F32) for i in n]
        rqp = [(rq[i].astype(F32) - ry[i][:, 0:LANES]).astype(BF16) for i in n]
        yv = [early[i][CHUNK:] - ry[i][:, LANES:] for i in n]
        xmat = [jnp.where(bd_mask, xraw[i], 0.0).astype(BF16) for i in n]
        nmat = [jnp.where(bd_mask, nraw[i], 0.0) for i in n]
        est_all = est_ref[...]
        state = [state_ref[p] for p in pairs]
        for u in range(CHUNKS_PER_TRIP):
            est_row = jnp.sum(jnp.where(est_rows == cs[u], est_all, 0.0), axis=0, keepdims=True)
            idx = [u * npair + p for p in pairs]
            s16 = [state[p].astype(BF16) for p in pairs]
            ys = [yv[idx[p]] + lax.dot_general(rqp[idx[p]], s16[p], NT_DIMS, preferred_element_type=F32)
                  for p in pairs]
            sx = [mm(s16[p], xmat[idx[p]]) for p in pairs]
            for p in pairs:
                rw, ln = jobs[idx[p]]
                y_ref[rw, ln] = ys[p]
                state[p] = state[p] * est_row[:, ln] - sx[p] + nmat[idx[p]]
        for p in pairs:
            state_ref[p] = state[p]
        return carry

    lax.fori_loop(0, nchunk // CHUNKS_PER_TRIP, body, 0)


def _rwkv_scan(kkq, rq, ad, kdk, kend, aend, est, v, batch, seq):
    tb, lw_lanes = 512, D_MODEL
    nt = seq // tb
    npair = lw_lanes // LANES

    def tmap(i, dr):
        return i + dr * (nt - 1 - 2 * i)

    v_spec = pl.BlockSpec((tb, lw_lanes), lambda b, dr, h, i: (b * nt + tmap(i, dr), h))
    d_spec = pl.BlockSpec((None, tb, lw_lanes), lambda b, dr, h, i: (dr, b * nt + tmap(i, dr), h))
    e_spec = pl.BlockSpec((None, tb // CHUNK, lw_lanes), lambda b, dr, h, i: (dr, b * nt + tmap(i, dr), h))
    kern = functools.partial(_rwkv_scan_kernel, tb=tb, npair=npair)
    return pl.pallas_call(
        kern,
        grid=(batch, 2, D_MODEL // lw_lanes, nt),
        in_specs=[d_spec] * 6 + [v_spec, e_spec],
        out_specs=d_spec,
        out_shape=jax.ShapeDtypeStruct((2, batch * seq, D_MODEL), F32),
        scratch_shapes=[pltpu.VMEM((npair, LANES, LANES), F32)],
        compiler_params=_cparams("parallel", "parallel", "parallel", "arbitrary"),
        name="rwkv_scan",
    )(kkq, rq, ad, kdk, kend, aend, v, est)


def _rwkv_out_kernel(y_ref, bonus_ref, sz_ref, gg_ref, gb_ref, x_ref, w_ref, g_ref, o_ref, yb_ref, *, tm):
    gmat = _group_matrix(LANES)
    inv_n = 1.0 / RWKV_HEAD_DIM
    part_rows = tm // OUT_PARTS

    def groupnorm(p, cb):
        rows = slice(p * part_rows, (p + 1) * part_rows)
        sl = slice(cb * LANES, (cb + 1) * LANES)
        y = y_ref[0, rows, sl] + y_ref[1, rows, sl]
        mu = _group_sum(y, gmat) * inv_n
        yc = y - mu
        var = _group_sum(yc * yc, gmat) * inv_n
        yn = yc * lax.rsqrt(var + GN_EPS) * gg_ref[:, sl] + gb_ref[:, sl]
        yb_ref[p, :, sl] = ((yn + bonus_ref[rows, sl]) * sz_ref[rows, sl].astype(F32)).astype(BF16)

    _project_parts(groupnorm, OUT_PARTS, part_rows, yb_ref, w_ref, g_ref, x_ref, o_ref)


def _rwkv_out(y2, bonus, sz, gn_g, gn_b, x2, w_bf, layer, g):
    rows = x2.shape[0]
    tm = 256
    row_spec = pl.BlockSpec((tm, D_MODEL), lambda i: (i, 0))
    vec = pl.BlockSpec((1, D_MODEL), lambda i: (0, 0))
    return pl.pallas_call(
        functools.partial(_rwkv_out_kernel, tm=tm),
        grid=(rows // tm,),
        in_specs=[pl.BlockSpec((2, tm, D_MODEL), lambda i: (0, i, 0)), row_spec, row_spec, vec, vec, row_spec,
                  pl.BlockSpec((None, D_MODEL, D_MODEL), lambda i: (layer, 0, 0)), vec],
        out_specs=row_spec,
        out_shape=jax.ShapeDtypeStruct((rows, D_MODEL), F32),
        scratch_shapes=[pltpu.VMEM((OUT_PARTS, tm // OUT_PARTS, D_MODEL), BF16)],
        compiler_params=_cparams("parallel"),
        name="rwkv_out",
    )(y2, bonus, sz, gn_g, gn_b, x2, w_bf, g)


def _trunk(x, p, rope):
    batch, seq, _ = x.shape
    x2 = x.reshape(batch * seq, D_MODEL)
    v_first = None
    depth = p["norm_pre"].shape[0]
    for layer in range(depth):
        j = layer // 2
        g_pre = p["norm_pre"][layer][None, :]
        g_post = p["norm_post"][layer][None, :]
        if layer % 2 == 0:
            w_in = p["att_w_in"]
            outs, lses = [], []
            for gidx, (_, dil) in enumerate(ATT_GROUPS):
                q, k, v = _att_in(x2, g_pre, w_in, j, *rope[gidx], seq, gidx, dil)
                o, lse = _attn_group(q, k, v, dil, batch, seq)
                outs.append(o)
                lses.append(lse)
            z = _gate_in(x2, g_pre, w_in, j)
            x2 = _att_out(outs, lses, z, x2, p["att_w_out"], j, g_post)
        else:
            vres = None if j == 0 else (p["rwkv_v0"][j - 1][None, :], p["rwkv_v1"][j - 1], p["rwkv_v2"][j - 1])
            kkq, rq, ad, kdk, kend, aend, est, v, bonus, sz = _rwkv_in(
                x2, seq, g_pre, p["rwkv_mu_prev"][j], p["rwkv_mu_next"][j], p["rwkv_w_in"], j,
                p["rwkv_w0"][j], p["rwkv_w1"][j], p["rwkv_w2"][j],
                p["rwkv_a0"][j], p["rwkv_a1"][j], p["rwkv_a2"][j],
                p["rwkv_k_k"][j][None, :], p["rwkv_k_a"][j][None, :], p["rwkv_r_k"][j].reshape(2, D_MODEL),
                vres, v_first)
            if j == 0:
                v_first = v
            y2 = _rwkv_scan(kkq, rq, ad, kdk, kend, aend, est, v, batch, seq)
            x2 = _rwkv_out(y2, bonus, sz, p["rwkv_gn_g"][j][None, :], p["rwkv_gn_b"][j][None, :], x2,
                           p["rwkv_w_out"], j, g_post)
    return x2.reshape(batch, seq, D_MODEL)


def kernel(x_prompt, x_sample, norm_pre, norm_post, att_w_in, att_w_out, rwkv_mu_prev, rwkv_mu_next, rwkv_w_in, rwkv_w0, rwkv_w1, rwkv_w2, rwkv_a0, rwkv_a1, rwkv_a2, rwkv_v0, rwkv_v1, rwkv_v2, rwkv_k_k, rwkv_k_a, rwkv_r_k, rwkv_gn_g, rwkv_gn_b, rwkv_w_out):
    p = dict(
        norm_pre=norm_pre, norm_post=norm_post,
        att_w_in=att_w_in.astype(BF16), att_w_out=att_w_out.astype(BF16),
        rwkv_mu_prev=rwkv_mu_prev, rwkv_mu_next=rwkv_mu_next, rwkv_w_in=rwkv_w_in.astype(BF16),
        rwkv_w0=rwkv_w0, rwkv_w1=rwkv_w1, rwkv_w2=rwkv_w2,
        rwkv_a0=rwkv_a0, rwkv_a1=rwkv_a1, rwkv_a2=rwkv_a2,
        rwkv_v0=rwkv_v0, rwkv_v1=rwkv_v1, rwkv_v2=rwkv_v2,
        rwkv_k_k=rwkv_k_k, rwkv_k_a=rwkv_k_a, rwkv_r_k=rwkv_r_k,
        rwkv_gn_g=rwkv_gn_g, rwkv_gn_b=rwkv_gn_b, rwkv_w_out=rwkv_w_out.astype(BF16),
    )
    max_seq = max(x_prompt.shape[1], x_sample.shape[1])
    rope = [_rope_tables(max_seq, dil) for _, dil in ATT_GROUPS]
    return (_trunk(x_prompt, p, rope), _trunk(x_sample, p, rope))
```

```python
import functools
import math

import jax
import jax.numpy as jnp
from jax import lax
from jax.experimental import pallas as pl
from jax.experimental.pallas import tpu as pltpu

F32 = jnp.float32
BF16 = jnp.bfloat16

D_MODEL = 2048
LANES = 128
MXU_COLS = 256
ATT_HEAD_DIM = 128
ATT_GROUPS = ((128, 1), (512, 4), (2048, 16))
ATT_HALF = 64
ROPE_THETA = 10000.0
RWKV_HEAD_DIM = 64
LORA_PAD = 128
RMS_EPS = 1e-6
GN_EPS = 64e-5
NEG_INF = -1e30
CHUNK = 64
ATT_IN_ROWS = 512
OUT_PARTS = 2
CHUNKS_PER_TRIP = 2
VMEM_LIMIT_BYTES = 56 * 1024 * 1024

NT_DIMS = (((1,), (1,)), ((), ()))
TN_DIMS = (((0,), (0,)), ((), ()))


def _cparams(*sem):
    return pltpu.CompilerParams(dimension_semantics=sem, vmem_limit_bytes=VMEM_LIMIT_BYTES)


def _rms_scale(x):
    return lax.rsqrt(jnp.mean(x * x, axis=-1, keepdims=True) + RMS_EPS)


def _sigmoid(x):
    return 0.5 * jnp.tanh(0.5 * x) + 0.5


def _split_dot(lhs_bf, x):
    hi = x.astype(BF16)
    lo = (x - hi.astype(F32)).astype(BF16)
    return (jnp.dot(lhs_bf, hi, preferred_element_type=F32)
            + jnp.dot(lhs_bf, lo, preferred_element_type=F32))


def _group_sum(x, gmat):
    hi = x.astype(BF16)
    lo = (x - hi.astype(F32)).astype(BF16)
    return (jnp.dot(hi, gmat, preferred_element_type=F32)
            + jnp.dot(lo, gmat, preferred_element_type=F32))


def _group_matrix(n):
    r = lax.broadcasted_iota(jnp.int32, (n, n), 0) // RWKV_HEAD_DIM
    c = lax.broadcasted_iota(jnp.int32, (n, n), 1) // RWKV_HEAD_DIM
    return jnp.where(r == c, 1.0, 0.0).astype(BF16)


def _rope_table_kernel(invf_ref, cos_ref, sin_ref, *, dil):
    rows = cos_ref.shape[0]
    base = pl.program_id(0) * rows
    local = lax.broadcasted_iota(jnp.int32, (rows, LANES), 0)
    per_res = rows // dil
    pos = (base + (local % per_res) * dil + local // per_res).astype(F32)
    ang = pos * invf_ref[...]
    lane = lax.broadcasted_iota(jnp.int32, (rows, LANES), 1)
    s = jnp.sin(ang)
    cos_ref[...] = jnp.cos(ang)
    sin_ref[...] = jnp.where(lane < ATT_HEAD_DIM // 2, -s, s)


def _rope_tables(seq, dil):
    half = ATT_HEAD_DIM // 2
    inv_freq = 1.0 / (ROPE_THETA ** (jnp.arange(half, dtype=F32) * 2.0 / ATT_HEAD_DIM))
    invf = jnp.concatenate([inv_freq, inv_freq])[None, :]
    rows = ATT_IN_ROWS
    return pl.pallas_call(
        functools.partial(_rope_table_kernel, dil=dil),
        grid=(seq // rows,),
        in_specs=[pl.BlockSpec((1, LANES), lambda i: (0, 0))],
        out_specs=[pl.BlockSpec((rows, LANES), lambda i: (i, 0))] * 2,
        out_shape=[jax.ShapeDtypeStruct((seq, LANES), F32)] * 2,
        compiler_params=_cparams("arbitrary"),
        name=f"rope_table_d{dil}",
    )(invf)


def _att_in_kernel(x_ref, g_ref, wq_ref, wk_ref, wv_ref, cos_ref, sin_ref, q_ref, k_ref, v_ref, h_ref, *hs_ref,
                   tm, tn, dil, scale):
    per_res = tm // dil

    @pl.when(pl.program_id(1) == 0)
    def _():
        x = x_ref[...]
        h = x * _rms_scale(x) * g_ref[...]
        if dil == 1:
            h_ref[...] = h.astype(BF16)
        else:
            for cb in range(D_MODEL // LANES):
                sl = slice(cb * LANES, (cb + 1) * LANES)
                hs_ref[0][cb] = h[:, sl]
                for r in range(dil):
                    h_ref[r * per_res:(r + 1) * per_res, sl] = (
                        hs_ref[0][cb, pl.ds(r, per_res, stride=dil), :].astype(BF16))

    h = h_ref[...]
    cos = cos_ref[...]
    sin = sin_ref[...]
    for w_ref, o_ref, rope, sc in ((wq_ref, q_ref, True, scale), (wk_ref, k_ref, True, None),
                                   (wv_ref, v_ref, False, None)):
        for cb in range(tn // MXU_COLS):
            acc = jnp.dot(h, w_ref[:, cb * MXU_COLS:(cb + 1) * MXU_COLS], preferred_element_type=F32)
            for hh in range(MXU_COLS // LANES):
                sl = slice(cb * MXU_COLS + hh * LANES, cb * MXU_COLS + (hh + 1) * LANES)
                t = acc[:, hh * LANES:(hh + 1) * LANES]
                if rope:
                    t = t * cos + pltpu.roll(t, ATT_HEAD_DIM // 2, axis=1) * sin
                if sc is not None:
                    t = t * sc
                t = t.astype(BF16)
                for r in range(dil):
                    o_ref[r, :, sl] = t[r * per_res:(r + 1) * per_res]


def _att_in(x2, g, w_bf, layer, cos_t, sin_t, seq, gidx, dil):
    rows = x2.shape[0]
    tm, tn = ATT_IN_ROWS, 1024
    tiles_per_seq = seq // tm
    ncol = D_MODEL // tn
    kern = functools.partial(_att_in_kernel, tm=tm, tn=tn, dil=dil, scale=ATT_HEAD_DIM ** -0.5)

    def wcol(part):
        return pl.BlockSpec((None, D_MODEL, tn), lambda i, j: (layer, 0, (gidx * 3 + part) * ncol + j))

    out_spec = pl.BlockSpec((dil, tm // dil, tn), lambda i, j: (0, i, j))
    out_sds = jax.ShapeDtypeStruct((dil, rows // dil, D_MODEL), BF16)
    scratch = [pltpu.VMEM((tm, D_MODEL), BF16)]
    if dil > 1:
        scratch.append(pltpu.VMEM((D_MODEL // LANES, tm, LANES), F32))
    return pl.pallas_call(
        kern,
        grid=(rows // tm, ncol),
        in_specs=[
            pl.BlockSpec((tm, D_MODEL), lambda i, j: (i, 0)),
            pl.BlockSpec((1, D_MODEL), lambda i, j: (0, 0)),
            wcol(0), wcol(1), wcol(2),
            pl.BlockSpec((tm, LANES), lambda i, j: (i % tiles_per_seq, 0)),
            pl.BlockSpec((tm, LANES), lambda i, j: (i % tiles_per_seq, 0)),
        ],
        out_specs=[out_spec] * 3,
        out_shape=[out_sds] * 3,
        scratch_shapes=scratch,
        compiler_params=_cparams("parallel", "arbitrary"),
        name=f"att_in_d{dil}",
    )(x2, g, w_bf, w_bf, w_bf, cos_t, sin_t)


def _gate_in_kernel(x_ref, g_ref, w_ref, o_ref):
    x = x_ref[...]
    h = (x * _rms_scale(x) * g_ref[...]).astype(BF16)
    o_ref[...] = jnp.dot(h, w_ref[...], preferred_element_type=F32).astype(BF16)


def _gate_in(x2, g, w_bf, layer):
    rows = x2.shape[0]
    tm = 512
    col0 = 3 * len(ATT_GROUPS)
    return pl.pallas_call(
        _gate_in_kernel,
        grid=(rows // tm,),
        in_specs=[
            pl.BlockSpec((tm, D_MODEL), lambda i: (i, 0)),
            pl.BlockSpec((1, D_MODEL), lambda i: (0, 0)),
            pl.BlockSpec((None, D_MODEL, D_MODEL), lambda i: (layer, 0, col0)),
        ],
        out_specs=pl.BlockSpec((tm, D_MODEL), lambda i: (i, 0)),
        out_shape=jax.ShapeDtypeStruct((rows, D_MODEL), BF16),
        compiler_params=_cparams("parallel"),
        name="att_gate_in",
    )(x2, g, w_bf)


def _attn_kernel(q_ref, kp_ref, kc_ref, kn_ref, vp_ref, vc_ref, vn_ref, o_ref, lse_ref, *, bq, sb, sub_len, nh):
    i = pl.program_id(3)
    nk = sb + 2 * ATT_HALF
    nsub = bq // sb
    ii = lax.broadcasted_iota(jnp.int32, (sb, nk), 0)
    jj = lax.broadcasted_iota(jnp.int32, (sb, nk), 1)
    rel = jj - ii
    in_band = (rel >= 0) & (rel <= 2 * ATT_HALF)
    valid = []
    for u in range(nsub):
        kpos = i * bq + u * sb - ATT_HALF + jj
        valid.append(in_band & (kpos >= 0) & (kpos < sub_len))
    lane = lax.broadcasted_iota(jnp.int32, (sb, LANES), 1)
    units = [(hh, u) for hh in range(nh) for u in range(nsub)]

    def scores(hh, u):
        sl = slice(hh * LANES, (hh + 1) * LANES)
        kcat = jnp.concatenate([kp_ref[:, sl], kc_ref[:, sl], kn_ref[:, sl]], axis=0)
        s = lax.dot_general(q_ref[u * sb:(u + 1) * sb, sl], kcat[u * sb:u * sb + nk], NT_DIMS,
                            preferred_element_type=F32)
        return jnp.where(valid[u], s, NEG_INF)

    def finish(hh, u, s, lse_acc):
        sl = slice(hh * LANES, (hh + 1) * LANES)
        vcat = jnp.concatenate([vp_ref[:, sl], vc_ref[:, sl], vn_ref[:, sl]], axis=0)
        m = jnp.max(s, axis=-1, keepdims=True)
        p = jnp.exp(s - m)
        l = jnp.sum(p, axis=-1, keepdims=True)
        o = jnp.dot(p.astype(BF16), vcat[u * sb:u * sb + nk], preferred_element_type=F32)
        o_ref[u * sb:(u + 1) * sb, sl] = (o / l).astype(BF16)
        lse_acc[u] = jnp.where(lane == hh, m + jnp.log(l), lse_acc[u])

    lse_acc = [jnp.zeros((sb, LANES), F32) for _ in range(nsub)]
    s_prev = scores(*units[0])
    for k in range(1, len(units)):
        s_next = scores(*units[k])
        finish(*units[k - 1], s_prev, lse_acc)
        s_prev = s_next
    finish(*units[-1], s_prev, lse_acc)
    for u in range(nsub):
        lse_ref[u * sb:(u + 1) * sb, :] = lse_acc[u]


def _attn_group(q, k, v, dil, batch, seq):
    sub_len = seq // dil
    bq = min(512, sub_len)
    sb = min(128, bq)
    hw = D_MODEL
    nh = hw // LANES
    hblocks = D_MODEL // hw
    nqb = sub_len // bq
    halo_per_q = bq // ATT_HALF
    n_halo = sub_len // ATT_HALF

    cur = pl.BlockSpec((None, bq, hw), lambda b, r, h, i: (r, b * nqb + i, h))
    prev = pl.BlockSpec((None, ATT_HALF, hw),
                        lambda b, r, h, i: (r, b * n_halo + jnp.maximum(i * halo_per_q - 1, 0), h))
    nxt = pl.BlockSpec((None, ATT_HALF, hw),
                       lambda b, r, h, i: (r, b * n_halo + jnp.minimum((i + 1) * halo_per_q, n_halo - 1), h))
    lse_spec = pl.BlockSpec((None, bq, LANES), lambda b, r, h, i: (r, b * nqb + i, h))
    rows = batch * sub_len
    kern = functools.partial(_attn_kernel, bq=bq, sb=sb, sub_len=sub_len, nh=nh)
    return pl.pallas_call(
        kern,
        grid=(batch, dil, hblocks, nqb),
        in_specs=[cur, prev, cur, nxt, prev, cur, nxt],
        out_specs=[cur, lse_spec],
        out_shape=[jax.ShapeDtypeStruct((dil, rows, D_MODEL), BF16),
                   jax.ShapeDtypeStruct((dil, rows, hblocks * LANES), F32)],
        compiler_params=_cparams("parallel", "parallel", "parallel", "arbitrary"),
        name=f"attn_d{dil}",
    )(q, k, k, k, v, v, v)


def _project_parts(prologue, nparts, part_rows, y_ref, w_ref, g_ref, x_ref, o_ref):
    ncb = D_MODEL // LANES
    cb_per_k = MXU_COLS // LANES

    def finish(out, p):
        rows = slice(p * part_rows, (p + 1) * part_rows)
        o_ref[rows, :] = x_ref[rows, :] + out * _rms_scale(out) * g_ref[...]

    for cb in range(ncb):
        prologue(0, cb)
    for p in range(nparts):
        acc = None
        for kc in range(D_MODEL // MXU_COLS):
            ks = slice(kc * MXU_COLS, (kc + 1) * MXU_COLS)
            part = jnp.dot(y_ref[p, :, ks], w_ref[ks, :], preferred_element_type=F32)
            acc = part if acc is None else acc + part
            if p + 1 < nparts:
                for cb in range(kc * cb_per_k, (kc + 1) * cb_per_k):
                    prologue(p + 1, cb)
        finish(acc, p)


def _att_out_kernel(o0_ref, o1_ref, o2_ref, l0_ref, l1_ref, l2_ref, z_ref, x_ref, w_ref, g_ref, o_ref,
                    so1, so2, sl1, sl2, y_ref, *, tm, dils, heads_per_block):
    nlb = l0_ref.shape[-1] // LANES
    for src, dst, d in ((l1_ref, sl1, dils[1]), (l2_ref, sl2, dils[2])):
        for lb in range(nlb):
            for r in range(d):
                dst[lb, pl.ds(r, tm // d, stride=d), :] = src[r, :, lb * LANES:(lb + 1) * LANES]
    wts = []
    for lb in range(nlb):
        l0, l1, l2 = l0_ref[0, :, lb * LANES:(lb + 1) * LANES], sl1[lb], sl2[lb]
        m = jnp.maximum(jnp.maximum(l0, l1), l2)
        e0, e1, e2 = jnp.exp(l0 - m), jnp.exp(l1 - m), jnp.exp(l2 - m)
        inv = 1.0 / (e0 + e1 + e2)
        wts.append((e0 * inv, e1 * inv, e2 * inv))
    for cb in range(D_MODEL // LANES):
        sl = slice(cb * LANES, (cb + 1) * LANES)
        for src, dst, d in ((o1_ref, so1, dils[1]), (o2_ref, so2, dils[2])):
            for r in range(d):
                dst[cb, pl.ds(r, tm // d, stride=d), :] = src[r, :, sl].astype(F32)
    part_rows = tm // OUT_PARTS

    def combine(p, cb):
        rows = slice(p * part_rows, (p + 1) * part_rows)
        sl = slice(cb * LANES, (cb + 1) * LANES)
        w0, w1, w2 = wts[cb // heads_per_block]
        hl = cb % heads_per_block
        bc = lambda w: jnp.broadcast_to(w[rows, hl:hl + 1], (part_rows, LANES))
        o = (bc(w0) * o0_ref[0, rows, sl].astype(F32) + bc(w1) * so1[cb, rows, :]
             + bc(w2) * so2[cb, rows, :])
        z = z_ref[rows, sl].astype(F32)
        y_ref[p, :, sl] = (o * (z * _sigmoid(z))).astype(BF16)

    _project_parts(combine, OUT_PARTS, part_rows, y_ref, w_ref, g_ref, x_ref, o_ref)


def _att_out(outs, lses, z, x2, w_bf, layer, g):
    rows = x2.shape[0]
    tm = 256
    dils = tuple(d for _, d in ATT_GROUPS)
    lse_w = lses[0].shape[-1]
    nlb = lse_w // LANES
    row_spec = pl.BlockSpec((tm, D_MODEL), lambda i: (i, 0))

    def res_spec(d, width):
        return pl.BlockSpec((d, tm // d, width), lambda i: (0, i, 0))

    kern = functools.partial(_att_out_kernel, tm=tm, dils=dils, heads_per_block=D_MODEL // LANES // nlb)
    return pl.pallas_call(
        kern,
        grid=(rows // tm,),
        in_specs=[res_spec(d, D_MODEL) for d in dils] + [res_spec(d, lse_w) for d in dils] + [
            row_spec, row_spec,
            pl.BlockSpec((None, D_MODEL, D_MODEL), lambda i: (layer, 0, 0)),
            pl.BlockSpec((1, D_MODEL), lambda i: (0, 0)),
        ],
        out_specs=row_spec,
        out_shape=jax.ShapeDtypeStruct((rows, D_MODEL), F32),
        scratch_shapes=[pltpu.VMEM((D_MODEL // LANES, tm, LANES), F32)] * 2
        + [pltpu.VMEM((nlb, tm, LANES), F32)] * 2 + [pltpu.VMEM((OUT_PARTS, tm // OUT_PARTS, D_MODEL), BF16)],
        compiler_params=_cparams("parallel"),
        name="att_out",
    )(*outs, *lses, z, x2, w_bf, g)


def _chunk_tri(n, reverse):
    t = lax.broadcasted_iota(jnp.int32, (n, n), 0)
    s = lax.broadcasted_iota(jnp.int32, (n, n), 1)
    same = (t // CHUNK) == (s // CHUNK)
    order = (s >= t) if reverse else (s <= t)
    return jnp.where(same & order, 1.0, 0.0).astype(BF16)


def _rwkv_in_kernel(*refs, tm, tn, tiles_per_seq, has_vres):
    (x_ref, xp_ref, xn_ref, g_ref, mup_ref, mun_ref,
     wr_ref, wk_ref, wv_ref, wz_ref,
     w1_ref, w2_ref, w0_ref, a1_ref, a2_ref, a0_ref,
     kk_ref, ka_ref, rk_ref) = refs[:19]
    pos = 19
    if has_vres:
        v1_ref, v2_ref, v0_ref, vf_ref = refs[pos:pos + 4]
        pos += 4
    (kkq_out, rq_out, ad_out, kdk_out, kend_out, aend_out, est_out, v_out, bonus_out, sz_out) = refs[pos:pos + 10]
    pos += 10
    xs_ref, hw_ref, ha_ref = refs[pos:pos + 3]
    hv_ref = refs[pos + 3] if has_vres else None

    i = pl.program_id(0)
    j = pl.program_id(1)
    mix_slot = {0: 0, 2: 1, 3: 2, 5: 3}

    @pl.when(j == 0)
    def _():
        t_in_seq = i % tiles_per_seq
        keep_prev = jnp.where(t_in_seq == 0, 0.0, 1.0).astype(F32)
        keep_next = jnp.where(t_in_seq == tiles_per_seq - 1, 0.0, 1.0).astype(F32)
        sx = _rms_scale(x_ref[...])
        xp = xp_ref[7:8, :]
        xn = xn_ref[0:1, :]
        sp = _rms_scale(xp) * keep_prev
        sn = _rms_scale(xn) * keep_next
        cw = 512
        row = lax.broadcasted_iota(jnp.int32, (tm, cw), 0)
        hw_acc = [jnp.zeros((tm, LORA_PAD), F32) for _ in range(2)]
        ha_acc = [jnp.zeros((tm, LORA_PAD), F32) for _ in range(2)]
        hv_acc = jnp.zeros((tm, LORA_PAD), F32)
        for cb in range(D_MODEL // cw):
            sl = slice(cb * cw, (cb + 1) * cw)
            g = g_ref[:, sl]
            h = x_ref[:, sl] * sx * g
            hp_row = xp[:, sl] * sp * g
            hn_row = xn[:, sl] * sn * g
            h_prev = jnp.where(row == 0, hp_row, pltpu.roll(h, 1, axis=0))
            h_next = jnp.where(row == tm - 1, hn_row, pltpu.roll(h, tm - 1, axis=0))
            h16 = h.astype(BF16)
            dp = (h_prev - h).astype(BF16)
            dn = (h_next - h).astype(BF16)
            mixes = {}
            for t in range(6):
                mixes[t] = h16 + dp * mup_ref[t:t + 1, sl].astype(BF16) + dn * mun_ref[t:t + 1, sl].astype(BF16)
                if t in mix_slot:
                    xs_ref[mix_slot[t], :, sl] = mixes[t]
            for c in range(2):
                hw_acc[c] = hw_acc[c] + jnp.dot(mixes[1], w1_ref[c, sl, :], preferred_element_type=F32)
                ha_acc[c] = ha_acc[c] + jnp.dot(mixes[4], a1_ref[c, sl, :], preferred_element_type=F32)
            if has_vres:
                hv_acc = hv_acc + jnp.dot(mixes[3], v1_ref[sl, :], preferred_element_type=F32)
        for c in range(2):
            hw_ref[c] = jnp.tanh(hw_acc[c]).astype(BF16)
            ha_ref[c] = ha_acc[c].astype(BF16)
        if has_vres:
            hv_ref[...] = hv_acc.astype(BF16)

    wl = [w0_ref[c:c + 1, :] + jnp.dot(hw_ref[c], w2_ref[c], preferred_element_type=F32) for c in range(2)]
    al = [a0_ref[c:c + 1, :] + jnp.dot(ha_ref[c], a2_ref[c], preferred_element_type=F32) for c in range(2)]
    if has_vres:
        gl = v0_ref[...] + jnp.dot(hv_ref[...], v2_ref[...], preferred_element_type=F32)
    k = jnp.dot(xs_ref[1], wk_ref[...], preferred_element_type=F32)
    r = jnp.dot(xs_ref[0], wr_ref[...], preferred_element_type=F32)
    z = jnp.dot(xs_ref[3], wz_ref[...], preferred_element_type=F32)
    v = jnp.dot(xs_ref[2], wv_ref[...], preferred_element_type=F32)
    if has_vres:
        v = v + (vf_ref[...] - v) * _sigmoid(gl)
    gmat = _group_matrix(tn)
    kk = k * kk_ref[...]
    kk = kk * lax.rsqrt(jnp.maximum(_group_sum(kk * kk, gmat), 1e-24))
    k_a = ka_ref[...]
    rk_acc = jnp.zeros((tm, tn), F32)
    nchunk = tm // CHUNK
    half = 256
    for c in range(2):
        lw = -math.exp(-0.5) * _sigmoid(wl[c])
        a = _sigmoid(al[c])
        kd = k * (1.0 + (a - 1.0) * k_a)
        rk_acc = rk_acc + r * kd * rk_ref[c:c + 1, :]
        tri = _chunk_tri(half, reverse=(c == 1))
        g = jnp.concatenate([_split_dot(tri, lw[hh * half:(hh + 1) * half]) for hh in range(tm // half)], axis=0)
        g3 = g.reshape(nchunk, CHUNK, tn)
        last = 0 if c == 1 else CHUNK - 1
        g_tot = g3[:, last:last + 1, :]
        e_tot = jnp.exp(g_tot)
        est_out[c] = e_tot.reshape(nchunk, tn)
        e_q = jnp.exp(g)
        e_qp = jnp.exp(g - lw)
        e_k = jnp.exp(-g)
        e_end = (e_k.reshape(nchunk, CHUNK, tn) * e_tot).reshape(tm, tn)
        kka = kk * a
        kkq_out[c] = (kk * e_qp).astype(BF16)
        rq_out[c] = (r * e_q).astype(BF16)
        ad_out[c] = (kka * e_k).astype(BF16)
        kdk_out[c] = (kd * e_k).astype(BF16)
        kend_out[c] = (kd * e_end).astype(BF16)
        aend_out[c] = (kka * e_end).astype(BF16)
    v_out[...] = v
    bonus_out[...] = _group_sum(rk_acc, gmat) * v
    sz_out[...] = (z * _sigmoid(z)).astype(BF16)


def _pad_lora(w1, w2):
    rank = w1.shape[-1]
    pad1 = [(0, 0)] * (w1.ndim - 1) + [(0, LORA_PAD - rank)]
    pad2 = [(0, 0)] * (w2.ndim - 2) + [(0, LORA_PAD - rank), (0, 0)]
    return jnp.pad(w1, pad1).astype(BF16), jnp.pad(w2, pad2).astype(BF16)


def _rwkv_in(x2, seq, g, mu_prev, mu_next, w_in_bf, layer, w0, w1, w2, a0, a1, a2, k_k, k_a, r_k, vres, v_first):
    rows = x2.shape[0]
    tm, tn = 512, 256
    tiles_per_seq = seq // tm
    ncol = D_MODEL // tn
    has_vres = vres is not None
    w1p, w2p = _pad_lora(w1, w2)
    a1p, a2p = _pad_lora(a1, a2)
    sub = tm // 8
    nsub = rows // 8

    def const2(shape):
        return pl.BlockSpec(shape, lambda i, j: (0, 0))

    def col2(nrow):
        return pl.BlockSpec((nrow, tn), lambda i, j: (0, j))

    def wcol(gi):
        return pl.BlockSpec((None, D_MODEL, tn), lambda i, j: (layer, 0, gi * ncol + j))

    in_specs = [
        pl.BlockSpec((tm, D_MODEL), lambda i, j: (i, 0)),
        pl.BlockSpec((8, D_MODEL), lambda i, j: (jnp.maximum(i * sub - 1, 0), 0)),
        pl.BlockSpec((8, D_MODEL), lambda i, j: (jnp.minimum((i + 1) * sub, nsub - 1), 0)),
        const2((1, D_MODEL)), const2((6, D_MODEL)), const2((6, D_MODEL)),
        wcol(0), wcol(1), wcol(2), wcol(3),
        pl.BlockSpec((2, D_MODEL, LORA_PAD), lambda i, j: (0, 0, 0)),
        pl.BlockSpec((2, LORA_PAD, tn), lambda i, j: (0, 0, j)),
        col2(2),
        pl.BlockSpec((2, D_MODEL, LORA_PAD), lambda i, j: (0, 0, 0)),
        pl.BlockSpec((2, LORA_PAD, tn), lambda i, j: (0, 0, j)),
        col2(2),
        col2(1), col2(1), col2(2),
    ]
    args = [x2, x2, x2, g, mu_prev, mu_next, w_in_bf, w_in_bf, w_in_bf, w_in_bf,
            w1p, w2p, w0, a1p, a2p, a0, k_k, k_a, r_k]
    if has_vres:
        v0, v1, v2 = vres
        v1p, v2p = _pad_lora(v1, v2)
        in_specs += [const2((D_MODEL, LORA_PAD)), col2(LORA_PAD), col2(1),
                     pl.BlockSpec((tm, tn), lambda i, j: (i, j))]
        args += [v1p, v2p, v0, v_first]

    tile = pl.BlockSpec((tm, tn), lambda i, j: (i, j))
    tile2 = pl.BlockSpec((2, tm, tn), lambda i, j: (0, i, j))
    est_spec = pl.BlockSpec((2, tm // CHUNK, tn), lambda i, j: (0, i, j))
    sds_bf2 = jax.ShapeDtypeStruct((2, rows, D_MODEL), BF16)
    sds = jax.ShapeDtypeStruct((rows, D_MODEL), F32)
    scratch = [pltpu.VMEM((4, tm, D_MODEL), BF16), pltpu.VMEM((2, tm, LORA_PAD), BF16),
               pltpu.VMEM((2, tm, LORA_PAD), BF16)]
    if has_vres:
        scratch.append(pltpu.VMEM((tm, LORA_PAD), BF16))
    kern = functools.partial(_rwkv_in_kernel, tm=tm, tn=tn, tiles_per_seq=tiles_per_seq, has_vres=has_vres)
    return pl.pallas_call(
        kern,
        grid=(rows // tm, ncol),
        in_specs=in_specs,
        out_specs=[tile2] * 6 + [est_spec, tile, tile, tile],
        out_shape=[sds_bf2] * 6 + [jax.ShapeDtypeStruct((2, rows // CHUNK, D_MODEL), F32), sds, sds,
                                   jax.ShapeDtypeStruct((rows, D_MODEL), BF16)],
        scratch_shapes=scratch,
        compiler_params=_cparams("parallel", "arbitrary"),
        name="rwkv_in_vres" if has_vres else "rwkv_in",
    )(*args)


def _rwkv_scan_kernel(kkq_ref, rq_ref, ad_ref, kdk_ref, kend_ref, aend_ref, v_ref, est_ref, y_ref, state_ref,
                      *, tb, npair):
    rev = pl.program_id(1) == 1
    step = pl.program_id(3)
    nchunk = tb // CHUNK
    pairs = range(npair)

    @pl.when(step == 0)
    def _():
        state_ref[...] = jnp.zeros_like(state_ref)

    t = lax.broadcasted_iota(jnp.int32, (CHUNK, LANES), 0)
    lane = lax.broadcasted_iota(jnp.int32, (CHUNK, LANES), 1)
    s = lane & (CHUNK - 1)
    d = jnp.where(rev, s - t, t - s)
    strict = d > 0
    incl = d >= 0
    eye = jnp.where(d == 0, 1.0, 0.0).astype(F32)
    first_half = lane < CHUNK
    half_a = jnp.where(first_half, 1.0, 0.0).astype(BF16)
    half_b = jnp.where(first_half, 0.0, 1.0).astype(BF16)
    rr = lax.broadcasted_iota(jnp.int32, (LANES, LANES), 0)
    cc = lax.broadcasted_iota(jnp.int32, (LANES, LANES), 1)
    bd_mask = (rr < CHUNK) == (cc < CHUNK)
    est_rows = lax.broadcasted_iota(jnp.int32, (nchunk, npair * LANES), 0)

    def bd16(y):
        return jnp.concatenate([y * half_a, y * half_b], axis=0)

    def bd32(y):
        return jnp.concatenate([jnp.where(first_half, y, 0.0), jnp.where(first_half, 0.0, y)],
                               axis=0).astype(BF16)

    def mm(a, b):
        return jnp.dot(a, b, preferred_element_type=F32)

    def body(ci, carry):
        first = ci * CHUNKS_PER_TRIP
        cs = [jnp.where(rev, nchunk - 1 - (first + u), first + u) for u in range(CHUNKS_PER_TRIP)]
        jobs = [(pl.ds(pl.multiple_of(c * CHUNK, CHUNK), CHUNK), slice(p * LANES, (p + 1) * LANES))
                for c in cs for p in pairs]
        n = range(len(jobs))
        kkq = [kkq_ref[rw, ln] for rw, ln in jobs]
        rq = [rq_ref[rw, ln] for rw, ln in jobs]
        v = [v_ref[rw, ln] for rw, ln in jobs]
        gram = [lax.dot_general(
            jnp.concatenate([kkq[i], rq[i]], axis=0),
            jnp.concatenate([bd16(ad_ref[jobs[i]]), bd16(kdk_ref[jobs[i]])], axis=0),
            NT_DIMS, preferred_element_type=F32) for i in n]
        a_mat = [jnp.where(strict, gram[i][0:CHUNK, 0:LANES], 0.0) for i in n]
        ak = [jnp.where(strict, gram[i][0:CHUNK, LANES:], 0.0) for i in n]
        bra = [jnp.where(incl, gram[i][CHUNK:, 0:LANES], 0.0).astype(BF16) for i in n]
        brk = [jnp.where(incl, gram[i][CHUNK:, LANES:], 0.0) for i in n]
        early = [mm(jnp.concatenate([ak[i], brk[i]], axis=0).astype(BF16), bd32(v[i])) for i in n]
        a_pow = [mm(a_mat[i].astype(BF16), bd32(a_mat[i])) for i in n]
        inv = [eye - a_mat[i] for i in n]
        for _ in range(4):
            st = [mm(jnp.concatenate([inv[i], a_pow[i]], axis=0).astype(BF16), bd32(a_pow[i])) for i in n]
            inv = [inv[i] + st[i][0:CHUNK] for i in n]
            a_pow = [st[i][CHUNK:] for i in n]
        inv = [inv[i] + mm(inv[i].astype(BF16), bd32(a_pow[i])) for i in n]
        wu = [mm(inv[i].astype(BF16), jnp.concatenate([bd16(kkq[i]), bd32(early[i][0:CHUNK])], axis=1))
              for i in n]
        ry = [mm(bra[i], jnp.concatenate([bd32(wu[i][:, 0:LANES]), bd32(wu[i][:, LANES:])], axis=1))
              for i in n]
        aend = [aend_ref[jobs[i]] for i in n]
        xraw = [lax.dot_general(wu[i][:, 0:LANES].astype(BF16), aend[i], TN_DIMS, preferred_element_type=F32)
                for i in n]
        nraw = [lax.dot_general(jnp.concatenate([v[i], -wu[i][:, LANES:]], axis=0).astype(BF16),
                                jnp.concatenate([kend_ref[jobs[i]], aend[i]], axis=0),
                                TN_DIMS, preferred_element_type=F32) for i in n]
        rqp = [(rq[i].astype(F32) - ry[i][:, 0:LANES]).astype(BF16) for i in n]
        yv = [early[i][CHUNK:] - ry[i][:, LANES:] for i in n]
        xmat = [jnp.where(bd_mask, xraw[i], 0.0).astype(BF16) for i in n]
        nmat = [jnp.where(bd_mask, nraw[i], 0.0) for i in n]
        est_all = est_ref[...]
        state = [state_ref[p] for p in pairs]
        for u in range(CHUNKS_PER_TRIP):
            est_row = jnp.sum(jnp.where(est_rows == cs[u], est_all, 0.0), axis=0, keepdims=True)
            idx = [u * npair + p for p in pairs]
            s16 = [state[p].astype(BF16) for p in pairs]
            ys = [yv[idx[p]] + lax.dot_general(rqp[idx[p]], s16[p], NT_DIMS, preferred_element_type=F32)
                  for p in pairs]
            sx = [mm(s16[p], xmat[idx[p]]) for p in pairs]
            for p in pairs:
                rw, ln = jobs[idx[p]]
                y_ref[rw, ln] = ys[p]
                state[p] = state[p] * est_row[:, ln] - sx[p] + nmat[idx[p]]
        for p in pairs:
            state_ref[p] = state[p]
        return carry

    lax.fori_loop(0, nchunk // CHUNKS_PER_TRIP, body, 0)


def _rwkv_scan(kkq, rq, ad, kdk, kend, aend, est, v, batch, seq):
    tb, lw_lanes = 512, D_MODEL
    nt = seq // tb
    npair = lw_lanes // LANES

    def tmap(i, dr):
        return i + dr * (nt - 1 - 2 * i)

    v_spec = pl.BlockSpec((tb, lw_lanes), lambda b, dr, h, i: (b * nt + tmap(i, dr), h))
    d_spec = pl.BlockSpec((None, tb, lw_lanes), lambda b, dr, h, i: (dr, b * nt + tmap(i, dr), h))
    e_spec = pl.BlockSpec((None, tb // CHUNK, lw_lanes), lambda b, dr, h, i: (dr, b * nt + tmap(i, dr), h))
    kern = functools.partial(_rwkv_scan_kernel, tb=tb, npair=npair)
    return pl.pallas_call(
        kern,
        grid=(batch, 2, D_MODEL // lw_lanes, nt),
        in_specs=[d_spec] * 6 + [v_spec, e_spec],
        out_specs=d_spec,
        out_shape=jax.ShapeDtypeStruct((2, batch * seq, D_MODEL), F32),
        scratch_shapes=[pltpu.VMEM((npair, LANES, LANES), F32)],
        compiler_params=_cparams("parallel", "parallel", "parallel", "arbitrary"),
        name="rwkv_scan",
    )(kkq, rq, ad, kdk, kend, aend, v, est)


def _rwkv_out_kernel(y_ref, bonus_ref, sz_ref, gg_ref, gb_ref, x_ref, w_ref, g_ref, o_ref, yb_ref, *, tm):
    gmat = _group_matrix(LANES)
    inv_n = 1.0 / RWKV_HEAD_DIM
    part_rows = tm // OUT_PARTS

    def groupnorm(p, cb):
        rows = slice(p * part_rows, (p + 1) * part_rows)
        sl = slice(cb * LANES, (cb + 1) * LANES)
        y = y_ref[0, rows, sl] + y_ref[1, rows, sl]
        mu = _group_sum(y, gmat) * inv_n
        yc = y - mu
        var = _group_sum(yc * yc, gmat) * inv_n
        yn = yc * lax.rsqrt(var + GN_EPS) * gg_ref[:, sl] + gb_ref[:, sl]
        yb_ref[p, :, sl] = ((yn + bonus_ref[rows, sl]) * sz_ref[rows, sl].astype(F32)).astype(BF16)

    _project_parts(groupnorm, OUT_PARTS, part_rows, yb_ref, w_ref, g_ref, x_ref, o_ref)


def _rwkv_out(y2, bonus, sz, gn_g, gn_b, x2, w_bf, layer, g):
    rows = x2.shape[0]
    tm = 256
    row_spec = pl.BlockSpec((tm, D_MODEL), lambda i: (i, 0))
    vec = pl.BlockSpec((1, D_MODEL), lambda i: (0, 0))
    return pl.pallas_call(
        functools.partial(_rwkv_out_kernel, tm=tm),
        grid=(rows // tm,),
        in_specs=[pl.BlockSpec((2, tm, D_MODEL), lambda i: (0, i, 0)), row_spec, row_spec, vec, vec, row_spec,
                  pl.BlockSpec((None, D_MODEL, D_MODEL), lambda i: (layer, 0, 0)), vec],
        out_specs=row_spec,
        out_shape=jax.ShapeDtypeStruct((rows, D_MODEL), F32),
        scratch_shapes=[pltpu.VMEM((OUT_PARTS, tm // OUT_PARTS, D_MODEL), BF16)],
        compiler_params=_cparams("parallel"),
        name="rwkv_out",
    )(y2, bonus, sz, gn_g, gn_b, x2, w_bf, g)


def _trunk(x, p, rope):
    batch, seq, _ = x.shape
    x2 = x.reshape(batch * seq, D_MODEL)
    v_first = None
    depth = p["norm_pre"].shape[0]
    for layer in range(depth):
        j = layer // 2
        g_pre = p["norm_pre"][layer][None, :]
        g_post = p["norm_post"][layer][None, :]
        if layer % 2 == 0:
            w_in = p["att_w_in"]
            outs, lses = [], []
            for gidx, (_, dil) in enumerate(ATT_GROUPS):
                q, k, v = _att_in(x2, g_pre, w_in, j, *rope[gidx], seq, gidx, dil)
                o, lse = _attn_group(q, k, v, dil, batch, seq)
                outs.append(o)
                lses.append(lse)
            z = _gate_in(x2, g_pre, w_in, j)
            x2 = _att_out(outs, lses, z, x2, p["att_w_out"], j, g_post)
        else:
            vres = None if j == 0 else (p["rwkv_v0"][j - 1][None, :], p["rwkv_v1"][j - 1], p["rwkv_v2"][j - 1])
            kkq, rq, ad, kdk, kend, aend, est, v, bonus, sz = _rwkv_in(
                x2, seq, g_pre, p["rwkv_mu_prev"][j], p["rwkv_mu_next"][j], p["rwkv_w_in"], j,
                p["rwkv_w0"][j], p["rwkv_w1"][j], p["rwkv_w2"][j],
                p["rwkv_a0"][j], p["rwkv_a1"][j], p["rwkv_a2"][j],
                p["rwkv_k_k"][j][None, :], p["rwkv_k_a"][j][None, :], p["rwkv_r_k"][j].reshape(2, D_MODEL),
                vres, v_first)
            if j == 0:
                v_first = v
            y2 = _rwkv_scan(kkq, rq, ad, kdk, kend, aend, est, v, batch, seq)
            x2 = _rwkv_out(y2, bonus, sz, p["rwkv_gn_g"][j][None, :], p["rwkv_gn_b"][j][None, :], x2,
                           p["rwkv_w_out"], j, g_post)
    return x2.reshape(batch, seq, D_MODEL)


def kernel(x_prompt, x_sample, norm_pre, norm_post, att_w_in, att_w_out, rwkv_mu_prev, rwkv_mu_next, rwkv_w_in, rwkv_w0, rwkv_w1, rwkv_w2, rwkv_a0, rwkv_a1, rwkv_a2, rwkv_v0, rwkv_v1, rwkv_v2, rwkv_k_k, rwkv_k_a, rwkv_r_k, rwkv_gn_g, rwkv_gn_b, rwkv_w_out):
    p = dict(
        norm_pre=norm_pre, norm_post=norm_post,
        att_w_in=att_w_in.astype(BF16), att_w_out=att_w_out.astype(BF16),
        rwkv_mu_prev=rwkv_mu_prev, rwkv_mu_next=rwkv_mu_next, rwkv_w_in=rwkv_w_in.astype(BF16),
        rwkv_w0=rwkv_w0, rwkv_w1=rwkv_w1, rwkv_w2=rwkv_w2,
        rwkv_a0=rwkv_a0, rwkv_a1=rwkv_a1, rwkv_a2=rwkv_a2,
        rwkv_v0=rwkv_v0, rwkv_v1=rwkv_v1, rwkv_v2=rwkv_v2,
        rwkv_k_k=rwkv_k_k, rwkv_k_a=rwkv_k_a, rwkv_r_k=rwkv_r_k,
        rwkv_gn_g=rwkv_gn_g, rwkv_gn_b=rwkv_gn_b, rwkv_w_out=rwkv_w_out.astype(BF16),
    )
    max_seq = max(x_prompt.shape[1], x_sample.shape[1])
    rope = [_rope_tables(max_seq, dil) for _, dil in ATT_GROUPS]
    return (_trunk(x_prompt, p, rope), _trunk(x_sample, p, rope))
```

```python
import functools
import math

import jax
import jax.numpy as jnp
from jax import lax
from jax.experimental import pallas as pl
from jax.experimental.pallas import tpu as pltpu

F32 = jnp.float32
BF16 = jnp.bfloat16

D_MODEL = 2048
LANES = 128
MXU_COLS = 256
ATT_HEAD_DIM = 128
ATT_GROUPS = ((128, 1), (512, 4), (2048, 16))
ATT_HALF = 64
ROPE_THETA = 10000.0
RWKV_HEAD_DIM = 64
LORA_PAD = 128
RMS_EPS = 1e-6
GN_EPS = 64e-5
NEG_INF = -1e30
CHUNK = 64
ATT_IN_ROWS = 512
OUT_PARTS = 2
CHUNKS_PER_TRIP = 2
VMEM_LIMIT_BYTES = 56 * 1024 * 1024

NT_DIMS = (((1,), (1,)), ((), ()))
TN_DIMS = (((0,), (0,)), ((), ()))


def _cparams(*sem):
    return pltpu.CompilerParams(dimension_semantics=sem, vmem_limit_bytes=VMEM_LIMIT_BYTES)


def _rms_scale(x):
    return lax.rsqrt(jnp.mean(x * x, axis=-1, keepdims=True) + RMS_EPS)


def _sigmoid(x):
    return 0.5 * jnp.tanh(0.5 * x) + 0.5


def _split_dot(lhs_bf, x):
    hi = x.astype(BF16)
    lo = (x - hi.astype(F32)).astype(BF16)
    return (jnp.dot(lhs_bf, hi, preferred_element_type=F32)
            + jnp.dot(lhs_bf, lo, preferred_element_type=F32))


def _group_sum(x, gmat):
    hi = x.astype(BF16)
    lo = (x - hi.astype(F32)).astype(BF16)
    return (jnp.dot(hi, gmat, preferred_element_type=F32)
            + jnp.dot(lo, gmat, preferred_element_type=F32))


def _group_matrix(n):
    r = lax.broadcasted_iota(jnp.int32, (n, n), 0) // RWKV_HEAD_DIM
    c = lax.broadcasted_iota(jnp.int32, (n, n), 1) // RWKV_HEAD_DIM
    return jnp.where(r == c, 1.0, 0.0).astype(BF16)


def _rope_table_kernel(invf_ref, cos_ref, sin_ref, *, dil):
    rows = cos_ref.shape[0]
    base = pl.program_id(0) * rows
    local = lax.broadcasted_iota(jnp.int32, (rows, LANES), 0)
    per_res = rows // dil
    pos = (base + (local % per_res) * dil + local // per_res).astype(F32)
    ang = pos * invf_ref[...]
    lane = lax.broadcasted_iota(jnp.int32, (rows, LANES), 1)
    s = jnp.sin(ang)
    cos_ref[...] = jnp.cos(ang)
    sin_ref[...] = jnp.where(lane < ATT_HEAD_DIM // 2, -s, s)


def _rope_tables(seq, dil):
    half = ATT_HEAD_DIM // 2
    inv_freq = 1.0 / (ROPE_THETA ** (jnp.arange(half, dtype=F32) * 2.0 / ATT_HEAD_DIM))
    invf = jnp.concatenate([inv_freq, inv_freq])[None, :]
    rows = ATT_IN_ROWS
    return pl.pallas_call(
        functools.partial(_rope_table_kernel, dil=dil),
        grid=(seq // rows,),
        in_specs=[pl.BlockSpec((1, LANES), lambda i: (0, 0))],
        out_specs=[pl.BlockSpec((rows, LANES), lambda i: (i, 0))] * 2,
        out_shape=[jax.ShapeDtypeStruct((seq, LANES), F32)] * 2,
        compiler_params=_cparams("arbitrary"),
        name=f"rope_table_d{dil}",
    )(invf)


def _att_in_kernel(x_ref, g_ref, wq_ref, wk_ref, wv_ref, cos_ref, sin_ref, q_ref, k_ref, v_ref, h_ref, *hs_ref,
                   tm, tn, dil, scale):
    per_res = tm // dil

    @pl.when(pl.program_id(1) == 0)
    def _():
        x = x_ref[...]
        h = x * _rms_scale(x) * g_ref[...]
        if dil == 1:
            h_ref[...] = h.astype(BF16)
        else:
            for cb in range(D_MODEL // LANES):
                sl = slice(cb * LANES, (cb + 1) * LANES)
                hs_ref[0][cb] = h[:, sl]
                for r in range(dil):
                    h_ref[r * per_res:(r + 1) * per_res, sl] = (
                        hs_ref[0][cb, pl.ds(r, per_res, stride=dil), :].astype(BF16))

    h = h_ref[...]
    cos = cos_ref[...]
    sin = sin_ref[...]
    for w_ref, o_ref, rope, sc in ((wq_ref, q_ref, True, scale), (wk_ref, k_ref, True, None),
                                   (wv_ref, v_ref, False, None)):
        for cb in range(tn // MXU_COLS):
            acc = jnp.dot(h, w_ref[:, cb * MXU_COLS:(cb + 1) * MXU_COLS], preferred_element_type=F32)
            for hh in range(MXU_COLS // LANES):
                sl = slice(cb * MXU_COLS + hh * LANES, cb * MXU_COLS + (hh + 1) * LANES)
                t = acc[:, hh * LANES:(hh + 1) * LANES]
                if rope:
                    t = t * cos + pltpu.roll(t, ATT_HEAD_DIM // 2, axis=1) * sin
                if sc is not None:
                    t = t * sc
                t = t.astype(BF16)
                for r in range(dil):
                    o_ref[r, :, sl] = t[r * per_res:(r + 1) * per_res]


def _att_in(x2, g, w_bf, layer, cos_t, sin_t, seq, gidx, dil):
    rows = x2.shape[0]
    tm, tn = ATT_IN_ROWS, 1024
    tiles_per_seq = seq // tm
    ncol = D_MODEL // tn
    kern = functools.partial(_att_in_kernel, tm=tm, tn=tn, dil=dil, scale=ATT_HEAD_DIM ** -0.5)

    def wcol(part):
        return pl.BlockSpec((None, D_MODEL, tn), lambda i, j: (layer, 0, (gidx * 3 + part) * ncol + j))

    out_spec = pl.BlockSpec((dil, tm // dil, tn), lambda i, j: (0, i, j))
    out_sds = jax.ShapeDtypeStruct((dil, rows // dil, D_MODEL), BF16)
    scratch = [pltpu.VMEM((tm, D_MODEL), BF16)]
    if dil > 1:
        scratch.append(pltpu.VMEM((D_MODEL // LANES, tm, LANES), F32))
    return pl.pallas_call(
        kern,
        grid=(rows // tm, ncol),
        in_specs=[
            pl.BlockSpec((tm, D_MODEL), lambda i, j: (i, 0)),
            pl.BlockSpec((1, D_MODEL), lambda i, j: (0, 0)),
            wcol(0), wcol(1), wcol(2),
            pl.BlockSpec((tm, LANES), lambda i, j: (i % tiles_per_seq, 0)),
            pl.BlockSpec((tm, LANES), lambda i, j: (i % tiles_per_seq, 0)),
        ],
        out_specs=[out_spec] * 3,
        out_shape=[out_sds] * 3,
        scratch_shapes=scratch,
        compiler_params=_cparams("parallel", "arbitrary"),
        name=f"att_in_d{dil}",
    )(x2, g, w_bf, w_bf, w_bf, cos_t, sin_t)


def _gate_in_kernel(x_ref, g_ref, w_ref, o_ref):
    x = x_ref[...]
    h = (x * _rms_scale(x) * g_ref[...]).astype(BF16)
    o_ref[...] = jnp.dot(h, w_ref[...], preferred_element_type=F32).astype(BF16)


def _gate_in(x2, g, w_bf, layer):
    rows = x2.shape[0]
    tm = 1024
    col0 = 3 * len(ATT_GROUPS)
    return pl.pallas_call(
        _gate_in_kernel,
        grid=(rows // tm,),
        in_specs=[
            pl.BlockSpec((tm, D_MODEL), lambda i: (i, 0)),
            pl.BlockSpec((1, D_MODEL), lambda i: (0, 0)),
            pl.BlockSpec((None, D_MODEL, D_MODEL), lambda i: (layer, 0, col0)),
        ],
        out_specs=pl.BlockSpec((tm, D_MODEL), lambda i: (i, 0)),
        out_shape=jax.ShapeDtypeStruct((rows, D_MODEL), BF16),
        compiler_params=_cparams("parallel"),
        name="att_gate_in",
    )(x2, g, w_bf)


def _attn_kernel(q_ref, kp_ref, kc_ref, kn_ref, vp_ref, vc_ref, vn_ref, o_ref, lse_ref, *, bq, sb, sub_len, nh):
    i = pl.program_id(3)
    nk = sb + 2 * ATT_HALF
    nsub = bq // sb
    ii = lax.broadcasted_iota(jnp.int32, (sb, nk), 0)
    jj = lax.broadcasted_iota(jnp.int32, (sb, nk), 1)
    rel = jj - ii
    in_band = (rel >= 0) & (rel <= 2 * ATT_HALF)
    valid = []
    for u in range(nsub):
        kpos = i * bq + u * sb - ATT_HALF + jj
        valid.append(in_band & (kpos >= 0) & (kpos < sub_len))
    lane = lax.broadcasted_iota(jnp.int32, (sb, LANES), 1)
    units = [(hh, u) for hh in range(nh) for u in range(nsub)]

    def scores(hh, u):
        sl = slice(hh * LANES, (hh + 1) * LANES)
        kcat = jnp.concatenate([kp_ref[:, sl], kc_ref[:, sl], kn_ref[:, sl]], axis=0)
        s = lax.dot_general(q_ref[u * sb:(u + 1) * sb, sl], kcat[u * sb:u * sb + nk], NT_DIMS,
                            preferred_element_type=F32)
        return jnp.where(valid[u], s, NEG_INF)

    def finish(hh, u, s, lse_acc):
        sl = slice(hh * LANES, (hh + 1) * LANES)
        vcat = jnp.concatenate([vp_ref[:, sl], vc_ref[:, sl], vn_ref[:, sl]], axis=0)
        m = jnp.max(s, axis=-1, keepdims=True)
        p = jnp.exp(s - m)
        l = jnp.sum(p, axis=-1, keepdims=True)
        o = jnp.dot(p.astype(BF16), vcat[u * sb:u * sb + nk], preferred_element_type=F32)
        o_ref[u * sb:(u + 1) * sb, sl] = (o / l).astype(BF16)
        lse_acc[u] = jnp.where(lane == hh, m + jnp.log(l), lse_acc[u])

    lse_acc = [jnp.zeros((sb, LANES), F32) for _ in range(nsub)]
    s_prev = scores(*units[0])
    for k in range(1, len(units)):
        s_next = scores(*units[k])
        finish(*units[k - 1], s_prev, lse_acc)
        s_prev = s_next
    finish(*units[-1], s_prev, lse_acc)
    for u in range(nsub):
        lse_ref[u * sb:(u + 1) * sb, :] = lse_acc[u]


def _attn_group(q, k, v, dil, batch, seq):
    sub_len = seq // dil
    bq = min(512, sub_len)
    sb = min(128, bq)
    hw = D_MODEL
    nh = hw // LANES
    hblocks = D_MODEL // hw
    nqb = sub_len // bq
    halo_per_q = bq // ATT_HALF
    n_halo = sub_len // ATT_HALF

    cur = pl.BlockSpec((None, bq, hw), lambda b, r, h, i: (r, b * nqb + i, h))
    prev = pl.BlockSpec((None, ATT_HALF, hw),
                        lambda b, r, h, i: (r, b * n_halo + jnp.maximum(i * halo_per_q - 1, 0), h))
    nxt = pl.BlockSpec((None, ATT_HALF, hw),
                       lambda b, r, h, i: (r, b * n_halo + jnp.minimum((i + 1) * halo_per_q, n_halo - 1), h))
    lse_spec = pl.BlockSpec((None, bq, LANES), lambda b, r, h, i: (r, b * nqb + i, h))
    rows = batch * sub_len
    kern = functools.partial(_attn_kernel, bq=bq, sb=sb, sub_len=sub_len, nh=nh)
    return pl.pallas_call(
        kern,
        grid=(batch, dil, hblocks, nqb),
        in_specs=[cur, prev, cur, nxt, prev, cur, nxt],
        out_specs=[cur, lse_spec],
        out_shape=[jax.ShapeDtypeStruct((dil, rows, D_MODEL), BF16),
                   jax.ShapeDtypeStruct((dil, rows, hblocks * LANES), F32)],
        compiler_params=_cparams("parallel", "parallel", "parallel", "arbitrary"),
        name=f"attn_d{dil}",
    )(q, k, k, k, v, v, v)


def _project_parts(prologue, nparts, part_rows, y_ref, w_ref, g_ref, x_ref, o_ref):
    ncb = D_MODEL // LANES
    cb_per_k = MXU_COLS // LANES

    def finish(out, p):
        rows = slice(p * part_rows, (p + 1) * part_rows)
        o_ref[rows, :] = x_ref[rows, :] + out * _rms_scale(out) * g_ref[...]

    for cb in range(ncb):
        prologue(0, cb)
    for p in range(nparts):
        acc = None
        for kc in range(D_MODEL // MXU_COLS):
            ks = slice(kc * MXU_COLS, (kc + 1) * MXU_COLS)
            part = jnp.dot(y_ref[p, :, ks], w_ref[ks, :], preferred_element_type=F32)
            acc = part if acc is None else acc + part
            if p + 1 < nparts:
                for cb in range(kc * cb_per_k, (kc + 1) * cb_per_k):
                    prologue(p + 1, cb)
        finish(acc, p)


def _att_out_kernel(o0_ref, o1_ref, o2_ref, l0_ref, l1_ref, l2_ref, z_ref, x_ref, w_ref, g_ref, o_ref,
                    so1, so2, sl1, sl2, y_ref, *, tm, dils, heads_per_block):
    nlb = l0_ref.shape[-1] // LANES
    for src, dst, d in ((l1_ref, sl1, dils[1]), (l2_ref, sl2, dils[2])):
        for lb in range(nlb):
            for r in range(d):
                dst[lb, pl.ds(r, tm // d, stride=d), :] = src[r, :, lb * LANES:(lb + 1) * LANES]
    wts = []
    for lb in range(nlb):
        l0, l1, l2 = l0_ref[0, :, lb * LANES:(lb + 1) * LANES], sl1[lb], sl2[lb]
        m = jnp.maximum(jnp.maximum(l0, l1), l2)
        e0, e1, e2 = jnp.exp(l0 - m), jnp.exp(l1 - m), jnp.exp(l2 - m)
        inv = 1.0 / (e0 + e1 + e2)
        wts.append((e0 * inv, e1 * inv, e2 * inv))
    for cb in range(D_MODEL // LANES):
        sl = slice(cb * LANES, (cb + 1) * LANES)
        for src, dst, d in ((o1_ref, so1, dils[1]), (o2_ref, so2, dils[2])):
            for r in range(d):
                dst[cb, pl.ds(r, tm // d, stride=d), :] = src[r, :, sl].astype(F32)
    part_rows = tm // OUT_PARTS

    def combine(p, cb):
        rows = slice(p * part_rows, (p + 1) * part_rows)
        sl = slice(cb * LANES, (cb + 1) * LANES)
        w0, w1, w2 = wts[cb // heads_per_block]
        hl = cb % heads_per_block
        bc = lambda w: jnp.broadcast_to(w[rows, hl:hl + 1], (part_rows, LANES))
        o = (bc(w0) * o0_ref[0, rows, sl].astype(F32) + bc(w1) * so1[cb, rows, :]
             + bc(w2) * so2[cb, rows, :])
        z = z_ref[rows, sl].astype(F32)
        y_ref[p, :, sl] = (o * (z * _sigmoid(z))).astype(BF16)

    _project_parts(combine, OUT_PARTS, part_rows, y_ref, w_ref, g_ref, x_ref, o_ref)


def _att_out(outs, lses, z, x2, w_bf, layer, g):
    rows = x2.shape[0]
    tm = 512
    dils = tuple(d for _, d in ATT_GROUPS)
    lse_w = lses[0].shape[-1]
    nlb = lse_w // LANES
    row_spec = pl.BlockSpec((tm, D_MODEL), lambda i: (i, 0))

    def res_spec(d, width):
        return pl.BlockSpec((d, tm // d, width), lambda i: (0, i, 0))

    kern = functools.partial(_att_out_kernel, tm=tm, dils=dils, heads_per_block=D_MODEL // LANES // nlb)
    return pl.pallas_call(
        kern,
        grid=(rows // tm,),
        in_specs=[res_spec(d, D_MODEL) for d in dils] + [res_spec(d, lse_w) for d in dils] + [
            row_spec, row_spec,
            pl.BlockSpec((None, D_MODEL, D_MODEL), lambda i: (layer, 0, 0), pipeline_mode=pl.Buffered(1)),
            pl.BlockSpec((1, D_MODEL), lambda i: (0, 0)),
        ],
        out_specs=row_spec,
        out_shape=jax.ShapeDtypeStruct((rows, D_MODEL), F32),
        scratch_shapes=[pltpu.VMEM((D_MODEL // LANES, tm, LANES), F32)] * 2
        + [pltpu.VMEM((nlb, tm, LANES), F32)] * 2 + [pltpu.VMEM((OUT_PARTS, tm // OUT_PARTS, D_MODEL), BF16)],
        compiler_params=_cparams("parallel"),
        name="att_out",
    )(*outs, *lses, z, x2, w_bf, g)


def _chunk_tri(n, reverse):
    t = lax.broadcasted_iota(jnp.int32, (n, n), 0)
    s = lax.broadcasted_iota(jnp.int32, (n, n), 1)
    same = (t // CHUNK) == (s // CHUNK)
    order = (s >= t) if reverse else (s <= t)
    return jnp.where(same & order, 1.0, 0.0).astype(BF16)


def _rwkv_in_kernel(*refs, tm, tn, tiles_per_seq, has_vres):
    (x_ref, xp_ref, xn_ref, g_ref, mup_ref, mun_ref,
     wr_ref, wk_ref, wv_ref, wz_ref,
     w1_ref, w2_ref, w0_ref, a1_ref, a2_ref, a0_ref,
     kk_ref, ka_ref, rk_ref) = refs[:19]
    pos = 19
    if has_vres:
        v1_ref, v2_ref, v0_ref, vf_ref = refs[pos:pos + 4]
        pos += 4
    (kkq_out, rq_out, ad_out, kdk_out, kend_out, aend_out, est_out, v_out, bonus_out, sz_out) = refs[pos:pos + 10]
    pos += 10
    xs_ref, hw_ref, ha_ref = refs[pos:pos + 3]
    hv_ref = refs[pos + 3] if has_vres else None

    i = pl.program_id(0)
    j = pl.program_id(1)
    mix_slot = {0: 0, 2: 1, 3: 2, 5: 3}

    @pl.when(j == 0)
    def _():
        t_in_seq = i % tiles_per_seq
        keep_prev = jnp.where(t_in_seq == 0, 0.0, 1.0).astype(F32)
        keep_next = jnp.where(t_in_seq == tiles_per_seq - 1, 0.0, 1.0).astype(F32)
        sx = _rms_scale(x_ref[...])
        xp = xp_ref[7:8, :]
        xn = xn_ref[0:1, :]
        sp = _rms_scale(xp) * keep_prev
        sn = _rms_scale(xn) * keep_next
        cw = 512
        row = lax.broadcasted_iota(jnp.int32, (tm, cw), 0)
        hw_acc = [jnp.zeros((tm, LORA_PAD), F32) for _ in range(2)]
        ha_acc = [jnp.zeros((tm, LORA_PAD), F32) for _ in range(2)]
        hv_acc = jnp.zeros((tm, LORA_PAD), F32)
        for cb in range(D_MODEL // cw):
            sl = slice(cb * cw, (cb + 1) * cw)
            g = g_ref[:, sl]
            h = x_ref[:, sl] * sx * g
            hp_row = xp[:, sl] * sp * g
            hn_row = xn[:, sl] * sn * g
            h_prev = jnp.where(row == 0, hp_row, pltpu.roll(h, 1, axis=0))
            h_next = jnp.where(row == tm - 1, hn_row, pltpu.roll(h, tm - 1, axis=0))
            h16 = h.astype(BF16)
            dp = (h_prev - h).astype(BF16)
            dn = (h_next - h).astype(BF16)
            mixes = {}
            for t in range(6):
                mixes[t] = h16 + dp * mup_ref[t:t + 1, sl].astype(BF16) + dn * mun_ref[t:t + 1, sl].astype(BF16)
                if t in mix_slot:
                    xs_ref[mix_slot[t], :, sl] = mixes[t]
            for c in range(2):
                hw_acc[c] = hw_acc[c] + jnp.dot(mixes[1], w1_ref[c, sl, :], preferred_element_type=F32)
                ha_acc[c] = ha_acc[c] + jnp.dot(mixes[4], a1_ref[c, sl, :], preferred_element_type=F32)
            if has_vres:
                hv_acc = hv_acc + jnp.dot(mixes[3], v1_ref[sl, :], preferred_element_type=F32)
        for c in range(2):
            hw_ref[c] = jnp.tanh(hw_acc[c]).astype(BF16)
            ha_ref[c] = ha_acc[c].astype(BF16)
        if has_vres:
            hv_ref[...] = hv_acc.astype(BF16)

    wl = [w0_ref[c:c + 1, :] + jnp.dot(hw_ref[c], w2_ref[c], preferred_element_type=F32) for c in range(2)]
    al = [a0_ref[c:c + 1, :] + jnp.dot(ha_ref[c], a2_ref[c], preferred_element_type=F32) for c in range(2)]
    if has_vres:
        gl = v0_ref[...] + jnp.dot(hv_ref[...], v2_ref[...], preferred_element_type=F32)
    k = jnp.dot(xs_ref[1], wk_ref[...], preferred_element_type=F32)
    r = jnp.dot(xs_ref[0], wr_ref[...], preferred_element_type=F32)
    z = jnp.dot(xs_ref[3], wz_ref[...], preferred_element_type=F32)
    v = jnp.dot(xs_ref[2], wv_ref[...], preferred_element_type=F32)
    if has_vres:
        v = v + (vf_ref[...] - v) * _sigmoid(gl)
    gmat = _group_matrix(tn)
    kk = k * kk_ref[...]
    kk = kk * lax.rsqrt(jnp.maximum(_group_sum(kk * kk, gmat), 1e-24))
    k_a = ka_ref[...]
    rk_acc = jnp.zeros((tm, tn), F32)
    nchunk = tm // CHUNK
    half = 256
    for c in range(2):
        lw = -math.exp(-0.5) * _sigmoid(wl[c])
        a = _sigmoid(al[c])
        kd = k * (1.0 + (a - 1.0) * k_a)
        rk_acc = rk_acc + r * kd * rk_ref[c:c + 1, :]
        tri = _chunk_tri(half, reverse=(c == 1))
        g = jnp.concatenate([_split_dot(tri, lw[hh * half:(hh + 1) * half]) for hh in range(tm // half)], axis=0)
        g3 = g.reshape(nchunk, CHUNK, tn)
        last = 0 if c == 1 else CHUNK - 1
        g_tot = g3[:, last:last + 1, :]
        e_tot = jnp.exp(g_tot)
        est_out[c] = e_tot.reshape(nchunk, tn)
        e_q = jnp.exp(g)
        e_qp = jnp.exp(g - lw)
        e_k = jnp.exp(-g)
        e_end = (e_k.reshape(nchunk, CHUNK, tn) * e_tot).reshape(tm, tn)
        kka = kk * a
        kkq_out[c] = (kk * e_qp).astype(BF16)
        rq_out[c] = (r * e_q).astype(BF16)
        ad_out[c] = (kka * e_k).astype(BF16)
        kdk_out[c] = (kd * e_k).astype(BF16)
        kend_out[c] = (kd * e_end).astype(BF16)
        aend_out[c] = (kka * e_end).astype(BF16)
    v_out[...] = v
    bonus_out[...] = _group_sum(rk_acc, gmat) * v
    sz_out[...] = (z * _sigmoid(z)).astype(BF16)


def _pad_lora(w1, w2):
    rank = w1.shape[-1]
    pad1 = [(0, 0)] * (w1.ndim - 1) + [(0, LORA_PAD - rank)]
    pad2 = [(0, 0)] * (w2.ndim - 2) + [(0, LORA_PAD - rank), (0, 0)]
    return jnp.pad(w1, pad1).astype(BF16), jnp.pad(w2, pad2).astype(BF16)


def _rwkv_in(x2, seq, g, mu_prev, mu_next, w_in_bf, layer, w0, w1, w2, a0, a1, a2, k_k, k_a, r_k, vres, v_first):
    rows = x2.shape[0]
    tm, tn = 512, 256
    tiles_per_seq = seq // tm
    ncol = D_MODEL // tn
    has_vres = vres is not None
    w1p, w2p = _pad_lora(w1, w2)
    a1p, a2p = _pad_lora(a1, a2)
    sub = tm // 8
    nsub = rows // 8

    def const2(shape):
        return pl.BlockSpec(shape, lambda i, j: (0, 0))

    def col2(nrow):
        return pl.BlockSpec((nrow, tn), lambda i, j: (0, j))

    def wcol(gi):
        return pl.BlockSpec((None, D_MODEL, tn), lambda i, j: (layer, 0, gi * ncol + j))

    in_specs = [
        pl.BlockSpec((tm, D_MODEL), lambda i, j: (i, 0)),
        pl.BlockSpec((8, D_MODEL), lambda i, j: (jnp.maximum(i * sub - 1, 0), 0)),
        pl.BlockSpec((8, D_MODEL), lambda i, j: (jnp.minimum((i + 1) * sub, nsub - 1), 0)),
        const2((1, D_MODEL)), const2((6, D_MODEL)), const2((6, D_MODEL)),
        wcol(0), wcol(1), wcol(2), wcol(3),
        pl.BlockSpec((2, D_MODEL, LORA_PAD), lambda i, j: (0, 0, 0)),
        pl.BlockSpec((2, LORA_PAD, tn), lambda i, j: (0, 0, j)),
        col2(2),
        pl.BlockSpec((2, D_MODEL, LORA_PAD), lambda i, j: (0, 0, 0)),
        pl.BlockSpec((2, LORA_PAD, tn), lambda i, j: (0, 0, j)),
        col2(2),
        col2(1), col2(1), col2(2),
    ]
    args = [x2, x2, x2, g, mu_prev, mu_next, w_in_bf, w_in_bf, w_in_bf, w_in_bf,
            w1p, w2p, w0, a1p, a2p, a0, k_k, k_a, r_k]
    if has_vres:
        v0, v1, v2 = vres
        v1p, v2p = _pad_lora(v1, v2)
        in_specs += [const2((D_MODEL, LORA_PAD)), col2(LORA_PAD), col2(1),
                     pl.BlockSpec((tm, tn), lambda i, j: (i, j))]
        args += [v1p, v2p, v0, v_first]

    tile = pl.BlockSpec((tm, tn), lambda i, j: (i, j))
    tile2 = pl.BlockSpec((2, tm, tn), lambda i, j: (0, i, j))
    est_spec = pl.BlockSpec((2, tm // CHUNK, tn), lambda i, j: (0, i, j))
    sds_bf2 = jax.ShapeDtypeStruct((2, rows, D_MODEL), BF16)
    sds = jax.ShapeDtypeStruct((rows, D_MODEL), F32)
    scratch = [pltpu.VMEM((4, tm, D_MODEL), BF16), pltpu.VMEM((2, tm, LORA_PAD), BF16),
               pltpu.VMEM((2, tm, LORA_PAD), BF16)]
    if has_vres:
        scratch.append(pltpu.VMEM((tm, LORA_PAD), BF16))
    kern = functools.partial(_rwkv_in_kernel, tm=tm, tn=tn, tiles_per_seq=tiles_per_seq, has_vres=has_vres)
    return pl.pallas_call(
        kern,
        grid=(rows // tm, ncol),
        in_specs=in_specs,
        out_specs=[tile2] * 6 + [est_spec, tile, tile, tile],
        out_shape=[sds_bf2] * 6 + [jax.ShapeDtypeStruct((2, rows // CHUNK, D_MODEL), F32), sds, sds,
                                   jax.ShapeDtypeStruct((rows, D_MODEL), BF16)],
        scratch_shapes=scratch,
        compiler_params=_cparams("parallel", "arbitrary"),
        name="rwkv_in_vres" if has_vres else "rwkv_in",
    )(*args)


def _rwkv_scan_kernel(kkq_ref, rq_ref, ad_ref, kdk_ref, kend_ref, aend_ref, v_ref, est_ref, y_ref, state_ref,
                      *, tb, npair):
    rev = pl.program_id(1) == 1
    step = pl.program_id(3)
    nchunk = tb // CHUNK
    pairs = range(npair)

    @pl.when(step == 0)
    def _():
        state_ref[...] = jnp.zeros_like(state_ref)

    t = lax.broadcasted_iota(jnp.int32, (CHUNK, LANES), 0)
    lane = lax.broadcasted_iota(jnp.int32, (CHUNK, LANES), 1)
    s = lane & (CHUNK - 1)
    d = jnp.where(rev, s - t, t - s)
    strict = d > 0
    incl = d >= 0
    eye = jnp.where(d == 0, 1.0, 0.0).astype(F32)
    first_half = lane < CHUNK
    half_a = jnp.where(first_half, 1.0, 0.0).astype(BF16)
    half_b = jnp.where(first_half, 0.0, 1.0).astype(BF16)
    rr = lax.broadcasted_iota(jnp.int32, (LANES, LANES), 0)
    cc = lax.broadcasted_iota(jnp.int32, (LANES, LANES), 1)
    bd_mask = (rr < CHUNK) == (cc < CHUNK)
    est_rows = lax.broadcasted_iota(jnp.int32, (nchunk, npair * LANES), 0)

    def bd16(y):
        return jnp.concatenate([y * half_a, y * half_b], axis=0)

    def bd32(y):
        return jnp.concatenate([jnp.where(first_half, y, 0.0), jnp.where(first_half, 0.0, y)],
                               axis=0).astype(BF16)

    def mm(a, b):
        return jnp.dot(a, b, preferred_element_type=F32)

    def body(ci, carry):
        first = ci * CHUNKS_PER_TRIP
        cs = [jnp.where(rev, nchunk - 1 - (first + u), first + u) for u in range(CHUNKS_PER_TRIP)]
        jobs = [(pl.ds(pl.multiple_of(c * CHUNK, CHUNK), CHUNK), slice(p * LANES, (p + 1) * LANES))
                for c in cs for p in pairs]
        n = range(len(jobs))
        kkq = [kkq_ref[rw, ln] for rw, ln in jobs]
        rq = [rq_ref[rw, ln] for rw, ln in jobs]
        v = [v_ref[rw, ln] for rw, ln in jobs]
        gram = [lax.dot_general(
            jnp.concatenate([kkq[i], rq[i]], axis=0),
            jnp.concatenate([bd16(ad_ref[jobs[i]]), bd16(kdk_ref[jobs[i]])], axis=0),
            NT_DIMS, preferred_element_type=F32) for i in n]
        a_mat = [jnp.where(strict, gram[i][0:CHUNK, 0:LANES], 0.0) for i in n]
        ak = [jnp.where(strict, gram[i][0:CHUNK, LANES:], 0.0) for i in n]
        bra = [jnp.where(incl, gram[i][CHUNK:, 0:LANES], 0.0).astype(BF16) for i in n]
        brk = [jnp.where(incl, gram[i][CHUNK:, LANES:], 0.0) for i in n]
        early = [mm(jnp.concatenate([ak[i], brk[i]], axis=0).astype(BF16), bd32(v[i])) for i in n]
        a_pow = [mm(a_mat[i].astype(BF16), bd32(a_mat[i])) for i in n]
        inv = [eye - a_mat[i] for i in n]
        for _ in range(4):
            st = [mm(jnp.concatenate([inv[i], a_pow[i]], axis=0).astype(BF16), bd32(a_pow[i])) for i in n]
            inv = [inv[i] + st[i][0:CHUNK] for i in n]
            a_pow = [st[i][CHUNK:] for i in n]
        inv = [inv[i] + mm(inv[i].astype(BF16), bd32(a_pow[i])) for i in n]
        wu = [mm(inv[i].astype(BF16), jnp.concatenate([bd16(kkq[i]), bd32(early[i][0:CHUNK])], axis=1))
              for i in n]
        ry = [mm(bra[i], jnp.concatenate([bd32(wu[i][:, 0:LANES]), bd32(wu[i][:, LANES:])], axis=1))
              for i in n]
        aend = [aend_ref[jobs[i]] for i in n]
        xraw = [lax.dot_general(wu[i][:, 0:LANES].astype(BF16), aend[i], TN_DIMS, preferred_element_type=F32)
                for i in n]
        nraw = [lax.dot_general(jnp.concatenate([v[i], -wu[i][:, LANES:]], axis=0).astype(BF16),
                                jnp.concatenate([kend_ref[jobs[i]], aend[i]], axis=0),
                                TN_DIMS, preferred_element_type=F32) for i in n]
        rqp = [(rq[i].astype(F32) - ry[i][:, 0:LANES]).astype(BF16) for i in n]
        yv = [early[i][CHUNK:] - ry[i][:, LANES:] for i in n]
        xmat = [jnp.where(bd_mask, xraw[i], 0.0).astype(BF16) for i in n]
        nmat = [jnp.where(bd_mask, nraw[i], 0.0) for i in n]
        est_all = est_ref[...]
        state = [state_ref[p] for p in pairs]
        for u in range(CHUNKS_PER_TRIP):
            est_row = jnp.sum(jnp.where(est_rows == cs[u], est_all, 0.0), axis=0, keepdims=True)
            idx = [u * npair + p for p in pairs]
            s16 = [state[p].astype(BF16) for p in pairs]
            ys = [yv[idx[p]] + lax.dot_general(rqp[idx[p]], s16[p], NT_DIMS, preferred_element_type=F32)
                  for p in pairs]
            sx = [mm(s16[p], xmat[idx[p]]) for p in pairs]
            for p in pairs:
                rw, ln = jobs[idx[p]]
                y_ref[rw, ln] = ys[p]
                state[p] = state[p] * est_row[:, ln] - sx[p] + nmat[idx[p]]
        for p in pairs:
            state_ref[p] = state[p]
        return carry

    lax.fori_loop(0, nchunk // CHUNKS_PER_TRIP, body, 0)


def _rwkv_scan(kkq, rq, ad, kdk, kend, aend, est, v, batch, seq):
    tb, lw_lanes = 512, D_MODEL
    nt = seq // tb
    npair = lw_lanes // LANES

    def tmap(i, dr):
        return i + dr * (nt - 1 - 2 * i)

    v_spec = pl.BlockSpec((tb, lw_lanes), lambda b, dr, h, i: (b * nt + tmap(i, dr), h))
    d_spec = pl.BlockSpec((None, tb, lw_lanes), lambda b, dr, h, i: (dr, b * nt + tmap(i, dr), h))
    e_spec = pl.BlockSpec((None, tb // CHUNK, lw_lanes), lambda b, dr, h, i: (dr, b * nt + tmap(i, dr), h))
    kern = functools.partial(_rwkv_scan_kernel, tb=tb, npair=npair)
    return pl.pallas_call(
        kern,
        grid=(batch, 2, D_MODEL // lw_lanes, nt),
        in_specs=[d_spec] * 6 + [v_spec, e_spec],
        out_specs=d_spec,
        out_shape=jax.ShapeDtypeStruct((2, batch * seq, D_MODEL), F32),
        scratch_shapes=[pltpu.VMEM((npair, LANES, LANES), F32)],
        compiler_params=_cparams("parallel", "parallel", "parallel", "arbitrary"),
        name="rwkv_scan",
    )(kkq, rq, ad, kdk, kend, aend, v, est)


def _rwkv_out_kernel(y_ref, bonus_ref, sz_ref, gg_ref, gb_ref, x_ref, w_ref, g_ref, o_ref, yb_ref, *, tm):
    gmat = _group_matrix(LANES)
    inv_n = 1.0 / RWKV_HEAD_DIM
    part_rows = tm // OUT_PARTS

    def groupnorm(p, cb):
        rows = slice(p * part_rows, (p + 1) * part_rows)
        sl = slice(cb * LANES, (cb + 1) * LANES)
        y = y_ref[0, rows, sl] + y_ref[1, rows, sl]
        mu = _group_sum(y, gmat) * inv_n
        yc = y - mu
        var = _group_sum(yc * yc, gmat) * inv_n
        yn = yc * lax.rsqrt(var + GN_EPS) * gg_ref[:, sl] + gb_ref[:, sl]
        yb_ref[p, :, sl] = ((yn + bonus_ref[rows, sl]) * sz_ref[rows, sl].astype(F32)).astype(BF16)

    _project_parts(groupnorm, OUT_PARTS, part_rows, yb_ref, w_ref, g_ref, x_ref, o_ref)


def _rwkv_out(y2, bonus, sz, gn_g, gn_b, x2, w_bf, layer, g):
    rows = x2.shape[0]
    tm = 256
    row_spec = pl.BlockSpec((tm, D_MODEL), lambda i: (i, 0))
    vec = pl.BlockSpec((1, D_MODEL), lambda i: (0, 0))
    return pl.pallas_call(
        functools.partial(_rwkv_out_kernel, tm=tm),
        grid=(rows // tm,),
        in_specs=[pl.BlockSpec((2, tm, D_MODEL), lambda i: (0, i, 0)), row_spec, row_spec, vec, vec, row_spec,
                  pl.BlockSpec((None, D_MODEL, D_MODEL), lambda i: (layer, 0, 0)), vec],
        out_specs=row_spec,
        out_shape=jax.ShapeDtypeStruct((rows, D_MODEL), F32),
        scratch_shapes=[pltpu.VMEM((OUT_PARTS, tm // OUT_PARTS, D_MODEL), BF16)],
        compiler_params=_cparams("parallel"),
        name="rwkv_out",
    )(y2, bonus, sz, gn_g, gn_b, x2, w_bf, g)


def _trunk(x, p, rope):
    batch, seq, _ = x.shape
    x2 = x.reshape(batch * seq, D_MODEL)
    v_first = None
    depth = p["norm_pre"].shape[0]
    for layer in range(depth):
        j = layer // 2
        g_pre = p["norm_pre"][layer][None, :]
        g_post = p["norm_post"][layer][None, :]
        if layer % 2 == 0:
            w_in = p["att_w_in"]
            outs, lses = [], []
            for gidx, (_, dil) in enumerate(ATT_GROUPS):
                q, k, v = _att_in(x2, g_pre, w_in, j, *rope[gidx], seq, gidx, dil)
                o, lse = _attn_group(q, k, v, dil, batch, seq)
                outs.append(o)
                lses.append(lse)
            z = _gate_in(x2, g_pre, w_in, j)
            x2 = _att_out(outs, lses, z, x2, p["att_w_out"], j, g_post)
        else:
            vres = None if j == 0 else (p["rwkv_v0"][j - 1][None, :], p["rwkv_v1"][j - 1], p["rwkv_v2"][j - 1])
            kkq, rq, ad, kdk, kend, aend, est, v, bonus, sz = _rwkv_in(
                x2, seq, g_pre, p["rwkv_mu_prev"][j], p["rwkv_mu_next"][j], p["rwkv_w_in"], j,
                p["rwkv_w0"][j], p["rwkv_w1"][j], p["rwkv_w2"][j],
                p["rwkv_a0"][j], p["rwkv_a1"][j], p["rwkv_a2"][j],
                p["rwkv_k_k"][j][None, :], p["rwkv_k_a"][j][None, :], p["rwkv_r_k"][j].reshape(2, D_MODEL),
                vres, v_first)
            if j == 0:
                v_first = v
            y2 = _rwkv_scan(kkq, rq, ad, kdk, kend, aend, est, v, batch, seq)
            x2 = _rwkv_out(y2, bonus, sz, p["rwkv_gn_g"][j][None, :], p["rwkv_gn_b"][j][None, :], x2,
                           p["rwkv_w_out"], j, g_post)
    return x2.reshape(batch, seq, D_MODEL)


def kernel(x_prompt, x_sample, norm_pre, norm_post, att_w_in, att_w_out, rwkv_mu_prev, rwkv_mu_next, rwkv_w_in, rwkv_w0, rwkv_w1, rwkv_w2, rwkv_a0, rwkv_a1, rwkv_a2, rwkv_v0, rwkv_v1, rwkv_v2, rwkv_k_k, rwkv_k_a, rwkv_r_k, rwkv_gn_g, rwkv_gn_b, rwkv_w_out):
    p = dict(
        norm_pre=norm_pre, norm_post=norm_post,
        att_w_in=att_w_in.astype(BF16), att_w_out=att_w_out.astype(BF16),
        rwkv_mu_prev=rwkv_mu_prev, rwkv_mu_next=rwkv_mu_next, rwkv_w_in=rwkv_w_in.astype(BF16),
        rwkv_w0=rwkv_w0, rwkv_w1=rwkv_w1, rwkv_w2=rwkv_w2,
        rwkv_a0=rwkv_a0, rwkv_a1=rwkv_a1, rwkv_a2=rwkv_a2,
        rwkv_v0=rwkv_v0, rwkv_v1=rwkv_v1, rwkv_v2=rwkv_v2,
        rwkv_k_k=rwkv_k_k, rwkv_k_a=rwkv_k_a, rwkv_r_k=rwkv_r_k,
        rwkv_gn_g=rwkv_gn_g, rwkv_gn_b=rwkv_gn_b, rwkv_w_out=rwkv_w_out.astype(BF16),
    )
    max_seq = max(x_prompt.shape[1], x_sample.shape[1])
    rope = [_rope_tables(max_seq, dil) for _, dil in ATT_GROUPS]
    return (_trunk(x_prompt, p, rope), _trunk(x_sample, p, rope))
```

```python
import functools
import math

import jax
import jax.numpy as jnp
from jax import lax
from jax.experimental import pallas as pl
from jax.experimental.pallas import tpu as pltpu

F32 = jnp.float32
BF16 = jnp.bfloat16

D_MODEL = 2048
LANES = 128
MXU_COLS = 256
ATT_HEAD_DIM = 128
ATT_GROUPS = ((128, 1), (512, 4), (2048, 16))
ATT_HALF = 64
ROPE_THETA = 10000.0
RWKV_HEAD_DIM = 64
LORA_PAD = 128
RMS_EPS = 1e-6
GN_EPS = 64e-5
NEG_INF = -1e30
CHUNK = 64
ATT_IN_ROWS = 512
OUT_PARTS = 2
CHUNKS_PER_TRIP = 2
VMEM_LIMIT_BYTES = 56 * 1024 * 1024
RWKV_OUT_VMEM_LIMIT_BYTES = 60 * 1024 * 1024

NT_DIMS = (((1,), (1,)), ((), ()))
TN_DIMS = (((0,), (0,)), ((), ()))


def _cparams(*sem, vmem_limit_bytes=VMEM_LIMIT_BYTES):
    return pltpu.CompilerParams(dimension_semantics=sem, vmem_limit_bytes=vmem_limit_bytes)


def _rms_scale(x):
    return lax.rsqrt(jnp.mean(x * x, axis=-1, keepdims=True) + RMS_EPS)


def _sigmoid(x):
    return 0.5 * jnp.tanh(0.5 * x) + 0.5


def _split_dot(lhs_bf, x):
    hi = x.astype(BF16)
    lo = (x - hi.astype(F32)).astype(BF16)
    return (jnp.dot(lhs_bf, hi, preferred_element_type=F32)
            + jnp.dot(lhs_bf, lo, preferred_element_type=F32))


def _group_sum(x, gmat):
    hi = x.astype(BF16)
    lo = (x - hi.astype(F32)).astype(BF16)
    return (jnp.dot(hi, gmat, preferred_element_type=F32)
            + jnp.dot(lo, gmat, preferred_element_type=F32))


def _group_matrix(n):
    r = lax.broadcasted_iota(jnp.int32, (n, n), 0) // RWKV_HEAD_DIM
    c = lax.broadcasted_iota(jnp.int32, (n, n), 1) // RWKV_HEAD_DIM
    return jnp.where(r == c, 1.0, 0.0).astype(BF16)


def _rope_table_kernel(invf_ref, cos_ref, sin_ref, *, dil):
    rows = cos_ref.shape[0]
    base = pl.program_id(0) * rows
    local = lax.broadcasted_iota(jnp.int32, (rows, LANES), 0)
    per_res = rows // dil
    pos = (base + (local % per_res) * dil + local // per_res).astype(F32)
    ang = pos * invf_ref[...]
    lane = lax.broadcasted_iota(jnp.int32, (rows, LANES), 1)
    s = jnp.sin(ang)
    cos_ref[...] = jnp.cos(ang)
    sin_ref[...] = jnp.where(lane < ATT_HEAD_DIM // 2, -s, s)


def _rope_tables(seq, dil):
    half = ATT_HEAD_DIM // 2
    inv_freq = 1.0 / (ROPE_THETA ** (jnp.arange(half, dtype=F32) * 2.0 / ATT_HEAD_DIM))
    invf = jnp.concatenate([inv_freq, inv_freq])[None, :]
    rows = ATT_IN_ROWS
    return pl.pallas_call(
        functools.partial(_rope_table_kernel, dil=dil),
        grid=(seq // rows,),
        in_specs=[pl.BlockSpec((1, LANES), lambda i: (0, 0))],
        out_specs=[pl.BlockSpec((rows, LANES), lambda i: (i, 0))] * 2,
        out_shape=[jax.ShapeDtypeStruct((seq, LANES), F32)] * 2,
        compiler_params=_cparams("arbitrary"),
        name=f"rope_table_d{dil}",
    )(invf)


def _att_in_kernel(x_ref, g_ref, wq_ref, wk_ref, wv_ref, cos_ref, sin_ref, q_ref, k_ref, v_ref, h_ref, *hs_ref,
                   tm, tn, dil, scale):
    per_res = tm // dil

    @pl.when(pl.program_id(1) == 0)
    def _():
        x = x_ref[...]
        h = x * _rms_scale(x) * g_ref[...]
        if dil == 1:
            h_ref[...] = h.astype(BF16)
        else:
            for cb in range(D_MODEL // LANES):
                sl = slice(cb * LANES, (cb + 1) * LANES)
                hs_ref[0][cb] = h[:, sl]
                for r in range(dil):
                    h_ref[r * per_res:(r + 1) * per_res, sl] = (
                        hs_ref[0][cb, pl.ds(r, per_res, stride=dil), :].astype(BF16))

    h = h_ref[...]
    cos = cos_ref[...]
    sin = sin_ref[...]
    for w_ref, o_ref, rope, sc in ((wq_ref, q_ref, True, scale), (wk_ref, k_ref, True, None),
                                   (wv_ref, v_ref, False, None)):
        for cb in range(tn // MXU_COLS):
            acc = jnp.dot(h, w_ref[:, cb * MXU_COLS:(cb + 1) * MXU_COLS], preferred_element_type=F32)
            for hh in range(MXU_COLS // LANES):
                sl = slice(cb * MXU_COLS + hh * LANES, cb * MXU_COLS + (hh + 1) * LANES)
                t = acc[:, hh * LANES:(hh + 1) * LANES]
                if rope:
                    t = t * cos + pltpu.roll(t, ATT_HEAD_DIM // 2, axis=1) * sin
                if sc is not None:
                    t = t * sc
                t = t.astype(BF16)
                for r in range(dil):
                    o_ref[r, :, sl] = t[r * per_res:(r + 1) * per_res]


def _att_in(x2, g, w_bf, layer, cos_t, sin_t, seq, gidx, dil):
    rows = x2.shape[0]
    tm, tn = ATT_IN_ROWS, 1024
    tiles_per_seq = seq // tm
    ncol = D_MODEL // tn
    kern = functools.partial(_att_in_kernel, tm=tm, tn=tn, dil=dil, scale=ATT_HEAD_DIM ** -0.5)

    def wcol(part):
        return pl.BlockSpec((None, D_MODEL, tn), lambda i, j: (layer, 0, (gidx * 3 + part) * ncol + j))

    out_spec = pl.BlockSpec((dil, tm // dil, tn), lambda i, j: (0, i, j))
    out_sds = jax.ShapeDtypeStruct((dil, rows // dil, D_MODEL), BF16)
    scratch = [pltpu.VMEM((tm, D_MODEL), BF16)]
    if dil > 1:
        scratch.append(pltpu.VMEM((D_MODEL // LANES, tm, LANES), F32))
    return pl.pallas_call(
        kern,
        grid=(rows // tm, ncol),
        in_specs=[
            pl.BlockSpec((tm, D_MODEL), lambda i, j: (i, 0)),
            pl.BlockSpec((1, D_MODEL), lambda i, j: (0, 0)),
            wcol(0), wcol(1), wcol(2),
            pl.BlockSpec((tm, LANES), lambda i, j: (i % tiles_per_seq, 0)),
            pl.BlockSpec((tm, LANES), lambda i, j: (i % tiles_per_seq, 0)),
        ],
        out_specs=[out_spec] * 3,
        out_shape=[out_sds] * 3,
        scratch_shapes=scratch,
        compiler_params=_cparams("parallel", "arbitrary"),
        name=f"att_in_d{dil}",
    )(x2, g, w_bf, w_bf, w_bf, cos_t, sin_t)


def _gate_in_kernel(x_ref, g_ref, w_ref, o_ref):
    x = x_ref[...]
    h = (x * _rms_scale(x) * g_ref[...]).astype(BF16)
    o_ref[...] = jnp.dot(h, w_ref[...], preferred_element_type=F32).astype(BF16)


def _gate_in(x2, g, w_bf, layer):
    rows = x2.shape[0]
    tm = 1024
    col0 = 3 * len(ATT_GROUPS)
    return pl.pallas_call(
        _gate_in_kernel,
        grid=(rows // tm,),
        in_specs=[
            pl.BlockSpec((tm, D_MODEL), lambda i: (i, 0)),
            pl.BlockSpec((1, D_MODEL), lambda i: (0, 0)),
            pl.BlockSpec((None, D_MODEL, D_MODEL), lambda i: (layer, 0, col0)),
        ],
        out_specs=pl.BlockSpec((tm, D_MODEL), lambda i: (i, 0)),
        out_shape=jax.ShapeDtypeStruct((rows, D_MODEL), BF16),
        compiler_params=_cparams("parallel"),
        name="att_gate_in",
    )(x2, g, w_bf)


def _attn_kernel(q_ref, kp_ref, kc_ref, kn_ref, vp_ref, vc_ref, vn_ref, o_ref, lse_ref, *, bq, sb, sub_len, nh):
    i = pl.program_id(3)
    nk = sb + 2 * ATT_HALF
    nsub = bq // sb
    ii = lax.broadcasted_iota(jnp.int32, (sb, nk), 0)
    jj = lax.broadcasted_iota(jnp.int32, (sb, nk), 1)
    rel = jj - ii
    in_band = (rel >= 0) & (rel <= 2 * ATT_HALF)
    valid = []
    for u in range(nsub):
        kpos = i * bq + u * sb - ATT_HALF + jj
        valid.append(in_band & (kpos >= 0) & (kpos < sub_len))
    lane = lax.broadcasted_iota(jnp.int32, (sb, LANES), 1)
    units = [(hh, u) for hh in range(nh) for u in range(nsub)]

    def scores(hh, u):
        sl = slice(hh * LANES, (hh + 1) * LANES)
        kcat = jnp.concatenate([kp_ref[:, sl], kc_ref[:, sl], kn_ref[:, sl]], axis=0)
        s = lax.dot_general(q_ref[u * sb:(u + 1) * sb, sl], kcat[u * sb:u * sb + nk], NT_DIMS,
                            preferred_element_type=F32)
        return jnp.where(valid[u], s, NEG_INF)

    def finish(hh, u, s, lse_acc):
        sl = slice(hh * LANES, (hh + 1) * LANES)
        vcat = jnp.concatenate([vp_ref[:, sl], vc_ref[:, sl], vn_ref[:, sl]], axis=0)
        m = jnp.max(s, axis=-1, keepdims=True)
        p = jnp.exp(s - m)
        l = jnp.sum(p, axis=-1, keepdims=True)
        o = jnp.dot(p.astype(BF16), vcat[u * sb:u * sb + nk], preferred_element_type=F32)
        o_ref[u * sb:(u + 1) * sb, sl] = (o / l).astype(BF16)
        lse_acc[u] = jnp.where(lane == hh, m + jnp.log(l), lse_acc[u])

    lse_acc = [jnp.zeros((sb, LANES), F32) for _ in range(nsub)]
    s_prev = scores(*units[0])
    for k in range(1, len(units)):
        s_next = scores(*units[k])
        finish(*units[k - 1], s_prev, lse_acc)
        s_prev = s_next
    finish(*units[-1], s_prev, lse_acc)
    for u in range(nsub):
        lse_ref[u * sb:(u + 1) * sb, :] = lse_acc[u]


def _attn_group(q, k, v, dil, batch, seq):
    sub_len = seq // dil
    bq = min(512, sub_len)
    sb = min(128, bq)
    hw = D_MODEL
    nh = hw // LANES
    hblocks = D_MODEL // hw
    nqb = sub_len // bq
    halo_per_q = bq // ATT_HALF
    n_halo = sub_len // ATT_HALF

    cur = pl.BlockSpec((None, bq, hw), lambda b, r, h, i: (r, b * nqb + i, h))
    prev = pl.BlockSpec((None, ATT_HALF, hw),
                        lambda b, r, h, i: (r, b * n_halo + jnp.maximum(i * halo_per_q - 1, 0), h))
    nxt = pl.BlockSpec((None, ATT_HALF, hw),
                       lambda b, r, h, i: (r, b * n_halo + jnp.minimum((i + 1) * halo_per_q, n_halo - 1), h))
    lse_spec = pl.BlockSpec((None, bq, LANES), lambda b, r, h, i: (r, b * nqb + i, h))
    rows = batch * sub_len
    kern = functools.partial(_attn_kernel, bq=bq, sb=sb, sub_len=sub_len, nh=nh)
    return pl.pallas_call(
        kern,
        grid=(batch, dil, hblocks, nqb),
        in_specs=[cur, prev, cur, nxt, prev, cur, nxt],
        out_specs=[cur, lse_spec],
        out_shape=[jax.ShapeDtypeStruct((dil, rows, D_MODEL), BF16),
                   jax.ShapeDtypeStruct((dil, rows, hblocks * LANES), F32)],
        compiler_params=_cparams("parallel", "parallel", "parallel", "arbitrary"),
        name=f"attn_d{dil}",
    )(q, k, k, k, v, v, v)


def _project_parts(prologue, nparts, part_rows, y_ref, w_ref, g_ref, x_ref, o_ref):
    ncb = D_MODEL // LANES
    cb_per_k = MXU_COLS // LANES

    def finish(out, p):
        rows = slice(p * part_rows, (p + 1) * part_rows)
        o_ref[rows, :] = x_ref[rows, :] + out * _rms_scale(out) * g_ref[...]

    for cb in range(ncb):
        prologue(0, cb)
    for p in range(nparts):
        acc = None
        for kc in range(D_MODEL // MXU_COLS):
            ks = slice(kc * MXU_COLS, (kc + 1) * MXU_COLS)
            part = jnp.dot(y_ref[p, :, ks], w_ref[ks, :], preferred_element_type=F32)
            acc = part if acc is None else acc + part
            if p + 1 < nparts:
                for cb in range(kc * cb_per_k, (kc + 1) * cb_per_k):
                    prologue(p + 1, cb)
        finish(acc, p)


def _att_out_kernel(o0_ref, o1_ref, o2_ref, l0_ref, l1_ref, l2_ref, z_ref, x_ref, w_ref, g_ref, o_ref,
                    so1, so2, sl1, sl2, y_ref, *, tm, dils, heads_per_block):
    nlb = l0_ref.shape[-1] // LANES
    for src, dst, d in ((l1_ref, sl1, dils[1]), (l2_ref, sl2, dils[2])):
        for lb in range(nlb):
            for r in range(d):
                dst[lb, pl.ds(r, tm // d, stride=d), :] = src[r, :, lb * LANES:(lb + 1) * LANES]
    wts = []
    for lb in range(nlb):
        l0, l1, l2 = l0_ref[0, :, lb * LANES:(lb + 1) * LANES], sl1[lb], sl2[lb]
        m = jnp.maximum(jnp.maximum(l0, l1), l2)
        e0, e1, e2 = jnp.exp(l0 - m), jnp.exp(l1 - m), jnp.exp(l2 - m)
        inv = 1.0 / (e0 + e1 + e2)
        wts.append((e0 * inv, e1 * inv, e2 * inv))
    for cb in range(D_MODEL // LANES):
        sl = slice(cb * LANES, (cb + 1) * LANES)
        for src, dst, d in ((o1_ref, so1, dils[1]), (o2_ref, so2, dils[2])):
            for r in range(d):
                dst[cb, pl.ds(r, tm // d, stride=d), :] = src[r, :, sl].astype(F32)
    part_rows = tm // OUT_PARTS

    def combine(p, cb):
        rows = slice(p * part_rows, (p + 1) * part_rows)
        sl = slice(cb * LANES, (cb + 1) * LANES)
        w0, w1, w2 = wts[cb // heads_per_block]
        hl = cb % heads_per_block
        bc = lambda w: jnp.broadcast_to(w[rows, hl:hl + 1], (part_rows, LANES))
        o = (bc(w0) * o0_ref[0, rows, sl].astype(F32) + bc(w1) * so1[cb, rows, :]
             + bc(w2) * so2[cb, rows, :])
        z = z_ref[rows, sl].astype(F32)
        y_ref[p, :, sl] = (o * (z * _sigmoid(z))).astype(BF16)

    _project_parts(combine, OUT_PARTS, part_rows, y_ref, w_ref, g_ref, x_ref, o_ref)


def _att_out(outs, lses, z, x2, w_bf, layer, g):
    rows = x2.shape[0]
    tm = 512
    dils = tuple(d for _, d in ATT_GROUPS)
    lse_w = lses[0].shape[-1]
    nlb = lse_w // LANES
    row_spec = pl.BlockSpec((tm, D_MODEL), lambda i: (i, 0))

    def res_spec(d, width):
        return pl.BlockSpec((d, tm // d, width), lambda i: (0, i, 0))

    kern = functools.partial(_att_out_kernel, tm=tm, dils=dils, heads_per_block=D_MODEL // LANES // nlb)
    return pl.pallas_call(
        kern,
        grid=(rows // tm,),
        in_specs=[res_spec(d, D_MODEL) for d in dils] + [res_spec(d, lse_w) for d in dils] + [
            row_spec, row_spec,
            pl.BlockSpec((None, D_MODEL, D_MODEL), lambda i: (layer, 0, 0), pipeline_mode=pl.Buffered(1)),
            pl.BlockSpec((1, D_MODEL), lambda i: (0, 0)),
        ],
        out_specs=row_spec,
        out_shape=jax.ShapeDtypeStruct((rows, D_MODEL), F32),
        scratch_shapes=[pltpu.VMEM((D_MODEL // LANES, tm, LANES), F32)] * 2
        + [pltpu.VMEM((nlb, tm, LANES), F32)] * 2 + [pltpu.VMEM((OUT_PARTS, tm // OUT_PARTS, D_MODEL), BF16)],
        compiler_params=_cparams("parallel"),
        name="att_out",
    )(*outs, *lses, z, x2, w_bf, g)


def _chunk_tri(n, reverse):
    t = lax.broadcasted_iota(jnp.int32, (n, n), 0)
    s = lax.broadcasted_iota(jnp.int32, (n, n), 1)
    same = (t // CHUNK) == (s // CHUNK)
    order = (s >= t) if reverse else (s <= t)
    return jnp.where(same & order, 1.0, 0.0).astype(BF16)


def _rwkv_in_kernel(*refs, tm, tn, tiles_per_seq, has_vres):
    (x_ref, xp_ref, xn_ref, g_ref, mup_ref, mun_ref,
     wr_ref, wk_ref, wv_ref, wz_ref,
     w1_ref, w2_ref, w0_ref, a1_ref, a2_ref, a0_ref,
     kk_ref, ka_ref, rk_ref) = refs[:19]
    pos = 19
    if has_vres:
        v1_ref, v2_ref, v0_ref, vf_ref = refs[pos:pos + 4]
        pos += 4
    (kkq_out, rq_out, ad_out, kdk_out, kend_out, aend_out, est_out, v_out, bonus_out, sz_out) = refs[pos:pos + 10]
    pos += 10
    xs_ref, hw_ref, ha_ref = refs[pos:pos + 3]
    hv_ref = refs[pos + 3] if has_vres else None

    i = pl.program_id(0)
    j = pl.program_id(1)
    mix_slot = {0: 0, 2: 1, 3: 2, 5: 3}

    @pl.when(j == 0)
    def _():
        t_in_seq = i % tiles_per_seq
        keep_prev = jnp.where(t_in_seq == 0, 0.0, 1.0).astype(F32)
        keep_next = jnp.where(t_in_seq == tiles_per_seq - 1, 0.0, 1.0).astype(F32)
        sx = _rms_scale(x_ref[...])
        xp = xp_ref[7:8, :]
        xn = xn_ref[0:1, :]
        sp = _rms_scale(xp) * keep_prev
        sn = _rms_scale(xn) * keep_next
        cw = 512
        row = lax.broadcasted_iota(jnp.int32, (tm, cw), 0)
        hw_acc = [jnp.zeros((tm, LORA_PAD), F32) for _ in range(2)]
        ha_acc = [jnp.zeros((tm, LORA_PAD), F32) for _ in range(2)]
        hv_acc = jnp.zeros((tm, LORA_PAD), F32)
        for cb in range(D_MODEL // cw):
            sl = slice(cb * cw, (cb + 1) * cw)
            g = g_ref[:, sl]
            h = x_ref[:, sl] * sx * g
            hp_row = xp[:, sl] * sp * g
            hn_row = xn[:, sl] * sn * g
            h_prev = jnp.where(row == 0, hp_row, pltpu.roll(h, 1, axis=0))
            h_next = jnp.where(row == tm - 1, hn_row, pltpu.roll(h, tm - 1, axis=0))
            h16 = h.astype(BF16)
            dp = (h_prev - h).astype(BF16)
            dn = (h_next - h).astype(BF16)
            mixes = {}
            for t in range(6):
                mixes[t] = h16 + dp * mup_ref[t:t + 1, sl].astype(BF16) + dn * mun_ref[t:t + 1, sl].astype(BF16)
                if t in mix_slot:
                    xs_ref[mix_slot[t], :, sl] = mixes[t]
            for c in range(2):
                hw_acc[c] = hw_acc[c] + jnp.dot(mixes[1], w1_ref[c, sl, :], preferred_element_type=F32)
                ha_acc[c] = ha_acc[c] + jnp.dot(mixes[4], a1_ref[c, sl, :], preferred_element_type=F32)
            if has_vres:
                hv_acc = hv_acc + jnp.dot(mixes[3], v1_ref[sl, :], preferred_element_type=F32)
        for c in range(2):
            hw_ref[c] = jnp.tanh(hw_acc[c]).astype(BF16)
            ha_ref[c] = ha_acc[c].astype(BF16)
        if has_vres:
            hv_ref[...] = hv_acc.astype(BF16)

    wl = [w0_ref[c:c + 1, :] + jnp.dot(hw_ref[c], w2_ref[c], preferred_element_type=F32) for c in range(2)]
    al = [a0_ref[c:c + 1, :] + jnp.dot(ha_ref[c], a2_ref[c], preferred_element_type=F32) for c in range(2)]
    if has_vres:
        gl = v0_ref[...] + jnp.dot(hv_ref[...], v2_ref[...], preferred_element_type=F32)
    k = jnp.dot(xs_ref[1], wk_ref[...], preferred_element_type=F32)
    r = jnp.dot(xs_ref[0], wr_ref[...], preferred_element_type=F32)
    z = jnp.dot(xs_ref[3], wz_ref[...], preferred_element_type=F32)
    v = jnp.dot(xs_ref[2], wv_ref[...], preferred_element_type=F32)
    if has_vres:
        v = v + (vf_ref[...] - v) * _sigmoid(gl)
    gmat = _group_matrix(tn)
    kk = k * kk_ref[...]
    kk = kk * lax.rsqrt(jnp.maximum(_group_sum(kk * kk, gmat), 1e-24))
    k_a = ka_ref[...]
    rk_acc = jnp.zeros((tm, tn), F32)
    nchunk = tm // CHUNK
    half = 256
    for c in range(2):
        lw = -math.exp(-0.5) * _sigmoid(wl[c])
        a = _sigmoid(al[c])
        kd = k * (1.0 + (a - 1.0) * k_a)
        rk_acc = rk_acc + r * kd * rk_ref[c:c + 1, :]
        tri = _chunk_tri(half, reverse=(c == 1))
        g = jnp.concatenate([_split_dot(tri, lw[hh * half:(hh + 1) * half]) for hh in range(tm // half)], axis=0)
        g3 = g.reshape(nchunk, CHUNK, tn)
        last = 0 if c == 1 else CHUNK - 1
        g_tot = g3[:, last:last + 1, :]
        e_tot = jnp.exp(g_tot)
        est_out[c] = e_tot.reshape(nchunk, tn)
        e_q = jnp.exp(g)
        e_qp = jnp.exp(g - lw)
        e_k = jnp.exp(-g)
        e_end = (e_k.reshape(nchunk, CHUNK, tn) * e_tot).reshape(tm, tn)
        kka = kk * a
        kkq_out[c] = (kk * e_qp).astype(BF16)
        rq_out[c] = (r * e_q).astype(BF16)
        ad_out[c] = (kka * e_k).astype(BF16)
        kdk_out[c] = (kd * e_k).astype(BF16)
        kend_out[c] = (kd * e_end).astype(BF16)
        aend_out[c] = (kka * e_end).astype(BF16)
    v_out[...] = v
    bonus_out[...] = _group_sum(rk_acc, gmat) * v
    sz_out[...] = (z * _sigmoid(z)).astype(BF16)


def _pad_lora(w1, w2):
    rank = w1.shape[-1]
    pad1 = [(0, 0)] * (w1.ndim - 1) + [(0, LORA_PAD - rank)]
    pad2 = [(0, 0)] * (w2.ndim - 2) + [(0, LORA_PAD - rank), (0, 0)]
    return jnp.pad(w1, pad1).astype(BF16), jnp.pad(w2, pad2).astype(BF16)


def _rwkv_in(x2, seq, g, mu_prev, mu_next, w_in_bf, layer, w0, w1, w2, a0, a1, a2, k_k, k_a, r_k, vres, v_first):
    rows = x2.shape[0]
    tm, tn = 512, 256
    tiles_per_seq = seq // tm
    ncol = D_MODEL // tn
    has_vres = vres is not None
    w1p, w2p = _pad_lora(w1, w2)
    a1p, a2p = _pad_lora(a1, a2)
    sub = tm // 8
    nsub = rows // 8

    def const2(shape):
        return pl.BlockSpec(shape, lambda i, j: (0, 0))

    def col2(nrow):
        return pl.BlockSpec((nrow, tn), lambda i, j: (0, j))

    def wcol(gi):
        return pl.BlockSpec((None, D_MODEL, tn), lambda i, j: (layer, 0, gi * ncol + j))

    in_specs = [
        pl.BlockSpec((tm, D_MODEL), lambda i, j: (i, 0)),
        pl.BlockSpec((8, D_MODEL), lambda i, j: (jnp.maximum(i * sub - 1, 0), 0)),
        pl.BlockSpec((8, D_MODEL), lambda i, j: (jnp.minimum((i + 1) * sub, nsub - 1), 0)),
        const2((1, D_MODEL)), const2((6, D_MODEL)), const2((6, D_MODEL)),
        wcol(0), wcol(1), wcol(2), wcol(3),
        pl.BlockSpec((2, D_MODEL, LORA_PAD), lambda i, j: (0, 0, 0)),
        pl.BlockSpec((2, LORA_PAD, tn), lambda i, j: (0, 0, j)),
        col2(2),
        pl.BlockSpec((2, D_MODEL, LORA_PAD), lambda i, j: (0, 0, 0)),
        pl.BlockSpec((2, LORA_PAD, tn), lambda i, j: (0, 0, j)),
        col2(2),
        col2(1), col2(1), col2(2),
    ]
    args = [x2, x2, x2, g, mu_prev, mu_next, w_in_bf, w_in_bf, w_in_bf, w_in_bf,
            w1p, w2p, w0, a1p, a2p, a0, k_k, k_a, r_k]
    if has_vres:
        v0, v1, v2 = vres
        v1p, v2p = _pad_lora(v1, v2)
        in_specs += [const2((D_MODEL, LORA_PAD)), col2(LORA_PAD), col2(1),
                     pl.BlockSpec((tm, tn), lambda i, j: (i, j))]
        args += [v1p, v2p, v0, v_first]

    tile = pl.BlockSpec((tm, tn), lambda i, j: (i, j))
    tile2 = pl.BlockSpec((2, tm, tn), lambda i, j: (0, i, j))
    est_spec = pl.BlockSpec((2, tm // CHUNK, tn), lambda i, j: (0, i, j))
    sds_bf2 = jax.ShapeDtypeStruct((2, rows, D_MODEL), BF16)
    sds = jax.ShapeDtypeStruct((rows, D_MODEL), F32)
    scratch = [pltpu.VMEM((4, tm, D_MODEL), BF16), pltpu.VMEM((2, tm, LORA_PAD), BF16),
               pltpu.VMEM((2, tm, LORA_PAD), BF16)]
    if has_vres:
        scratch.append(pltpu.VMEM((tm, LORA_PAD), BF16))
    kern = functools.partial(_rwkv_in_kernel, tm=tm, tn=tn, tiles_per_seq=tiles_per_seq, has_vres=has_vres)
    return pl.pallas_call(
        kern,
        grid=(rows // tm, ncol),
        in_specs=in_specs,
        out_specs=[tile2] * 6 + [est_spec, tile, tile, tile],
        out_shape=[sds_bf2] * 6 + [jax.ShapeDtypeStruct((2, rows // CHUNK, D_MODEL), F32), sds, sds,
                                   jax.ShapeDtypeStruct((rows, D_MODEL), BF16)],
        scratch_shapes=scratch,
        compiler_params=_cparams("parallel", "arbitrary"),
        name="rwkv_in_vres" if has_vres else "rwkv_in",
    )(*args)


def _rwkv_scan_kernel(kkq_ref, rq_ref, ad_ref, kdk_ref, kend_ref, aend_ref, v_ref, est_ref, y_ref, state_ref,
                      *, tb, npair):
    rev = pl.program_id(1) == 1
    step = pl.program_id(3)
    nchunk = tb // CHUNK
    pairs = range(npair)

    @pl.when(step == 0)
    def _():
        state_ref[...] = jnp.zeros_like(state_ref)

    t = lax.broadcasted_iota(jnp.int32, (CHUNK, LANES), 0)
    lane = lax.broadcasted_iota(jnp.int32, (CHUNK, LANES), 1)
    s = lane & (CHUNK - 1)
    d = jnp.where(rev, s - t, t - s)
    strict = d > 0
    incl = d >= 0
    eye = jnp.where(d == 0, 1.0, 0.0).astype(F32)
    first_half = lane < CHUNK
    half_a = jnp.where(first_half, 1.0, 0.0).astype(BF16)
    half_b = jnp.where(first_half, 0.0, 1.0).astype(BF16)
    rr = lax.broadcasted_iota(jnp.int32, (LANES, LANES), 0)
    cc = lax.broadcasted_iota(jnp.int32, (LANES, LANES), 1)
    bd_mask = (rr < CHUNK) == (cc < CHUNK)
    est_rows = lax.broadcasted_iota(jnp.int32, (nchunk, npair * LANES), 0)

    def bd16(y):
        return jnp.concatenate([y * half_a, y * half_b], axis=0)

    def bd32(y):
        return jnp.concatenate([jnp.where(first_half, y, 0.0), jnp.where(first_half, 0.0, y)],
                               axis=0).astype(BF16)

    def mm(a, b):
        return jnp.dot(a, b, preferred_element_type=F32)

    def body(ci, carry):
        first = ci * CHUNKS_PER_TRIP
        cs = [jnp.where(rev, nchunk - 1 - (first + u), first + u) for u in range(CHUNKS_PER_TRIP)]
        jobs = [(pl.ds(pl.multiple_of(c * CHUNK, CHUNK), CHUNK), slice(p * LANES, (p + 1) * LANES))
                for c in cs for p in pairs]
        n = range(len(jobs))
        kkq = [kkq_ref[rw, ln] for rw, ln in jobs]
        rq = [rq_ref[rw, ln] for rw, ln in jobs]
        v = [v_ref[rw, ln] for rw, ln in jobs]
        gram = [lax.dot_general(
            jnp.concatenate([kkq[i], rq[i]], axis=0),
            jnp.concatenate([bd16(ad_ref[jobs[i]]), bd16(kdk_ref[jobs[i]])], axis=0),
            NT_DIMS, preferred_element_type=F32) for i in n]
        a_mat = [jnp.where(strict, gram[i][0:CHUNK, 0:LANES], 0.0) for i in n]
        ak = [jnp.where(strict, gram[i][0:CHUNK, LANES:], 0.0) for i in n]
        bra = [jnp.where(incl, gram[i][CHUNK:, 0:LANES], 0.0).astype(BF16) for i in n]
        brk = [jnp.where(incl, gram[i][CHUNK:, LANES:], 0.0) for i in n]
        early = [mm(jnp.concatenate([ak[i], brk[i]], axis=0).astype(BF16), bd32(v[i])) for i in n]
        a_pow = [mm(a_mat[i].astype(BF16), bd32(a_mat[i])) for i in n]
        inv = [eye - a_mat[i] for i in n]
        for _ in range(4):
            st = [mm(jnp.concatenate([inv[i], a_pow[i]], axis=0).astype(BF16), bd32(a_pow[i])) for i in n]
            inv = [inv[i] + st[i][0:CHUNK] for i in n]
            a_pow = [st[i][CHUNK:] for i in n]
        inv = [inv[i] + mm(inv[i].astype(BF16), bd32(a_pow[i])) for i in n]
        wu = [mm(inv[i].astype(BF16), jnp.concatenate([bd16(kkq[i]), bd32(early[i][0:CHUNK])], axis=1))
              for i in n]
        ry = [mm(bra[i], jnp.concatenate([bd32(wu[i][:, 0:LANES]), bd32(wu[i][:, LANES:])], axis=1))
              for i in n]
        aend = [aend_ref[jobs[i]] for i in n]
        xraw = [lax.dot_general(wu[i][:, 0:LANES].astype(BF16), aend[i], TN_DIMS, preferred_element_type=F32)
                for i in n]
        nraw = [lax.dot_general(jnp.concatenate([v[i], -wu[i][:, LANES:]], axis=0).astype(BF16),
                                jnp.concatenate([kend_ref[jobs[i]], aend[i]], axis=0),
                                TN_DIMS, preferred_element_type=F32) for i in n]
        rqp = [(rq[i].astype(F32) - ry[i][:, 0:LANES]).astype(BF16) for i in n]
        yv = [early[i][CHUNK:] - ry[i][:, LANES:] for i in n]
        xmat = [jnp.where(bd_mask, xraw[i], 0.0).astype(BF16) for i in n]
        nmat = [jnp.where(bd_mask, nraw[i], 0.0) for i in n]
        est_all = est_ref[...]
        state = [state_ref[p] for p in pairs]
        for u in range(CHUNKS_PER_TRIP):
            est_row = jnp.sum(jnp.where(est_rows == cs[u], est_all, 0.0), axis=0, keepdims=True)
            idx = [u * npair + p for p in pairs]
            s16 = [state[p].astype(BF16) for p in pairs]
            ys = [yv[idx[p]] + lax.dot_general(rqp[idx[p]], s16[p], NT_DIMS, preferred_element_type=F32)
                  for p in pairs]
            sx = [mm(s16[p], xmat[idx[p]]) for p in pairs]
            for p in pairs:
                rw, ln = jobs[idx[p]]
                y_ref[rw, ln] = ys[p]
                state[p] = state[p] * est_row[:, ln] - sx[p] + nmat[idx[p]]
        for p in pairs:
            state_ref[p] = state[p]
        return carry

    lax.fori_loop(0, nchunk // CHUNKS_PER_TRIP, body, 0)


def _rwkv_scan(kkq, rq, ad, kdk, kend, aend, est, v, batch, seq):
    tb, lw_lanes = 512, D_MODEL
    nt = seq // tb
    npair = lw_lanes // LANES

    def tmap(i, dr):
        return i + dr * (nt - 1 - 2 * i)

    v_spec = pl.BlockSpec((tb, lw_lanes), lambda b, dr, h, i: (b * nt + tmap(i, dr), h))
    d_spec = pl.BlockSpec((None, tb, lw_lanes), lambda b, dr, h, i: (dr, b * nt + tmap(i, dr), h))
    e_spec = pl.BlockSpec((None, tb // CHUNK, lw_lanes), lambda b, dr, h, i: (dr, b * nt + tmap(i, dr), h))
    kern = functools.partial(_rwkv_scan_kernel, tb=tb, npair=npair)
    return pl.pallas_call(
        kern,
        grid=(batch, 2, D_MODEL // lw_lanes, nt),
        in_specs=[d_spec] * 6 + [v_spec, e_spec],
        out_specs=d_spec,
        out_shape=jax.ShapeDtypeStruct((2, batch * seq, D_MODEL), F32),
        scratch_shapes=[pltpu.VMEM((npair, LANES, LANES), F32)],
        compiler_params=_cparams("parallel", "parallel", "parallel", "arbitrary"),
        name="rwkv_scan",
    )(kkq, rq, ad, kdk, kend, aend, v, est)


def _rwkv_out_kernel(y_ref, bonus_ref, sz_ref, gg_ref, gb_ref, x_ref, w_ref, g_ref, o_ref, yb_ref, *, tm):
    gmat = _group_matrix(LANES)
    inv_n = 1.0 / RWKV_HEAD_DIM
    part_rows = tm // OUT_PARTS

    def groupnorm(p, cb):
        rows = slice(p * part_rows, (p + 1) * part_rows)
        sl = slice(cb * LANES, (cb + 1) * LANES)
        y = y_ref[0, rows, sl] + y_ref[1, rows, sl]
        mu = _group_sum(y, gmat) * inv_n
        yc = y - mu
        var = _group_sum(yc * yc, gmat) * inv_n
        yn = yc * lax.rsqrt(var + GN_EPS) * gg_ref[:, sl] + gb_ref[:, sl]
        yb_ref[p, :, sl] = ((yn + bonus_ref[rows, sl]) * sz_ref[rows, sl].astype(F32)).astype(BF16)

    _project_parts(groupnorm, OUT_PARTS, part_rows, yb_ref, w_ref, g_ref, x_ref, o_ref)


def _rwkv_out(y2, bonus, sz, gn_g, gn_b, x2, w_bf, layer, g):
    rows = x2.shape[0]
    tm = 512
    row_spec = pl.BlockSpec((tm, D_MODEL), lambda i: (i, 0))
    vec = pl.BlockSpec((1, D_MODEL), lambda i: (0, 0))
    return pl.pallas_call(
        functools.partial(_rwkv_out_kernel, tm=tm),
        grid=(rows // tm,),
        in_specs=[pl.BlockSpec((2, tm, D_MODEL), lambda i: (0, i, 0)), row_spec, row_spec, vec, vec, row_spec,
                  pl.BlockSpec((None, D_MODEL, D_MODEL), lambda i: (layer, 0, 0), pipeline_mode=pl.Buffered(1)), vec],
        out_specs=row_spec,
        out_shape=jax.ShapeDtypeStruct((rows, D_MODEL), F32),
        scratch_shapes=[pltpu.VMEM((OUT_PARTS, tm // OUT_PARTS, D_MODEL), BF16)],
        compiler_params=_cparams("parallel", vmem_limit_bytes=RWKV_OUT_VMEM_LIMIT_BYTES),
        name="rwkv_out",
    )(y2, bonus, sz, gn_g, gn_b, x2, w_bf, g)


def _trunk(x, p, rope):
    batch, seq, _ = x.shape
    x2 = x.reshape(batch * seq, D_MODEL)
    v_first = None
    depth = p["norm_pre"].shape[0]
    for layer in range(depth):
        j = layer // 2
        g_pre = p["norm_pre"][layer][None, :]
        g_post = p["norm_post"][layer][None, :]
        if layer % 2 == 0:
            w_in = p["att_w_in"]
            outs, lses = [], []
            for gidx, (_, dil) in enumerate(ATT_GROUPS):
                q, k, v = _att_in(x2, g_pre, w_in, j, *rope[gidx], seq, gidx, dil)
                o, lse = _attn_group(q, k, v, dil, batch, seq)
                outs.append(o)
                lses.append(lse)
            z = _gate_in(x2, g_pre, w_in, j)
            x2 = _att_out(outs, lses, z, x2, p["att_w_out"], j, g_post)
        else:
            vres = None if j == 0 else (p["rwkv_v0"][j - 1][None, :], p["rwkv_v1"][j - 1], p["rwkv_v2"][j - 1])
            kkq, rq, ad, kdk, kend, aend, est, v, bonus, sz = _rwkv_in(
                x2, seq, g_pre, p["rwkv_mu_prev"][j], p["rwkv_mu_next"][j], p["rwkv_w_in"], j,
                p["rwkv_w0"][j], p["rwkv_w1"][j], p["rwkv_w2"][j],
                p["rwkv_a0"][j], p["rwkv_a1"][j], p["rwkv_a2"][j],
                p["rwkv_k_k"][j][None, :], p["rwkv_k_a"][j][None, :], p["rwkv_r_k"][j].reshape(2, D_MODEL),
                vres, v_first)
            if j == 0:
                v_first = v
            y2 = _rwkv_scan(kkq, rq, ad, kdk, kend, aend, est, v, batch, seq)
            x2 = _rwkv_out(y2, bonus, sz, p["rwkv_gn_g"][j][None, :], p["rwkv_gn_b"][j][None, :], x2,
                           p["rwkv_w_out"], j, g_post)
    return x2.reshape(batch, seq, D_MODEL)


def kernel(x_prompt, x_sample, norm_pre, norm_post, att_w_in, att_w_out, rwkv_mu_prev, rwkv_mu_next, rwkv_w_in, rwkv_w0, rwkv_w1, rwkv_w2, rwkv_a0, rwkv_a1, rwkv_a2, rwkv_v0, rwkv_v1, rwkv_v2, rwkv_k_k, rwkv_k_a, rwkv_r_k, rwkv_gn_g, rwkv_gn_b, rwkv_w_out):
    p = dict(
        norm_pre=norm_pre, norm_post=norm_post,
        att_w_in=att_w_in.astype(BF16), att_w_out=att_w_out.astype(BF16),
        rwkv_mu_prev=rwkv_mu_prev, rwkv_mu_next=rwkv_mu_next, rwkv_w_in=rwkv_w_in.astype(BF16),
        rwkv_w0=rwkv_w0, rwkv_w1=rwkv_w1, rwkv_w2=rwkv_w2,
        rwkv_a0=rwkv_a0, rwkv_a1=rwkv_a1, rwkv_a2=rwkv_a2,
        rwkv_v0=rwkv_v0, rwkv_v1=rwkv_v1, rwkv_v2=rwkv_v2,
        rwkv_k_k=rwkv_k_k, rwkv_k_a=rwkv_k_a, rwkv_r_k=rwkv_r_k,
        rwkv_gn_g=rwkv_gn_g, rwkv_gn_b=rwkv_gn_b, rwkv_w_out=rwkv_w_out.astype(BF16),
    )
    max_seq = max(x_prompt.shape[1], x_sample.shape[1])
    rope = [_rope_tables(max_seq, dil) for _, dil in ATT_GROUPS]
    return (_trunk(x_prompt, p, rope), _trunk(x_sample, p, rope))
```
